```python
import jax, jax.numpy as jnp
from jax import lax
import numpy as np

D_MODEL = 1024
BATCH = 32
SEQ = 2048
DEPTH = 2

RWKV_HEADS = 8
RWKV_HEAD_DIM = 64
RWKV_WIDTH = RWKV_HEADS * RWKV_HEAD_DIM
DECAY_LORA = 64
ICLR_LORA = 64
GATE_LORA = 128
RWKV_GN_EPS = 64e-5
GDN_HEADS = 4
GDN_HEAD_DIM = 128
GDN_WIDTH = GDN_HEADS * GDN_HEAD_DIM
CONV_WIDTH = 4
CHUNK = 64
D_FF = 2816
N_EXPERTS = 8
TOP_K = 2
NORM_EPS = 1e-6

RWKV_SPLITS = (RWKV_WIDTH, RWKV_WIDTH, RWKV_WIDTH, DECAY_LORA, ICLR_LORA, GATE_LORA)
RWKV_COLS = sum(RWKV_SPLITS)
GDN_SPLITS = (3 * GDN_WIDTH, GDN_HEADS, GDN_HEADS, GDN_WIDTH)
GDN_COLS = sum(GDN_SPLITS)
GATE_COLS = 2 * D_MODEL
IN_COLS = RWKV_COLS + GDN_COLS + GATE_COLS
N_DENSE = (DEPTH + 1) // 2
N_MOE = DEPTH // 2

kernel_name = 'hybrid_rwkv7_gdn_moe_trunk'


def _split(t, sizes):
    return jnp.split(t, np.cumsum(sizes)[:-1].tolist(), axis=-1)


def rms_norm(x, g, eps=NORM_EPS):
    xf = x.astype(jnp.float32)
    y = xf * lax.rsqrt(jnp.mean(xf * xf, axis=-1, keepdims=True) + eps)
    return (y * g.astype(jnp.float32)).astype(x.dtype)


def l2_normalize(x, eps=1e-6):
    return x * lax.rsqrt(jnp.sum(x * x, axis=-1, keepdims=True) + eps)


def causal_depthwise_conv(x, w):
    K, C = w.shape
    return lax.conv_general_dilated(x, w[:, None, :], window_strides=(1,), padding=[(K - 1, 0)],
                                    dimension_numbers=('NWC', 'WIO', 'NWC'), feature_group_count=C)


def swiglu(h, w_gate, w_up, w_down):
    return (jax.nn.silu(h @ w_gate) * (h @ w_up)) @ w_down


def rwkv7_recurrence(r, w, k, v, a, b):
    B, T, H, N = r.shape

    def step(S, inp):
        r_t, w_t, k_t, v_t, a_t, b_t = inp
        sa = jnp.einsum('bhvk,bhk->bhv', S, a_t)
        S = S * w_t[:, :, None, :] + sa[..., None] * b_t[:, :, None, :] + v_t[..., None] * k_t[:, :, None, :]
        return S, jnp.einsum('bhvk,bhk->bhv', S, r_t)

    xs = tuple(jnp.moveaxis(t, 1, 0) for t in (r, w, k, v, a, b))
    _, o = lax.scan(step, jnp.zeros((B, H, N, N), jnp.float32), xs)
    return jnp.moveaxis(o, 0, 1)


def chunked_gated_delta_rule(q, k, v, g_log, beta):
    B, T, H, dk = q.shape
    dv = v.shape[-1]
    C = CHUNK
    NC = T // C
    q = q * (dk ** -0.5)

    def to_chunks(t):
        return t.reshape(B, NC, C, H, -1).transpose(0, 3, 1, 2, 4)

    q, k, v = to_chunks(q), to_chunks(k), to_chunks(v)
    g_log = g_log.reshape(B, NC, C, H).transpose(0, 3, 1, 2)
    beta = beta.reshape(B, NC, C, H).transpose(0, 3, 1, 2)
    gc = jnp.cumsum(g_log, axis=-1)
    causal = jnp.tril(jnp.ones((C, C), dtype=bool))
    strict = jnp.tril(jnp.ones((C, C), dtype=bool), -1)
    decay = jnp.exp(jnp.where(causal, gc[..., :, None] - gc[..., None, :], -jnp.inf))
    k_beta = k * beta[..., None]
    v_beta = v * beta[..., None]
    A = jnp.where(strict, jnp.einsum('bhncd,bhnsd->bhncs', k_beta, k) * decay, 0.0)
    L = A + jnp.eye(C, dtype=A.dtype)
    rhs = jnp.concatenate([v_beta, k_beta * jnp.exp(gc)[..., None]], axis=-1)
    sol = lax.linalg.triangular_solve(L, rhs, left_side=True, lower=True, unit_diagonal=True)
    u, w = sol[..., :dv], sol[..., dv:]
    qk = jnp.einsum('bhncd,bhnsd->bhncs', q, k) * decay

    def step(S, inp):
        q_c, k_c, u_c, w_c, qk_c, gc_c = inp
        v_new = u_c - jnp.einsum('bhcd,bhde->bhce', w_c, S)
        o = (jnp.einsum('bhcd,bhde->bhce', q_c * jnp.exp(gc_c)[..., None], S)
             + jnp.einsum('bhcs,bhse->bhce', qk_c, v_new))
        g_last = gc_c[..., -1]
        S = (S * jnp.exp(g_last)[..., None, None]
             + jnp.einsum('bhcd,bhce->bhde', k_c * jnp.exp(g_last[..., None] - gc_c)[..., None], v_new))
        return S, o

    xs = tuple(jnp.moveaxis(t, 2, 0) for t in (q, k, u, w, qk, gc))
    _, o = lax.scan(step, jnp.zeros((B, H, dk, dv), jnp.float32), xs)
    return o.transpose(1, 0, 3, 2, 4).reshape(B, T, H, dv)


def rwkv7_branch(p, mu, w0, w_decay_up, a0, w_iclr_up, w_gate_up, k_k, k_a, r_k, lnx_g, lnx_b, w_pa):
    f32 = jnp.float32
    B, T, _ = p.shape
    prev = jnp.pad(p, ((0, 0), (1, 0), (0, 0)))[:, :T]
    p = p + (prev - p) * mu
    r, k, v, wl, al, gl = _split(p, RWKV_SPLITS)
    w_log = -jax.nn.softplus(-(w0 + jnp.tanh(wl) @ w_decay_up).astype(f32)) - 0.5
    decay = jnp.exp(-jnp.exp(w_log))
    a = jax.nn.sigmoid((a0 + al @ w_iclr_up).astype(f32))
    g = jax.nn.sigmoid(gl) @ w_gate_up

    def hd(t):
        return t.astype(f32).reshape(*t.shape[:-1], RWKV_HEADS, RWKV_HEAD_DIM)

    r, k, v, decay, a = hd(r), hd(k), hd(v), hd(decay), hd(a)
    kk = l2_normalize(k * hd(k_k))
    k = k * (1.0 + (a - 1.0) * hd(k_a))
    o = rwkv7_recurrence(r, decay, k, v, -kk, kk * a)
    mean = jnp.mean(o, axis=-1, keepdims=True)
    var = jnp.mean(jnp.square(o - mean), axis=-1, keepdims=True)
    o = (o - mean) * lax.rsqrt(var + RWKV_GN_EPS) * hd(lnx_g) + hd(lnx_b)
    o = o + jnp.sum(r * k * hd(r_k), axis=-1, keepdims=True) * v
    o = o.reshape(B, T, RWKV_WIDTH).astype(p.dtype) * g
    return o @ w_pa


def gated_deltanet_branch(p, conv_w, a_log, dt_bias, gdn_norm_g, w_pb):
    f32 = jnp.float32
    B, T, _ = p.shape
    qkv, a_in, b_in, z = _split(p, GDN_SPLITS)
    qkv = jax.nn.silu(causal_depthwise_conv(qkv, conv_w))
    q, k, v = [t.astype(f32).reshape(B, T, GDN_HEADS, GDN_HEAD_DIM) for t in jnp.split(qkv, 3, axis=-1)]
    q, k = l2_normalize(q), l2_normalize(k)
    beta = jax.nn.sigmoid(b_in.astype(f32))
    g_log = -jnp.exp(a_log.astype(f32)) * jax.nn.softplus(a_in.astype(f32) + dt_bias.astype(f32))
    o = chunked_gated_delta_rule(q, k, v, g_log, beta)
    o = o * lax.rsqrt(jnp.mean(o * o, axis=-1, keepdims=True) + NORM_EPS) * gdn_norm_g.astype(f32)
    o = o.reshape(B, T, GDN_WIDTH).astype(p.dtype) * jax.nn.silu(z)
    return o @ w_pb


def hybrid_mixer(h, w_in, mu, w0, w_decay_up, a0, w_iclr_up, w_gate_up, k_k, k_a, r_k, lnx_g, lnx_b,
                 w_pa, conv_w, a_log, dt_bias, gdn_norm_g, w_pb, w_out):
    proj = h @ w_in
    p_rwkv, p_gdn, p_gate = _split(proj, (RWKV_COLS, GDN_COLS, GATE_COLS))
    y_a = rwkv7_branch(p_rwkv, mu, w0, w_decay_up, a0, w_iclr_up, w_gate_up, k_k, k_a, r_k,
                       lnx_g, lnx_b, w_pa)
    y_b = gated_deltanet_branch(p_gdn, conv_w, a_log, dt_bias, gdn_norm_g, w_pb)
    g_a, g_b = jnp.split(p_gate, 2, axis=-1)
    y = jax.nn.sigmoid(g_a) * y_a + jax.nn.sigmoid(g_b) * y_b
    return y @ w_out


def moe_swiglu(h, router, w_gate, w_up, w_down):
    logits = (h @ router).astype(jnp.float32)
    top_v, top_i = lax.top_k(logits, TOP_K)
    gates = jax.nn.softmax(top_v, axis=-1)
    combine = jnp.sum(jax.nn.one_hot(top_i, N_EXPERTS, dtype=jnp.float32) * gates[..., None], axis=-2)
    combine = combine.astype(h.dtype)
    out = jnp.zeros_like(h)
    for e in range(N_EXPERTS):
        out = out + combine[..., e:e + 1] * swiglu(h, w_gate[e], w_up[e], w_down[e])
    return out


def setup_inputs(seed: int = 0) -> dict:
    key = jax.random.key(seed)
    ks = jax.random.split(key, 32)
    f32 = jnp.float32
    L, D = DEPTH, D_MODEL

    def nrm(k, shape, scale):
        return jax.random.normal(k, shape, f32) * scale

    dt = jnp.exp(jax.random.uniform(ks[17], (L, GDN_HEADS), f32, np.log(1e-3), np.log(1e-1)))
    return {
        'x': nrm(ks[0], (BATCH, SEQ, D), 1.0),
        'norm1_g': 1.0 + nrm(ks[1], (L, D), 0.02),
        'w_in': nrm(ks[2], (L, D, IN_COLS), D ** -0.5),
        'tshift_mu': jax.random.uniform(ks[3], (L, RWKV_COLS), f32),
        'w0': jax.random.uniform(ks[4], (L, RWKV_WIDTH), f32, -6.0, 1.0),
        'w_decay_up': nrm(ks[5], (L, DECAY_LORA, RWKV_WIDTH), 0.5 * DECAY_LORA ** -0.5),
        'a0': nrm(ks[6], (L, RWKV_WIDTH), 0.5),
        'w_iclr_up': nrm(ks[7], (L, ICLR_LORA, RWKV_WIDTH), ICLR_LORA ** -0.5),
        'w_gate_up': nrm(ks[8], (L, GATE_LORA, RWKV_WIDTH), GATE_LORA ** -0.5),
        'k_k': 1.0 + nrm(ks[9], (L, RWKV_WIDTH), 0.1),
        'k_a': 1.0 + nrm(ks[10], (L, RWKV_WIDTH), 0.1),
        'r_k': nrm(ks[11], (L, RWKV_WIDTH), 0.1),
        'lnx_g': 1.0 + nrm(ks[12], (L, RWKV_WIDTH), 0.02),
        'lnx_b': nrm(ks[13], (L, RWKV_WIDTH), 0.01),
        'w_pa': nrm(ks[14], (L, RWKV_WIDTH, D), RWKV_WIDTH ** -0.5),
        'conv_w': nrm(ks[15], (L, CONV_WIDTH, 3 * GDN_WIDTH), CONV_WIDTH ** -0.5),
        'a_log': jnp.log(jax.random.uniform(ks[16], (L, GDN_HEADS), f32, 1.0, 16.0)),
        'dt_bias': dt + jnp.log(-jnp.expm1(-dt)),
        'gdn_norm_g': 1.0 + nrm(ks[18], (L, GDN_HEAD_DIM), 0.02),
        'w_pb': nrm(ks[19], (L, GDN_WIDTH, D), GDN_WIDTH ** -0.5),
        'w_out': nrm(ks[20], (L, D, D), D ** -0.5),
        'norm2_g': 1.0 + nrm(ks[21], (L, D), 0.02),
        'ffn_w_gate': nrm(ks[22], (N_DENSE, D, D_FF), D ** -0.5),
        'ffn_w_up': nrm(ks[23], (N_DENSE, D, D_FF), D ** -0.5),
        'ffn_w_down': nrm(ks[24], (N_DENSE, D_FF, D), D_FF ** -0.5),
        'moe_router': nrm(ks[25], (N_MOE, D, N_EXPERTS), D ** -0.5),
        'moe_w_gate': nrm(ks[26], (N_MOE, N_EXPERTS, D, D_FF), D ** -0.5),
        'moe_w_up': nrm(ks[27], (N_MOE, N_EXPERTS, D, D_FF), D ** -0.5),
        'moe_w_down': nrm(ks[28], (N_MOE, N_EXPERTS, D_FF, D), D_FF ** -0.5),
        'final_g': 1.0 + nrm(ks[29], (D,), 0.02),
    }


def reference(x, norm1_g, w_in, tshift_mu, w0, w_decay_up, a0, w_iclr_up, w_gate_up, k_k, k_a, r_k,
              lnx_g, lnx_b, w_pa, conv_w, a_log, dt_bias, gdn_norm_g, w_pb, w_out, norm2_g,
              ffn_w_gate, ffn_w_up, ffn_w_down, moe_router, moe_w_gate, moe_w_up, moe_w_down, final_g):
    for layer in range(DEPTH):
        h = rms_norm(x, norm1_g[layer])
        x = x + hybrid_mixer(h, w_in[layer], tshift_mu[layer], w0[layer], w_decay_up[layer], a0[layer],
                             w_iclr_up[layer], w_gate_up[layer], k_k[layer], k_a[layer], r_k[layer],
                             lnx_g[layer], lnx_b[layer], w_pa[layer], conv_w[layer], a_log[layer],
                             dt_bias[layer], gdn_norm_g[layer], w_pb[layer], w_out[layer])
        h = rms_norm(x, norm2_g[layer])
        j = layer // 2
        if layer % 2 == 0:
            x = x + swiglu(h, ffn_w_gate[j], ffn_w_up[j], ffn_w_down[j])
        else:
            x = x + moe_swiglu(h, moe_router[j], moe_w_gate[j], moe_w_up[j], moe_w_down[j])
    return rms_norm(x, final_g)
```

```python
import functools

import jax
import jax.numpy as jnp
from jax import lax
from jax.experimental import pallas as pl
from jax.experimental.pallas import tpu as pltpu

F32 = jnp.float32
BF16 = jnp.bfloat16
HI = lax.Precision.HIGHEST

D_MODEL = 1024
DEPTH = 2
RWKV_HEADS = 8
RWKV_HEAD_DIM = 64
RWKV_WIDTH = 512
DECAY_LORA = 64
ICLR_LORA = 64
GATE_LORA = 128
RWKV_GN_EPS = 64e-5
GDN_HEADS = 4
GDN_HEAD_DIM = 128
GDN_WIDTH = 512
CONV_WIDTH = 4
CHUNK = 64
D_FF = 2816
N_EXPERTS = 8
NORM_EPS = 1e-6
L2_EPS = 1e-6

LANES = 128
P_COLS = 6144
COL_QKV = 1792
COL_Z = 3328
COL_AB = 3840
COL_GATE = 4096
VMEM_LIMIT = 56 * 1024 * 1024


def _dot(a, b):
    return jnp.dot(a.astype(BF16), b.astype(BF16), preferred_element_type=F32)


def _dot_nt(a, b):
    return lax.dot_general(a.astype(BF16), b.astype(BF16), (((1,), (1,)), ((), ())),
                           preferred_element_type=F32)


def _dot_tn(a, b):
    return lax.dot_general(a.astype(BF16), b.astype(BF16), (((0,), (0,)), ((), ())),
                           preferred_element_type=F32)


def _dot_hi(a, b):
    return jnp.dot(a, b, preferred_element_type=F32, precision=HI)


def _softplus(x):
    return jnp.maximum(x, 0.0) + jnp.log(1.0 + jnp.exp(-jnp.abs(x)))


def _sigmoid(x):
    return 1.0 / (1.0 + jnp.exp(-x))


def _silu(x):
    return x * _sigmoid(x)


def _iota(shape, dim):
    return lax.broadcasted_iota(jnp.int32, shape, dim)


def _inproj_kernel(x_ref, g_ref, w_ref, p_ref, h_scr):
    @pl.when(pl.program_id(1) == 0)
    def _():
        x = x_ref[...]
        h = x * lax.rsqrt(jnp.mean(x * x, axis=-1, keepdims=True) + NORM_EPS) * g_ref[...]
        h_scr[...] = h.astype(BF16)

    p_ref[...] = jnp.dot(h_scr[...], w_ref[...], preferred_element_type=F32)


def _inproj(x2, g, w_cat, tm, tn):
    n = x2.shape[0]
    return pl.pallas_call(
        _inproj_kernel,
        grid=(n // tm, P_COLS // tn),
        in_specs=[
            pl.BlockSpec((tm, D_MODEL), lambda i, j: (i, 0)),
            pl.BlockSpec((1, D_MODEL), lambda i, j: (0, 0)),
            pl.BlockSpec((D_MODEL, tn), lambda i, j: (0, j)),
        ],
        out_specs=pl.BlockSpec((tm, tn), lambda i, j: (i, j)),
        out_shape=jax.ShapeDtypeStruct((n, P_COLS), F32),
        scratch_shapes=[pltpu.VMEM((tm, D_MODEL), BF16)],
        compiler_params=pltpu.CompilerParams(
            dimension_semantics=("parallel", "arbitrary"), vmem_limit_bytes=VMEM_LIMIT),
        name="inproj",
    )(x2, g, w_cat)


def _rwkv_kernel(r_ref, k_ref, v_ref, lo_ref, mur_ref, muk_ref, muv_ref, mulo_ref,
                 w0_ref, a0_ref, kk_ref, ka_ref, rk_ref, lg_ref, lb_ref,
                 wd_ref, wa_ref, wg_ref, o_ref):
    L = CHUNK
    seq = r_ref.shape[0]
    nc = seq // L
    W = LANES

    row = _iota((L, W), 0)
    lane = _iota((L, W), 1)
    h1 = lane < 64
    row_lo = _iota((L, 2 * W), 0)
    rm = _iota((W, W), 0)
    lm = _iota((W, W), 1)
    t_idx = rm & 63
    s_idx = lm & 63
    tri_mask = s_idx < t_idx + (rm >> 6)
    blk_mask = (rm >> 6) == (lm >> 6)
    eye = (rm == lm).astype(F32)
    bd_ones = blk_mask.astype(F32)
    rl = _iota((L, L), 0)
    cl_ = _iota((L, L), 1)
    tril = (cl_ <= rl).astype(F32)

    mur, muk, muv, mulo = mur_ref[...], muk_ref[...], muv_ref[...], mulo_ref[...]
    w0, a0, k_k, k_a, r_k = w0_ref[...], a0_ref[...], kk_ref[...], ka_ref[...], rk_ref[...]
    ln_g, ln_b = lg_ref[...], lb_ref[...]
    wd, wa, wg = wd_ref[...], wa_ref[...], wg_ref[...]

    def shift(x, prev_row, mu, rowi):
        xp = pltpu.roll(x, 1, axis=0)
        xp = jnp.where(rowi == 0, prev_row, xp)
        return x + (xp - x) * mu

    def body(c, carry):
        G, pr, pk, pv, plo = carry
        s = pl.multiple_of(c * L, L)
        r_raw = r_ref[pl.ds(s, L), :]
        k_raw = k_ref[pl.ds(s, L), :]
        v_raw = v_ref[pl.ds(s, L), :]
        lo_raw = lo_ref[pl.ds(s, L), :]
        r = shift(r_raw, pr, mur, row)
        k = shift(k_raw, pk, muk, row)
        v = shift(v_raw, pv, muv, row)
        lo = shift(lo_raw, plo, mulo, row_lo)
        wl = lo[:, 0:DECAY_LORA]
        al = lo[:, DECAY_LORA:DECAY_LORA + ICLR_LORA]
        gl = lo[:, DECAY_LORA + ICLR_LORA:]

        z = w0 + _dot(jnp.tanh(wl), wd)
        w_log = -_softplus(-z) - 0.5
        lw = -jnp.exp(w_log)
        a = _sigmoid(a0 + _dot(al, wa))
        g = _dot(_sigmoid(gl), wg)

        kk = k * k_k
        kk = kk * lax.rsqrt(_dot_hi(kk * kk, bd_ones) + L2_EPS)
        k2 = k * (1.0 + (a - 1.0) * k_a)
        ah = -kk
        bh = kk * a

        cl = _dot_hi(tril, lw)
        cl_last = cl[L - 1:L, :]
        e_pos = jnp.exp(cl)
        e_neg = jnp.exp(-cl)
        rt = r * e_pos
        at = ah * jnp.exp(cl - lw)
        bt = bh * e_neg
        kt = k2 * e_neg
        e_end = jnp.exp(cl_last - cl)
        b_end = bh * e_end
        k_end = k2 * e_end

        zero = jnp.zeros_like(at)
        lhs1 = jnp.concatenate([jnp.where(h1, at, zero), jnp.where(h1, rt, zero)], axis=0)
        lhs2 = jnp.concatenate([jnp.where(h1, zero, at), jnp.where(h1, zero, rt)], axis=0)
        m1 = jnp.where(tri_mask, _dot_nt(lhs1, jnp.concatenate([bt, kt], axis=0)), 0.0)
        m2 = jnp.where(tri_mask, _dot_nt(lhs2, jnp.concatenate([kt, bt], axis=0)), 0.0)
        a_bd = jnp.concatenate([jnp.where(h1, m1[0:L], 0.0), jnp.where(h1, 0.0, m2[0:L])], axis=0)
        ak_m = jnp.concatenate([jnp.where(h1, 0.0, m1[0:L]), jnp.where(h1, m2[0:L], 0.0)], axis=0)
        qb = jnp.where(h1, m1[L:], m2[L:])
        qk = jnp.where(h1, m2[L:], m1[L:])
        vv = jnp.concatenate([v, v], axis=0)
        vx = jnp.concatenate([jnp.where(h1, 0.0, v), jnp.where(h1, v, 0.0)], axis=0)

        pw = a_bd
        tinv = eye + a_bd
        for _ in range(5):
            pw = _dot_hi(pw, pw)
            tinv = tinv + _dot_hi(tinv, pw)

        u0 = _dot_hi(tinv, _dot(ak_m, vv))
        ta = _dot_hi(tinv, jnp.concatenate([at, at], axis=0))

        u_st = _dot_nt(ta, G) + u0
        u_bd = jnp.where(blk_mask, u_st, 0.0)
        o = _dot_nt(rt, G) + _dot(qb, u_bd) + _dot(qk, vx)
        u_w = u_bd[0:L] + u_bd[L:]
        g_new = G * jnp.exp(cl_last) + jnp.where(
            blk_mask, _dot_tn(u_w, b_end) + _dot_tn(v, k_end), 0.0)

        mean = _dot_hi(o, bd_ones) * (1.0 / 64.0)
        oc = o - mean
        var = _dot_hi(oc * oc, bd_ones) * (1.0 / 64.0)
        y = oc * lax.rsqrt(var + RWKV_GN_EPS) * ln_g + ln_b
        y = y + _dot_hi(r * k2 * r_k, bd_ones) * v
        o_ref[pl.ds(s, L), :] = y * g
        return (g_new, r_raw[L - 1:L, :], k_raw[L - 1:L, :], v_raw[L - 1:L, :], lo_raw[L - 1:L, :])

    init = (jnp.zeros((W, W), F32), jnp.zeros((1, W), F32), jnp.zeros((1, W), F32),
            jnp.zeros((1, W), F32), jnp.zeros((1, 2 * W), F32))
    lax.fori_loop(0, nc, body, init)


def _rwkv(p, seq, mu, w0, a0, k_k, k_a, r_k, ln_g, ln_b, wd, wa, wg):
    n = p.shape[0]
    nb = n // seq
    npair = RWKV_WIDTH // LANES

    def col(off):
        return pl.BlockSpec((seq, LANES), lambda b, q, off=off: (b, off + q))

    def par(off=0):
        return pl.BlockSpec((1, LANES), lambda b, q, off=off: (0, off + q))

    in_specs = [
        col(0), col(npair), col(2 * npair),
        pl.BlockSpec((seq, 2 * LANES), lambda b, q: (b, 3 * RWKV_WIDTH // (2 * LANES))),
        par(0), par(npair), par(2 * npair),
        pl.BlockSpec((1, 2 * LANES), lambda b, q: (0, 3 * RWKV_WIDTH // (2 * LANES))),
        par(), par(), par(), par(), par(), par(), par(),
        pl.BlockSpec((DECAY_LORA, LANES), lambda b, q: (0, q)),
        pl.BlockSpec((ICLR_LORA, LANES), lambda b, q: (0, q)),
        pl.BlockSpec((GATE_LORA, LANES), lambda b, q: (0, q)),
    ]
    return pl.pallas_call(
        _rwkv_kernel,
        grid=(nb, npair),
        in_specs=in_specs,
        out_specs=pl.BlockSpec((seq, LANES), lambda b, q: (b, q)),
        out_shape=jax.ShapeDtypeStruct((n, RWKV_WIDTH), F32),
        compiler_params=pltpu.CompilerParams(
            dimension_semantics=("parallel", "parallel"), vmem_limit_bytes=VMEM_LIMIT),
        name="rwkv",
    )(p, p, p, p, mu, mu, mu, mu, w0, a0, k_k, k_a, r_k, ln_g, ln_b, wd, wa, wg)


def _gdn_kernel(q_ref, k_ref, v_ref, z_ref, ab_ref, cwq_ref, cwk_ref, cwv_ref,
                alog_ref, dtb_ref, ng_ref, o_ref):
    L = CHUNK
    seq = q_ref.shape[0]
    nc = seq // L
    W = LANES
    hd = pl.program_id(1)

    row = _iota((L, W), 0)
    lane = _iota((L, W), 1)
    row8 = _iota((8, W), 0)
    rl = _iota((L, L), 0)
    cl_ = _iota((L, L), 1)
    causal = cl_ <= rl
    strict = cl_ < rl
    tril = causal.astype(F32)
    eye = (rl == cl_).astype(F32)

    lane1 = _iota((1, W), 1)
    a_log = jnp.sum(jnp.where(lane1 == hd, alog_ref[...], 0.0), axis=1, keepdims=True)
    dt_b = jnp.sum(jnp.where(lane1 == hd, dtb_ref[...], 0.0), axis=1, keepdims=True)
    neg_a = -jnp.exp(a_log)
    cwq, cwk, cwv = cwq_ref[...], cwk_ref[...], cwv_ref[...]
    ng = ng_ref[...]
    scale = GDN_HEAD_DIM ** -0.5

    def conv(x, prev8, cw):
        acc = x * cw[CONV_WIDTH - 1:CONV_WIDTH, :]
        for j in range(1, CONV_WIDTH):
            xs = pltpu.roll(x, j, axis=0)
            ps = pltpu.roll(prev8, j, axis=0)
            head = jnp.where(row8 < j, ps, xs[0:8])
            xs = jnp.concatenate([head, xs[8:]], axis=0)
            acc = acc + xs * cw[CONV_WIDTH - 1 - j:CONV_WIDTH - j, :]
        return _silu(acc)

    def l2n(x):
        return x * lax.rsqrt(jnp.sum(x * x, axis=-1, keepdims=True) + L2_EPS)

    def body(c, carry):
        S, pq, pk, pv = carry
        s = pl.multiple_of(c * L, L)
        q_raw = q_ref[pl.ds(s, L), :]
        k_raw = k_ref[pl.ds(s, L), :]
        v_raw = v_ref[pl.ds(s, L), :]
        q = l2n(conv(q_raw, pq, cwq)) * scale
        k = l2n(conv(k_raw, pk, cwk))
        v = conv(v_raw, pv, cwv)

        ab = ab_ref[pl.ds(s, L), :]
        a_in = jnp.sum(jnp.where(lane == hd, ab, 0.0), axis=1, keepdims=True)
        b_in = jnp.sum(jnp.where(lane == hd + GDN_HEADS, ab, 0.0), axis=1, keepdims=True)
        beta = _sigmoid(b_in)
        g_log = neg_a * _softplus(a_in + dt_b)
        gc = _dot_hi(tril, jnp.broadcast_to(g_log, (L, W)))
        gc_row = jnp.transpose(gc)[0:8, 0:L][0:1, :]
        gc_col = gc[:, 0:L]
        dec = jnp.exp(jnp.where(causal, gc_col - gc_row, -jnp.inf))
        kb = k * beta
        vb = v * beta
        A = jnp.where(strict, _dot_nt(kb, k) * dec, 0.0)
        pw = -A
        tinv = eye - A
        for _ in range(5):
            pw = _dot_hi(pw, pw)
            tinv = tinv + _dot_hi(tinv, pw)
        e_gc = jnp.exp(gc)
        u = _dot_hi(tinv, vb)
        w = _dot_hi(tinv, kb * e_gc)
        qk = jnp.where(causal, _dot_nt(q, k) * dec, 0.0)
        gc_last = gc[L - 1:L, :]
        k_dec = k * jnp.exp(gc_last - gc)

        v_new = u - _dot(w, S)
        o = _dot(q * e_gc, S) + _dot(qk, v_new)
        s_new = S * jnp.exp(gc_last) + _dot_tn(k_dec, v_new)

        o = o * lax.rsqrt(jnp.mean(o * o, axis=-1, keepdims=True) + NORM_EPS) * ng
        o_ref[pl.ds(s, L), :] = o * _silu(z_ref[pl.ds(s, L), :])
        return (s_new, q_raw[L - 8:, :], k_raw[L - 8:, :], v_raw[L - 8:, :])

    init = (jnp.zeros((W, W), F32), jnp.zeros((8, W), F32), jnp.zeros((8, W), F32),
            jnp.zeros((8, W), F32))
    lax.fori_loop(0, nc, body, init)


def _gdn(p, seq, conv_w, a_log, dt_bias, norm_g):
    n = p.shape[0]
    nb = n // seq
    qoff = COL_QKV // LANES

    def col(off):
        return pl.BlockSpec((seq, LANES), lambda b, h, off=off: (b, off + h))

    def cw(off):
        return pl.BlockSpec((CONV_WIDTH, LANES), lambda b, h, off=off: (0, off + h))

    one = pl.BlockSpec((1, LANES), lambda b, h: (0, 0))
    in_specs = [
        col(qoff), col(qoff + GDN_HEADS), col(qoff + 2 * GDN_HEADS), col(COL_Z // LANES),
        pl.BlockSpec((seq, LANES), lambda b, h: (b, COL_AB // LANES)),
        cw(0), cw(GDN_HEADS), cw(2 * GDN_HEADS), one, one, one,
    ]
    return pl.pallas_call(
        _gdn_kernel,
        grid=(nb, GDN_HEADS),
        in_specs=in_specs,
        out_specs=pl.BlockSpec((seq, LANES), lambda b, h: (b, h)),
        out_shape=jax.ShapeDtypeStruct((n, GDN_WIDTH), F32),
        compiler_params=pltpu.CompilerParams(
            dimension_semantics=("parallel", "parallel"), vmem_limit_bytes=VMEM_LIMIT),
        name="gdn",
    )(p, p, p, p, p, conv_w, conv_w, conv_w, a_log, dt_bias, norm_g)


def _mixout_kernel(x_ref, ya_ref, yb_ref, ga_ref, gb_ref, wpa_ref, wpb_ref, wout_ref, g2_ref,
                   *rest, with_router):
    if with_router:
        rt_ref, xo_ref, h_ref, comb_ref = rest
    else:
        xo_ref, h_ref = rest
    ya = _dot(ya_ref[...], wpa_ref[...])
    yb = _dot(yb_ref[...], wpb_ref[...])
    y = _sigmoid(ga_ref[...]) * ya + _sigmoid(gb_ref[...]) * yb
    xn = x_ref[...] + _dot(y, wout_ref[...])
    xo_ref[...] = xn
    h = xn * lax.rsqrt(jnp.mean(xn * xn, axis=-1, keepdims=True) + NORM_EPS) * g2_ref[...]
    h_ref[...] = h.astype(BF16)
    if with_router:
        logits = _dot_hi(h, rt_ref[...])
        lane = _iota(logits.shape, 1)
        neg = jnp.float32(-jnp.inf)
        logits = jnp.where(lane < N_EXPERTS, logits, neg)
        m1 = jnp.max(logits, axis=1, keepdims=True)
        i1 = jnp.min(jnp.where(logits == m1, lane, LANES), axis=1, keepdims=True)
        l2 = jnp.where(lane == i1, neg, logits)
        m2 = jnp.max(l2, axis=1, keepdims=True)
        i2 = jnp.min(jnp.where(l2 == m2, lane, LANES), axis=1, keepdims=True)
        e2 = jnp.exp(m2 - m1)
        g1 = 1.0 / (1.0 + e2)
        g2 = e2 / (1.0 + e2)
        comb_ref[...] = jnp.where(lane == i1, g1, 0.0) + jnp.where(lane == i2, g2, 0.0)


def _mixout(x2, ya, yb, p, wpa, wpb, wout, g2, router, tm):
    n = x2.shape[0]
    with_router = router is not None
    row = lambda w: pl.BlockSpec((tm, w), lambda i: (i, 0))
    full = lambda a: pl.BlockSpec(a.shape, lambda i: (0, 0))
    in_specs = [
        row(D_MODEL), row(RWKV_WIDTH), row(GDN_WIDTH),
        pl.BlockSpec((tm, D_MODEL), lambda i: (i, COL_GATE // D_MODEL)),
        pl.BlockSpec((tm, D_MODEL), lambda i: (i, COL_GATE // D_MODEL + 1)),
        full(wpa), full(wpb), full(wout), full(g2),
    ]
    args = [x2, ya, yb, p, p, wpa, wpb, wout, g2]
    out_specs = [row(D_MODEL), row(D_MODEL)]
    out_shape = [jax.ShapeDtypeStruct((n, D_MODEL), F32), jax.ShapeDtypeStruct((n, D_MODEL), BF16)]
    if with_router:
        in_specs.append(full(router))
        args.append(router)
        out_specs.append(row(LANES))
        out_shape.append(jax.ShapeDtypeStruct((n, LANES), F32))
    return pl.pallas_call(
        functools.partial(_mixout_kernel, with_router=with_router),
        grid=(n // tm,),
        in_specs=in_specs,
        out_specs=out_specs,
        out_shape=out_shape,
        compiler_params=pltpu.CompilerParams(
            dimension_semantics=("parallel",), vmem_limit_bytes=VMEM_LIMIT),
        name="mixout_router" if with_router else "mixout",
    )(*args)


def _ffn_kernel(h_ref, x_ref, *rest, routed, final):
    rest = list(rest)
    c_ref = rest.pop(0) if routed else None
    wg_ref, wu_ref, wd_ref = rest[0:3]
    rest = rest[3:]
    fg_ref = rest.pop(0) if final else None
    o_ref, acc_ref = rest
    e = pl.program_id(1)
    j = pl.program_id(2)

    @pl.when((e == 0) & (j == 0))
    def _():
        acc_ref[...] = jnp.zeros_like(acc_ref)

    h = h_ref[...]
    act = _silu(jnp.dot(h, wg_ref[0], preferred_element_type=F32)) * jnp.dot(
        h, wu_ref[0], preferred_element_type=F32)
    if routed:
        c = c_ref[...]
        lane = _iota(c.shape, 1)
        act = act * jnp.sum(jnp.where(lane == e, c, 0.0), axis=1, keepdims=True)
    acc_ref[...] += jnp.dot(act.astype(BF16), wd_ref[0], preferred_element_type=F32)

    @pl.when((e == pl.num_programs(1) - 1) & (j == pl.num_programs(2) - 1))
    def _():
        y = x_ref[...] + acc_ref[...]
        if final:
            y = y * lax.rsqrt(jnp.mean(y * y, axis=-1, keepdims=True) + NORM_EPS) * fg_ref[...]
        o_ref[...] = y


def _ffn(h, x2, comb, wg, wu, wd, final_g, tm, tf):
    n = x2.shape[0]
    ne = wg.shape[0]
    routed = comb is not None
    final = final_g is not None
    row = lambda w: pl.BlockSpec((tm, w), lambda i, e, j: (i, 0))
    in_specs = [row(D_MODEL), row(D_MODEL)]
    args = [h, x2]
    if routed:
        in_specs.append(row(LANES))
        args.append(comb)
    in_specs += [
        pl.BlockSpec((1, D_MODEL, tf), lambda i, e, j: (e, 0, j)),
        pl.BlockSpec((1, D_MODEL, tf), lambda i, e, j: (e, 0, j)),
        pl.BlockSpec((1, tf, D_MODEL), lambda i, e, j: (e, j, 0)),
    ]
    args += [wg, wu, wd]
    if final:
        in_specs.append(pl.BlockSpec((1, D_MODEL), lambda i, e, j: (0, 0)))
        args.append(final_g)
    return pl.pallas_call(
        functools.partial(_ffn_kernel, routed=routed, final=final),
        grid=(n // tm, ne, D_FF // tf),
        in_specs=in_specs,
        out_specs=row(D_MODEL),
        out_shape=jax.ShapeDtypeStruct((n, D_MODEL), F32),
        scratch_shapes=[pltpu.VMEM((tm, D_MODEL), F32)],
        compiler_params=pltpu.CompilerParams(
            dimension_semantics=("parallel", "arbitrary", "arbitrary"),
            vmem_limit_bytes=VMEM_LIMIT),
        name="moe" if routed else "ffn",
    )(*args)


def _tile(n, pref):
    t = min(pref, n)
    while n % t:
        t //= 2
    return t


def _pad_lanes(a, width=LANES):
    return jnp.pad(a, ((0, 0), (0, width - a.shape[-1])))


def kernel(x, norm1_g, w_in, tshift_mu, w0, w_decay_up, a0, w_iclr_up, w_gate_up, k_k, k_a, r_k,
           lnx_g, lnx_b, w_pa, conv_w, a_log, dt_bias, gdn_norm_g, w_pb, w_out, norm2_g,
           ffn_w_gate, ffn_w_up, ffn_w_down, moe_router, moe_w_gate, moe_w_up, moe_w_down, final_g):
    nb, seq, d = x.shape
    n = nb * seq
    x2 = x.reshape(n, d)
    tm_in = _tile(n, 1024)
    tm = _tile(n, 512)
    rw = 1792
    gq = 3 * GDN_WIDTH
    for layer in range(DEPTH):
        wi = w_in[layer]
        ab = wi[:, rw + gq:rw + gq + 2 * GDN_HEADS]
        zc = wi[:, rw + gq + 2 * GDN_HEADS:rw + gq + 2 * GDN_HEADS + GDN_WIDTH]
        gate = wi[:, rw + gq + 2 * GDN_HEADS + GDN_WIDTH:]
        w_cat = jnp.concatenate(
            [wi[:, :rw + gq], zc, ab, jnp.zeros((d, COL_GATE - COL_AB - 2 * GDN_HEADS), F32), gate],
            axis=1).astype(BF16)
        p = _inproj(x2, norm1_g[layer][None, :], w_cat, tm_in, 1536)
        one = lambda a: a[layer][None, :]
        ya = _rwkv(p, seq, one(tshift_mu), one(w0), one(a0), one(k_k), one(k_a), one(r_k),
                   one(lnx_g), one(lnx_b), w_decay_up[layer].astype(BF16),
                   w_iclr_up[layer].astype(BF16), w_gate_up[layer].astype(BF16))
        yb = _gdn(p, seq, conv_w[layer], _pad_lanes(one(a_log)), _pad_lanes(one(dt_bias)),
                  one(gdn_norm_g))
        j = layer // 2
        routed = layer % 2 == 1
        router = _pad_lanes(moe_router[j]) if routed else None
        res = _mixout(x2, ya, yb, p, w_pa[layer].astype(BF16), w_pb[layer].astype(BF16),
                      w_out[layer].astype(BF16), one(norm2_g), router, tm)
        fg = final_g[None, :] if layer == DEPTH - 1 else None
        if routed:
            x2, h, comb = res
            x2 = _ffn(h, x2, comb, moe_w_gate[j].astype(BF16), moe_w_up[j].astype(BF16),
                      moe_w_down[j].astype(BF16), fg, tm, 1408)
        else:
            x2, h = res
            x2 = _ffn(h, x2, None, ffn_w_gate[j][None].astype(BF16), ffn_w_up[j][None].astype(BF16),
                      ffn_w_down[j][None].astype(BF16), fg, tm, 1408)
    return x2.reshape(nb, seq, d)
```

```python
import functools

import jax
import jax.numpy as jnp
from jax import lax
from jax.experimental import pallas as pl
from jax.experimental.pallas import tpu as pltpu

F32 = jnp.float32
BF16 = jnp.bfloat16
HI = lax.Precision.HIGHEST

D_MODEL = 1024
DEPTH = 2
RWKV_HEADS = 8
RWKV_HEAD_DIM = 64
RWKV_WIDTH = 512
DECAY_LORA = 64
ICLR_LORA = 64
GATE_LORA = 128
RWKV_GN_EPS = 64e-5
GDN_HEADS = 4
GDN_HEAD_DIM = 128
GDN_WIDTH = 512
CONV_WIDTH = 4
CHUNK = 64
D_FF = 2816
N_EXPERTS = 8
NORM_EPS = 1e-6
L2_EPS = 1e-6

LANES = 128
P_COLS = 6144
COL_QKV = 1792
COL_Z = 3328
COL_AB = 3840
COL_GATE = 4096
VMEM_LIMIT = 56 * 1024 * 1024


def _dot(a, b):
    return jnp.dot(a.astype(BF16), b.astype(BF16), preferred_element_type=F32)


def _dot_nt(a, b):
    return lax.dot_general(a.astype(BF16), b.astype(BF16), (((1,), (1,)), ((), ())),
                           preferred_element_type=F32)


def _dot_tn(a, b):
    return lax.dot_general(a.astype(BF16), b.astype(BF16), (((0,), (0,)), ((), ())),
                           preferred_element_type=F32)


def _dot_hi(a, b):
    return jnp.dot(a, b, preferred_element_type=F32, precision=HI)


def _softplus(x):
    return jnp.maximum(x, 0.0) + jnp.log(1.0 + jnp.exp(-jnp.abs(x)))


def _sigmoid(x):
    return 1.0 / (1.0 + jnp.exp(-x))


def _silu(x):
    return x * _sigmoid(x)


def _iota(shape, dim):
    return lax.broadcasted_iota(jnp.int32, shape, dim)


def _inproj_kernel(x_ref, g_ref, w_ref, p_ref, h_scr):
    @pl.when(pl.program_id(1) == 0)
    def _():
        x = x_ref[...]
        h = x * lax.rsqrt(jnp.mean(x * x, axis=-1, keepdims=True) + NORM_EPS) * g_ref[...]
        h_scr[...] = h.astype(BF16)

    p_ref[...] = jnp.dot(h_scr[...], w_ref[...], preferred_element_type=F32)


def _inproj(x2, g, w_cat, tm, tn):
    n = x2.shape[0]
    return pl.pallas_call(
        _inproj_kernel,
        grid=(n // tm, P_COLS // tn),
        in_specs=[
            pl.BlockSpec((tm, D_MODEL), lambda i, j: (i, 0)),
            pl.BlockSpec((1, D_MODEL), lambda i, j: (0, 0)),
            pl.BlockSpec((D_MODEL, tn), lambda i, j: (0, j)),
        ],
        out_specs=pl.BlockSpec((tm, tn), lambda i, j: (i, j)),
        out_shape=jax.ShapeDtypeStruct((n, P_COLS), F32),
        scratch_shapes=[pltpu.VMEM((tm, D_MODEL), BF16)],
        compiler_params=pltpu.CompilerParams(
            dimension_semantics=("parallel", "arbitrary"), vmem_limit_bytes=VMEM_LIMIT),
        name="inproj",
    )(x2, g, w_cat)


def _split_dot(x, m):
    hi = x.astype(BF16)
    lo = (x - hi.astype(F32)).astype(BF16)
    return (jnp.dot(hi, m, preferred_element_type=F32) + jnp.dot(lo, m, preferred_element_type=F32))


def _chunk_cumsum(x, tin):
    y = x
    sh = 1
    while sh < CHUNK:
        y = y + jnp.where(tin >= sh, pltpu.roll(y, sh, axis=0), 0.0)
        sh *= 2
    return y


def _level_masks(rm, lm):
    out = []
    for k in range(6):
        x = rm >> k
        y = lm >> k
        out.append((((x ^ y) + 2 * (1 - (x & 1))) == 1).astype(F32))
    return out


def _tri_inverse(mats, eye, masks):
    ts = [eye + a * masks[0] for a in mats]
    for k in range(1, 6):
        xs = [_dot(a * masks[k], t) for a, t in zip(mats, ts)]
        ts = [t + _dot(t, x) for t, x in zip(ts, xs)]
    return ts


def _rwkv_kernel(r_ref, k_ref, v_ref, lo_ref, mur_ref, muk_ref, muv_ref, mulo_ref,
                 w0_ref, a0_ref, kk_ref, ka_ref, rk_ref, lg_ref, lb_ref,
                 wd_ref, wa_ref, wg_ref, bd_ref, o_ref,
                 lhs_scr, u0_scr, qb_scr, ov_scr, c_scr, *, nb):
    L = CHUNK
    W = LANES
    R = nb * L
    seq = r_ref.shape[0]
    nsb = seq // R

    row_r = _iota((R, W), 0)
    tin = row_r & (L - 1)
    row_r2 = _iota((R, 2 * W), 0)
    lane = _iota((L, W), 1)
    h1 = lane < 64
    rm = _iota((W, W), 0)
    lm = _iota((W, W), 1)
    tri_mask = (lm & 63) < (rm & 63) + (rm >> 6)
    blk_mask = (rm >> 6) == (lm >> 6)
    eye = (rm == lm).astype(F32)
    masks = _level_masks(rm, lm)

    mur, muk, muv, mulo = mur_ref[...], muk_ref[...], muv_ref[...], mulo_ref[...]
    w0, a0, k_k, k_a, r_k = w0_ref[...], a0_ref[...], kk_ref[...], ka_ref[...], rk_ref[...]
    ln_g, ln_b = lg_ref[...], lb_ref[...]
    wd, wa, wg = wd_ref[...], wa_ref[...], wg_ref[...]

    def shift(x, prev_row, mu, rowi):
        xp = pltpu.roll(x, 1, axis=0)
        xp = jnp.where(rowi == 0, prev_row, xp)
        return x + (xp - x) * mu

    def body(sb, carry):
        H, pr, pk, pv, plo = carry
        s = pl.multiple_of(sb * R, R)
        r_raw = r_ref[pl.ds(s, R), :]
        k_raw = k_ref[pl.ds(s, R), :]
        v_raw = v_ref[pl.ds(s, R), :]
        lo_raw = lo_ref[pl.ds(s, R), :]
        r = shift(r_raw, pr, mur, row_r)
        k = shift(k_raw, pk, muk, row_r)
        v = shift(v_raw, pv, muv, row_r)
        lo = shift(lo_raw, plo, mulo, row_r2)
        wl = lo[:, 0:DECAY_LORA]
        al = lo[:, DECAY_LORA:DECAY_LORA + ICLR_LORA]
        gl = lo[:, DECAY_LORA + ICLR_LORA:]

        z = w0 + _dot(jnp.tanh(wl), wd)
        w_log = -_softplus(-z) - 0.5
        lw = -jnp.exp(w_log)
        a = _sigmoid(a0 + _dot(al, wa))
        g = _dot(_sigmoid(gl), wg)

        kk = k * k_k
        kk = kk * lax.rsqrt(_split_dot(kk * kk, bd_ref[...]) + L2_EPS)
        k2 = k * (1.0 + (a - 1.0) * k_a)
        ah = -kk
        bh = kk * a
        bonus = _split_dot(r * k2 * r_k, bd_ref[...]) * v

        cl = _chunk_cumsum(lw, tin)
        e_neg = jnp.exp(-cl)
        rt_all = r * jnp.exp(cl)
        at_all = ah * jnp.exp(cl - lw)
        bt_all = bh * e_neg
        kt_all = k2 * e_neg

        ch = [slice(i * L, (i + 1) * L) for i in range(nb)]
        rts = [rt_all[c] for c in ch]
        ats = [at_all[c] for c in ch]
        zero = jnp.zeros((L, W), F32)
        m1s, m2s = [], []
        for c, rt, at in zip(ch, rts, ats):
            bt, kt = bt_all[c], kt_all[c]
            lhs1 = jnp.concatenate([jnp.where(h1, at, zero), jnp.where(h1, rt, zero)], axis=0)
            lhs2 = jnp.concatenate([jnp.where(h1, zero, at), jnp.where(h1, zero, rt)], axis=0)
            m1s.append(jnp.where(tri_mask, _dot_nt(lhs1, jnp.concatenate([bt, kt], axis=0)), 0.0))
            m2s.append(jnp.where(tri_mask, _dot_nt(lhs2, jnp.concatenate([kt, bt], axis=0)), 0.0))
        a_bds = [jnp.concatenate([jnp.where(h1, m1[0:L], 0.0), jnp.where(h1, 0.0, m2[0:L])], axis=0)
                 for m1, m2 in zip(m1s, m2s)]
        tinvs = _tri_inverse(a_bds, eye, masks)
        avs, o_vs = [], []
        for c, m1, m2 in zip(ch, m1s, m2s):
            vc = v[c]
            ak_m = jnp.concatenate(
                [jnp.where(h1, 0.0, m1[0:L]), jnp.where(h1, m2[0:L], 0.0)], axis=0)
            avs.append(_dot(ak_m, jnp.concatenate([vc, vc], axis=0)))
            qk = jnp.where(h1, m2[L:], m1[L:])
            vx = jnp.concatenate([jnp.where(h1, 0.0, vc), jnp.where(h1, vc, 0.0)], axis=0)
            o_vs.append(_dot(qk, vx))
        qbs = [jnp.where(h1, m1[L:], m2[L:]) for m1, m2 in zip(m1s, m2s)]
        u0s = [_dot(t, av) for t, av in zip(tinvs, avs)]
        tas = [jnp.where(blk_mask, _dot(t, jnp.concatenate([at, at], axis=0)), 0.0)
               for t, at in zip(tinvs, ats)]
        for i, (c, ta, u0) in enumerate(zip(ch, tas, u0s)):
            cl_i = cl[c]
            cl_last = cl_i[L - 1:L, :]
            e_end = jnp.exp(cl_last - cl_i)
            b_end = bh[c] * e_end
            b_st = jnp.concatenate([jnp.where(h1, b_end, 0.0), jnp.where(h1, 0.0, b_end)], axis=0)
            u0_bd = jnp.where(blk_mask, u0, 0.0)
            m_t = eye * jnp.exp(cl_last) + _dot_tn(b_st, ta)
            lhs_scr[i] = jnp.concatenate([ta, rts[i], m_t], axis=0).astype(BF16)
            u0_scr[i] = u0_bd
            qb_scr[i] = qbs[i].astype(BF16)
            ov_scr[i] = o_vs[i]
            c_scr[i] = (_dot_tn(b_st, u0_bd)
                        + jnp.where(blk_mask, _dot_tn(k2[c] * e_end, v[c]), 0.0))

        outs = []
        for i in range(nb):
            big = jnp.dot(lhs_scr[i], H.astype(BF16), preferred_element_type=F32)
            u_bd = jnp.where(blk_mask, big[0:W], 0.0) + u0_scr[i]
            outs.append(big[W:W + L] + _dot(qb_scr[i], u_bd) + ov_scr[i])
            H = big[W + L:] + c_scr[i]

        o = jnp.concatenate(outs, axis=0)
        mean = _split_dot(o, bd_ref[...]) * (1.0 / 64.0)
        oc = o - mean
        var = _split_dot(oc * oc, bd_ref[...]) * (1.0 / 64.0)
        y = oc * lax.rsqrt(var + RWKV_GN_EPS) * ln_g + ln_b + bonus
        o_ref[pl.ds(s, R), :] = y * g
        return (H, r_raw[R - 1:R, :], k_raw[R - 1:R, :], v_raw[R - 1:R, :], lo_raw[R - 1:R, :])

    init = (jnp.zeros((W, W), F32), jnp.zeros((1, W), F32), jnp.zeros((1, W), F32),
            jnp.zeros((1, W), F32), jnp.zeros((1, 2 * W), F32))
    lax.fori_loop(0, nsb, body, init)


def _rwkv(p, seq, mu, w0, a0, k_k, k_a, r_k, ln_g, ln_b, wd, wa, wg):
    n = p.shape[0]
    nbatch = n // seq
    npair = RWKV_WIDTH // LANES
    nb = 8 if seq % (8 * CHUNK) == 0 else 1

    def col(off):
        return pl.BlockSpec((seq, LANES), lambda b, q, off=off: (b, off + q))

    def par(off=0):
        return pl.BlockSpec((1, LANES), lambda b, q, off=off: (0, off + q))

    in_specs = [
        col(0), col(npair), col(2 * npair),
        pl.BlockSpec((seq, 2 * LANES), lambda b, q: (b, 3 * RWKV_WIDTH // (2 * LANES))),
        par(0), par(npair), par(2 * npair),
        pl.BlockSpec((1, 2 * LANES), lambda b, q: (0, 3 * RWKV_WIDTH // (2 * LANES))),
        par(), par(), par(), par(), par(), par(), par(),
        pl.BlockSpec((DECAY_LORA, LANES), lambda b, q: (0, q)),
        pl.BlockSpec((ICLR_LORA, LANES), lambda b, q: (0, q)),
        pl.BlockSpec((GATE_LORA, LANES), lambda b, q: (0, q)),
        pl.BlockSpec((LANES, LANES), lambda b, q: (0, 0)),
    ]
    head_id = jnp.arange(LANES) // RWKV_HEAD_DIM
    bd_ones = (head_id[:, None] == head_id[None, :]).astype(BF16)
    return pl.pallas_call(
        functools.partial(_rwkv_kernel, nb=nb),
        grid=(nbatch, npair),
        in_specs=in_specs,
        out_specs=pl.BlockSpec((seq, LANES), lambda b, q: (b, q)),
        out_shape=jax.ShapeDtypeStruct((n, RWKV_WIDTH), F32),
        scratch_shapes=[
            pltpu.VMEM((nb, 2 * LANES + CHUNK, LANES), BF16),
            pltpu.VMEM((nb, LANES, LANES), F32),
            pltpu.VMEM((nb, CHUNK, LANES), BF16),
            pltpu.VMEM((nb, CHUNK, LANES), F32),
            pltpu.VMEM((nb, LANES, LANES), F32),
        ],
        compiler_params=pltpu.CompilerParams(
            dimension_semantics=("parallel", "parallel"), vmem_limit_bytes=VMEM_LIMIT),
        name="rwkv",
    )(p, p, p, p, mu, mu, mu, mu, w0, a0, k_k, k_a, r_k, ln_g, ln_b, wd, wa, wg, bd_ones)


def _gdn_kernel(q_ref, k_ref, v_ref, z_ref, ab_ref, cwq_ref, cwk_ref, cwv_ref,
                alog_ref, dtb_ref, ng_ref, o_ref, lhs_scr, u_scr, qk_scr, c_scr, *, nb):
    L = CHUNK
    W = LANES
    P = 2 * L
    R = nb * P
    seq = q_ref.shape[0]
    nsb = seq // R
    hd = pl.program_id(1)

    row_r = _iota((R, W), 0)
    lane_r = _iota((R, W), 1)
    tin = row_r & (L - 1)
    row8 = _iota((8, W), 0)
    rm = _iota((W, W), 0)
    lm = _iota((W, W), 1)
    dif = rm - lm
    tin_m = rm & (L - 1)
    eye = (rm == lm).astype(F32)
    masks = _level_masks(rm, lm)

    lane1 = _iota((1, W), 1)
    a_log = jnp.sum(jnp.where(lane1 == hd, alog_ref[...], 0.0), axis=1, keepdims=True)
    dt_b = jnp.sum(jnp.where(lane1 == hd, dtb_ref[...], 0.0), axis=1, keepdims=True)
    neg_a = -jnp.exp(a_log)
    cwq, cwk, cwv = cwq_ref[...], cwk_ref[...], cwv_ref[...]
    ng = ng_ref[...]
    scale = GDN_HEAD_DIM ** -0.5
    neg_inf = jnp.float32(-jnp.inf)

    def conv(x, prev8, cw):
        acc = x * cw[CONV_WIDTH - 1:CONV_WIDTH, :]
        for j in range(1, CONV_WIDTH):
            xs = pltpu.roll(x, j, axis=0)
            ps = pltpu.roll(prev8, j, axis=0)
            head = jnp.where(row8 < j, ps, xs[0:8])
            xs = jnp.concatenate([head, xs[8:]], axis=0)
            acc = acc + xs * cw[CONV_WIDTH - 1 - j:CONV_WIDTH - j, :]
        return _silu(acc)

    def l2n(x):
        return x * lax.rsqrt(jnp.sum(x * x, axis=-1, keepdims=True) + L2_EPS)

    def body(sb, carry):
        S, pq, pk, pv = carry
        s = pl.multiple_of(sb * R, R)
        q_raw = q_ref[pl.ds(s, R), :]
        k_raw = k_ref[pl.ds(s, R), :]
        v_raw = v_ref[pl.ds(s, R), :]
        q = l2n(conv(q_raw, pq, cwq)) * scale
        k = l2n(conv(k_raw, pk, cwk))
        v = conv(v_raw, pv, cwv)

        ab = ab_ref[pl.ds(s, R), :]
        a_in = jnp.sum(jnp.where(lane_r == hd, ab, 0.0), axis=1, keepdims=True)
        b_in = jnp.sum(jnp.where(lane_r == hd + GDN_HEADS, ab, 0.0), axis=1, keepdims=True)
        beta = _sigmoid(b_in)
        g_log = neg_a * _softplus(a_in + dt_b)
        gc = _chunk_cumsum(jnp.broadcast_to(g_log, (R, W)), tin)
        e_gc = jnp.exp(gc)
        kb = k * beta
        vb = v * beta
        kbe = kb * e_gc
        qe = q * e_gc

        prs = [slice(i * P, (i + 1) * P) for i in range(nb)]
        decs = []
        for sl in prs:
            gc_p = gc[sl]
            dlog = gc_p - jnp.transpose(gc_p)
            dlog = jnp.where(dif >= 0, dlog, neg_inf)
            decs.append(jnp.exp(jnp.where(dif <= tin_m, dlog, neg_inf)))
        negAs = [jnp.where(dif > 0, _dot_nt(kb[sl], k[sl]) * dec, 0.0) * -1.0
                 for sl, dec in zip(prs, decs)]
        qks = [_dot_nt(q[sl], k[sl]) * dec for sl, dec in zip(prs, decs)]
        tinvs = _tri_inverse(negAs, eye, masks)
        uws = [_dot(t, jnp.concatenate([vb[sl], kbe[sl]], axis=1)) for t, sl in zip(tinvs, prs)]
        steps = []
        for sl, uw, qk in zip(prs, uws, qks):
            for j in range(2):
                cs = slice(sl.start + j * L, sl.start + (j + 1) * L)
                hs = slice(j * L, (j + 1) * L)
                gc_c = gc[cs]
                gc_last = gc_c[L - 1:L, :]
                k_dec = k[cs] * jnp.exp(gc_last - gc_c)
                u_c, w_c = uw[hs, 0:W], uw[hs, W:]
                m_c = eye * jnp.exp(gc_last) - _dot_tn(k_dec, w_c)
                n_st = len(steps)
                lhs_scr[n_st] = jnp.concatenate([w_c, qe[cs], m_c], axis=0).astype(BF16)
                u_scr[n_st] = u_c
                qk_scr[n_st] = qk[hs].astype(BF16)
                c_scr[n_st] = _dot_tn(k_dec, u_c)
                steps.append(n_st)

        outs = []
        for i in steps:
            big = jnp.dot(lhs_scr[i], S.astype(BF16), preferred_element_type=F32)
            v_new = u_scr[i] - big[0:L]
            outs.append(big[L:2 * L] + _dot(qk_scr[i], jnp.concatenate([v_new, v_new], axis=0)))
            S = big[2 * L:] + c_scr[i]

        o = jnp.concatenate(outs, axis=0)
        o = o * lax.rsqrt(jnp.mean(o * o, axis=-1, keepdims=True) + NORM_EPS) * ng
        o_ref[pl.ds(s, R), :] = o * _silu(z_ref[pl.ds(s, R), :])
        return (S, q_raw[R - 8:, :], k_raw[R - 8:, :], v_raw[R - 8:, :])

    init = (jnp.zeros((W, W), F32), jnp.zeros((8, W), F32), jnp.zeros((8, W), F32),
            jnp.zeros((8, W), F32))
    lax.fori_loop(0, nsb, body, init)


def _gdn(p, seq, conv_w, a_log, dt_bias, norm_g):
    n = p.shape[0]
    nbatch = n // seq
    qoff = COL_QKV // LANES
    nb = 4 if seq % (8 * CHUNK) == 0 else 1

    def col(off):
        return pl.BlockSpec((seq, LANES), lambda b, h, off=off: (b, off + h))

    def cw(off):
        return pl.BlockSpec((CONV_WIDTH, LANES), lambda b, h, off=off: (0, off + h))

    one = pl.BlockSpec((1, LANES), lambda b, h: (0, 0))
    in_specs = [
        col(qoff), col(qoff + GDN_HEADS), col(qoff + 2 * GDN_HEADS), col(COL_Z // LANES),
        pl.BlockSpec((seq, LANES), lambda b, h: (b, COL_AB // LANES)),
        cw(0), cw(GDN_HEADS), cw(2 * GDN_HEADS), one, one, one,
    ]
    return pl.pallas_call(
        functools.partial(_gdn_kernel, nb=nb),
        grid=(nbatch, GDN_HEADS),
        in_specs=in_specs,
        out_specs=pl.BlockSpec((seq, LANES), lambda b, h: (b, h)),
        out_shape=jax.ShapeDtypeStruct((n, GDN_WIDTH), F32),
        scratch_shapes=[
            pltpu.VMEM((2 * nb, 2 * CHUNK + LANES, LANES), BF16),
            pltpu.VMEM((2 * nb, CHUNK, LANES), F32),
            pltpu.VMEM((2 * nb, CHUNK, LANES), BF16),
            pltpu.VMEM((2 * nb, LANES, LANES), F32),
        ],
        compiler_params=pltpu.CompilerParams(
            dimension_semantics=("parallel", "parallel"), vmem_limit_bytes=VMEM_LIMIT),
        name="gdn",
    )(p, p, p, p, p, conv_w, conv_w, conv_w, a_log, dt_bias, norm_g)


def _mixout_kernel(x_ref, ya_ref, yb_ref, ga_ref, gb_ref, wpa_ref, wpb_ref, wout_ref, g2_ref,
                   *rest, with_router):
    if with_router:
        rt_ref, xo_ref, h_ref, comb_ref = rest
    else:
        xo_ref, h_ref = rest
    ya = _dot(ya_ref[...], wpa_ref[...])
    yb = _dot(yb_ref[...], wpb_ref[...])
    y = _sigmoid(ga_ref[...]) * ya + _sigmoid(gb_ref[...]) * yb
    xn = x_ref[...] + _dot(y, wout_ref[...])
    xo_ref[...] = xn
    h = xn * lax.rsqrt(jnp.mean(xn * xn, axis=-1, keepdims=True) + NORM_EPS) * g2_ref[...]
    h_ref[...] = h.astype(BF16)
    if with_router:
        logits = _dot_hi(h, rt_ref[...])
        lane = _iota(logits.shape, 1)
        neg = jnp.float32(-jnp.inf)
        logits = jnp.where(lane < N_EXPERTS, logits, neg)
        m1 = jnp.max(logits, axis=1, keepdims=True)
        i1 = jnp.min(jnp.where(logits == m1, lane, LANES), axis=1, keepdims=True)
        l2 = jnp.where(lane == i1, neg, logits)
        m2 = jnp.max(l2, axis=1, keepdims=True)
        i2 = jnp.min(jnp.where(l2 == m2, lane, LANES), axis=1, keepdims=True)
        e2 = jnp.exp(m2 - m1)
        g1 = 1.0 / (1.0 + e2)
        g2 = e2 / (1.0 + e2)
        comb_ref[...] = jnp.where(lane == i1, g1, 0.0) + jnp.where(lane == i2, g2, 0.0)


def _mixout(x2, ya, yb, p, wpa, wpb, wout, g2, router, tm):
    n = x2.shape[0]
    with_router = router is not None
    row = lambda w: pl.BlockSpec((tm, w), lambda i: (i, 0))
    full = lambda a: pl.BlockSpec(a.shape, lambda i: (0, 0))
    in_specs = [
        row(D_MODEL), row(RWKV_WIDTH), row(GDN_WIDTH),
        pl.BlockSpec((tm, D_MODEL), lambda i: (i, COL_GATE // D_MODEL)),
        pl.BlockSpec((tm, D_MODEL), lambda i: (i, COL_GATE // D_MODEL + 1)),
        full(wpa), full(wpb), full(wout), full(g2),
    ]
    args = [x2, ya, yb, p, p, wpa, wpb, wout, g2]
    out_specs = [row(D_MODEL), row(D_MODEL)]
    out_shape = [jax.ShapeDtypeStruct((n, D_MODEL), F32), jax.ShapeDtypeStruct((n, D_MODEL), BF16)]
    if with_router:
        in_specs.append(full(router))
        args.append(router)
        out_specs.append(row(LANES))
        out_shape.append(jax.ShapeDtypeStruct((n, LANES), F32))
    return pl.pallas_call(
        functools.partial(_mixout_kernel, with_router=with_router),
        grid=(n // tm,),
        in_specs=in_specs,
        out_specs=out_specs,
        out_shape=out_shape,
        compiler_params=pltpu.CompilerParams(
            dimension_semantics=("parallel",), vmem_limit_bytes=VMEM_LIMIT),
        name="mixout_router" if with_router else "mixout",
    )(*args)


def _ffn_kernel(h_ref, x_ref, *rest, routed, final):
    rest = list(rest)
    c_ref = rest.pop(0) if routed else None
    wg_ref, wu_ref, wd_ref = rest[0:3]
    rest = rest[3:]
    fg_ref = rest.pop(0) if final else None
    o_ref, acc_ref = rest
    e = pl.program_id(1)
    j = pl.program_id(2)

    @pl.when((e == 0) & (j == 0))
    def _():
        acc_ref[...] = jnp.zeros_like(acc_ref)

    h = h_ref[...]
    act = _silu(jnp.dot(h, wg_ref[0], preferred_element_type=F32)) * jnp.dot(
        h, wu_ref[0], preferred_element_type=F32)
    if routed:
        c = c_ref[...]
        lane = _iota(c.shape, 1)
        act = act * jnp.sum(jnp.where(lane == e, c, 0.0), axis=1, keepdims=True)
    acc_ref[...] += jnp.dot(act.astype(BF16), wd_ref[0], preferred_element_type=F32)

    @pl.when((e == pl.num_programs(1) - 1) & (j == pl.num_programs(2) - 1))
    def _():
        y = x_ref[...] + acc_ref[...]
        if final:
            y = y * lax.rsqrt(jnp.mean(y * y, axis=-1, keepdims=True) + NORM_EPS) * fg_ref[...]
        o_ref[...] = y


def _ffn(h, x2, comb, wg, wu, wd, final_g, tm, tf):
    n = x2.shape[0]
    ne = wg.shape[0]
    routed = comb is not None
    final = final_g is not None
    row = lambda w: pl.BlockSpec((tm, w), lambda i, e, j: (i, 0))
    in_specs = [row(D_MODEL), row(D_MODEL)]
    args = [h, x2]
    if routed:
        in_specs.append(row(LANES))
        args.append(comb)
    in_specs += [
        pl.BlockSpec((1, D_MODEL, tf), lambda i, e, j: (e, 0, j)),
        pl.BlockSpec((1, D_MODEL, tf), lambda i, e, j: (e, 0, j)),
        pl.BlockSpec((1, tf, D_MODEL), lambda i, e, j: (e, j, 0)),
    ]
    args += [wg, wu, wd]
    if final:
        in_specs.append(pl.BlockSpec((1, D_MODEL), lambda i, e, j: (0, 0)))
        args.append(final_g)
    return pl.pallas_call(
        functools.partial(_ffn_kernel, routed=routed, final=final),
        grid=(n // tm, ne, D_FF // tf),
        in_specs=in_specs,
        out_specs=row(D_MODEL),
        out_shape=jax.ShapeDtypeStruct((n, D_MODEL), F32),
        scratch_shapes=[pltpu.VMEM((tm, D_MODEL), F32)],
        compiler_params=pltpu.CompilerParams(
            dimension_semantics=("parallel", "arbitrary", "arbitrary"),
            vmem_limit_bytes=VMEM_LIMIT),
        name="moe" if routed else "ffn",
    )(*args)


def _tile(n, pref):
    t = min(pref, n)
    while n % t:
        t //= 2
    return t


def _pad_lanes(a, width=LANES):
    return jnp.pad(a, ((0, 0), (0, width - a.shape[-1])))


def kernel(x, norm1_g, w_in, tshift_mu, w0, w_decay_up, a0, w_iclr_up, w_gate_up, k_k, k_a, r_k,
           lnx_g, lnx_b, w_pa, conv_w, a_log, dt_bias, gdn_norm_g, w_pb, w_out, norm2_g,
           ffn_w_gate, ffn_w_up, ffn_w_down, moe_router, moe_w_gate, moe_w_up, moe_w_down, final_g):
    nb, seq, d = x.shape
    n = nb * seq
    x2 = x.reshape(n, d)
    tm_in = _tile(n, 1024)
    tm = _tile(n, 512)
    rw = 1792
    gq = 3 * GDN_WIDTH
    for layer in range(DEPTH):
        wi = w_in[layer]
        ab = wi[:, rw + gq:rw + gq + 2 * GDN_HEADS]
        zc = wi[:, rw + gq + 2 * GDN_HEADS:rw + gq + 2 * GDN_HEADS + GDN_WIDTH]
        gate = wi[:, rw + gq + 2 * GDN_HEADS + GDN_WIDTH:]
        w_cat = jnp.concatenate(
            [wi[:, :rw + gq], zc, ab, jnp.zeros((d, COL_GATE - COL_AB - 2 * GDN_HEADS), F32), gate],
            axis=1).astype(BF16)
        p = _inproj(x2, norm1_g[layer][None, :], w_cat, tm_in, 1536)
        one = lambda a: a[layer][None, :]
        ya = _rwkv(p, seq, one(tshift_mu), one(w0), one(a0), one(k_k), one(k_a), one(r_k),
                   one(lnx_g), one(lnx_b), w_decay_up[layer].astype(BF16),
                   w_iclr_up[layer].astype(BF16), w_gate_up[layer].astype(BF16))
        yb = _gdn(p, seq, conv_w[layer], _pad_lanes(one(a_log)), _pad_lanes(one(dt_bias)),
                  one(gdn_norm_g))
        j = layer // 2
        routed = layer % 2 == 1
        router = _pad_lanes(moe_router[j]) if routed else None
        res = _mixout(x2, ya, yb, p, w_pa[layer].astype(BF16), w_pb[layer].astype(BF16),
                      w_out[layer].astype(BF16), one(norm2_g), router, tm)
        fg = final_g[None, :] if layer == DEPTH - 1 else None
        if routed:
            x2, h, comb = res
            x2 = _ffn(h, x2, comb, moe_w_gate[j].astype(BF16), moe_w_up[j].astype(BF16),
                      moe_w_down[j].astype(BF16), fg, tm, 1408)
        else:
            x2, h = res
            x2 = _ffn(h, x2, None, ffn_w_gate[j][None].astype(BF16), ffn_w_up[j][None].astype(BF16),
                      ffn_w_down[j][None].astype(BF16), fg, tm, 1408)
    return x2.reshape(nb, seq, d)
```

```python
import functools

import jax
import jax.numpy as jnp
from jax import lax
from jax.experimental import pallas as pl
from jax.experimental.pallas import tpu as pltpu

F32 = jnp.float32
BF16 = jnp.bfloat16
HI = lax.Precision.HIGHEST

D_MODEL = 1024
DEPTH = 2
RWKV_HEADS = 8
RWKV_HEAD_DIM = 64
RWKV_WIDTH = 512
DECAY_LORA = 64
ICLR_LORA = 64
GATE_LORA = 128
RWKV_GN_EPS = 64e-5
GDN_HEADS = 4
GDN_HEAD_DIM = 128
GDN_WIDTH = 512
CONV_WIDTH = 4
CHUNK = 64
D_FF = 2816
N_EXPERTS = 8
NORM_EPS = 1e-6
L2_EPS = 1e-6

LANES = 128
P_COLS = 6144
COL_QKV = 1792
COL_Z = 3328
COL_AB = 3840
COL_GATE = 4096
VMEM_LIMIT = 56 * 1024 * 1024


def _dot(a, b):
    return jnp.dot(a.astype(BF16), b.astype(BF16), preferred_element_type=F32)


def _dot_nt(a, b):
    return lax.dot_general(a.astype(BF16), b.astype(BF16), (((1,), (1,)), ((), ())),
                           preferred_element_type=F32)


def _dot_tn(a, b):
    return lax.dot_general(a.astype(BF16), b.astype(BF16), (((0,), (0,)), ((), ())),
                           preferred_element_type=F32)


def _dot_hi(a, b):
    return jnp.dot(a, b, preferred_element_type=F32, precision=HI)


def _softplus(x):
    return jnp.maximum(x, 0.0) + jnp.log(1.0 + jnp.exp(-jnp.abs(x)))


def _sigmoid(x):
    return 1.0 / (1.0 + jnp.exp(-x))


def _silu(x):
    return x * _sigmoid(x)


def _iota(shape, dim):
    return lax.broadcasted_iota(jnp.int32, shape, dim)


def _inproj_kernel(x_ref, g_ref, w_ref, p_ref, h_scr):
    @pl.when(pl.program_id(1) == 0)
    def _():
        x = x_ref[...]
        h = x * lax.rsqrt(jnp.mean(x * x, axis=-1, keepdims=True) + NORM_EPS) * g_ref[...]
        h_scr[...] = h.astype(BF16)

    p_ref[...] = jnp.dot(h_scr[...], w_ref[...], preferred_element_type=F32)


def _inproj(x2, g, w_cat, tm, tn):
    n = x2.shape[0]
    return pl.pallas_call(
        _inproj_kernel,
        grid=(n // tm, P_COLS // tn),
        in_specs=[
            pl.BlockSpec((tm, D_MODEL), lambda i, j: (i, 0)),
            pl.BlockSpec((1, D_MODEL), lambda i, j: (0, 0)),
            pl.BlockSpec((D_MODEL, tn), lambda i, j: (0, j)),
        ],
        out_specs=pl.BlockSpec((tm, tn), lambda i, j: (i, j)),
        out_shape=jax.ShapeDtypeStruct((n, P_COLS), F32),
        scratch_shapes=[pltpu.VMEM((tm, D_MODEL), BF16)],
        compiler_params=pltpu.CompilerParams(
            dimension_semantics=("parallel", "arbitrary"), vmem_limit_bytes=VMEM_LIMIT),
        name="inproj",
    )(x2, g, w_cat)


def _split_dot(x, m):
    hi = x.astype(BF16)
    lo = (x - hi.astype(F32)).astype(BF16)
    return (jnp.dot(hi, m, preferred_element_type=F32) + jnp.dot(lo, m, preferred_element_type=F32))


def _chunk_cumsum(x, tin):
    y = x
    sh = 1
    while sh < CHUNK:
        y = y + jnp.where(tin >= sh, pltpu.roll(y, sh, axis=0), 0.0)
        sh *= 2
    return y


def _level_masks(rm, lm):
    out = []
    for k in range(6):
        x = rm >> k
        y = lm >> k
        out.append((((x ^ y) + 2 * (1 - (x & 1))) == 1).astype(F32))
    return out


def _tri_inverse(mats, eye, masks):
    ts = [eye + a * masks[0] for a in mats]
    for k in range(1, 6):
        xs = [_dot(a * masks[k], t) for a, t in zip(mats, ts)]
        ts = [t + _dot(t, x) for t, x in zip(ts, xs)]
    return ts


def _rwkv_kernel(r_ref, k_ref, v_ref, lo_ref, mur_ref, muk_ref, muv_ref, mulo_ref,
                 w0_ref, a0_ref, kk_ref, ka_ref, rk_ref, lg_ref, lb_ref,
                 wd_ref, wa_ref, wg_ref, bd_ref, o_ref,
                 lhs_scr, u0_scr, qb_scr, ov_scr, c_scr, *, nb):
    L = CHUNK
    W = LANES
    R = nb * L
    seq = r_ref.shape[0]
    nsb = seq // R

    row_r = _iota((R, W), 0)
    tin = row_r & (L - 1)
    row_r2 = _iota((R, 2 * W), 0)
    lane = _iota((L, W), 1)
    h1 = lane < 64
    rm = _iota((W, W), 0)
    lm = _iota((W, W), 1)
    tri_mask = (lm & 63) < (rm & 63) + (rm >> 6)
    blk_mask = (rm >> 6) == (lm >> 6)
    eye = (rm == lm).astype(F32)
    masks = _level_masks(rm, lm)

    mur, muk, muv, mulo = mur_ref[...], muk_ref[...], muv_ref[...], mulo_ref[...]
    w0, a0, k_k, k_a, r_k = w0_ref[...], a0_ref[...], kk_ref[...], ka_ref[...], rk_ref[...]
    ln_g, ln_b = lg_ref[...], lb_ref[...]
    wd, wa, wg = wd_ref[...], wa_ref[...], wg_ref[...]

    def shift(x, prev_row, mu, rowi):
        xp = pltpu.roll(x, 1, axis=0)
        xp = jnp.where(rowi == 0, prev_row, xp)
        return x + (xp - x) * mu

    def body(sb, carry):
        H, pr, pk, pv, plo = carry
        s = pl.multiple_of(sb * R, R)
        r_raw = r_ref[pl.ds(s, R), :]
        k_raw = k_ref[pl.ds(s, R), :]
        v_raw = v_ref[pl.ds(s, R), :]
        lo_raw = lo_ref[pl.ds(s, R), :]
        r = shift(r_raw, pr, mur, row_r)
        k = shift(k_raw, pk, muk, row_r)
        v = shift(v_raw, pv, muv, row_r)
        lo = shift(lo_raw, plo, mulo, row_r2)
        wl = lo[:, 0:DECAY_LORA]
        al = lo[:, DECAY_LORA:DECAY_LORA + ICLR_LORA]
        gl = lo[:, DECAY_LORA + ICLR_LORA:]

        z = w0 + _dot(jnp.tanh(wl), wd)
        w_log = -_softplus(-z) - 0.5
        lw = -jnp.exp(w_log)
        a = _sigmoid(a0 + _dot(al, wa))
        g = _dot(_sigmoid(gl), wg)

        kk = k * k_k
        kk = kk * lax.rsqrt(_split_dot(kk * kk, bd_ref[...]) + L2_EPS)
        k2 = k * (1.0 + (a - 1.0) * k_a)
        ah = -kk
        bh = kk * a
        bonus = _split_dot(r * k2 * r_k, bd_ref[...]) * v

        cl = _chunk_cumsum(lw, tin)
        e_neg = jnp.exp(-cl)
        rt_all = r * jnp.exp(cl)
        at_all = ah * jnp.exp(cl - lw)
        bt_all = bh * e_neg
        kt_all = k2 * e_neg

        ch = [slice(i * L, (i + 1) * L) for i in range(nb)]
        rts = [rt_all[c] for c in ch]
        ats = [at_all[c] for c in ch]
        zero = jnp.zeros((L, W), F32)
        m1s, m2s = [], []
        for c, rt, at in zip(ch, rts, ats):
            bt, kt = bt_all[c], kt_all[c]
            lhs1 = jnp.concatenate([jnp.where(h1, at, zero), jnp.where(h1, rt, zero)], axis=0)
            lhs2 = jnp.concatenate([jnp.where(h1, zero, at), jnp.where(h1, zero, rt)], axis=0)
            m1s.append(jnp.where(tri_mask, _dot_nt(lhs1, jnp.concatenate([bt, kt], axis=0)), 0.0))
            m2s.append(jnp.where(tri_mask, _dot_nt(lhs2, jnp.concatenate([kt, bt], axis=0)), 0.0))
        a_bds = [jnp.concatenate([jnp.where(h1, m1[0:L], 0.0), jnp.where(h1, 0.0, m2[0:L])], axis=0)
                 for m1, m2 in zip(m1s, m2s)]
        tinvs = _tri_inverse(a_bds, eye, masks)
        avs, o_vs = [], []
        for c, m1, m2 in zip(ch, m1s, m2s):
            vc = v[c]
            ak_m = jnp.concatenate(
                [jnp.where(h1, 0.0, m1[0:L]), jnp.where(h1, m2[0:L], 0.0)], axis=0)
            avs.append(_dot(ak_m, jnp.concatenate([vc, vc], axis=0)))
            qk = jnp.where(h1, m2[L:], m1[L:])
            vx = jnp.concatenate([jnp.where(h1, 0.0, vc), jnp.where(h1, vc, 0.0)], axis=0)
            o_vs.append(_dot(qk, vx))
        qbs = [jnp.where(h1, m1[L:], m2[L:]) for m1, m2 in zip(m1s, m2s)]
        u0s = [_dot(t, av) for t, av in zip(tinvs, avs)]
        tas = [jnp.where(blk_mask, _dot(t, jnp.concatenate([at, at], axis=0)), 0.0)
               for t, at in zip(tinvs, ats)]
        for i, (c, ta, u0) in enumerate(zip(ch, tas, u0s)):
            cl_i = cl[c]
            cl_last = cl_i[L - 1:L, :]
            e_end = jnp.exp(cl_last - cl_i)
            b_end = bh[c] * e_end
            b_st = jnp.concatenate([jnp.where(h1, b_end, 0.0), jnp.where(h1, 0.0, b_end)], axis=0)
            u0_bd = jnp.where(blk_mask, u0, 0.0)
            m_t = eye * jnp.exp(cl_last) + _dot_tn(b_st, ta)
            lhs_scr[i] = jnp.concatenate([ta, rts[i], m_t], axis=0).astype(BF16)
            u0_scr[i] = u0_bd
            qb_scr[i] = qbs[i].astype(BF16)
            ov_scr[i] = o_vs[i]
            c_scr[i] = (_dot_tn(b_st, u0_bd)
                        + jnp.where(blk_mask, _dot_tn(k2[c] * e_end, v[c]), 0.0))

        outs = []
        for i in range(nb):
            big = jnp.dot(lhs_scr[i], H.astype(BF16), preferred_element_type=F32)
            u_bd = jnp.where(blk_mask, big[0:W], 0.0) + u0_scr[i]
            outs.append(big[W:W + L] + _dot(qb_scr[i], u_bd) + ov_scr[i])
            H = big[W + L:] + c_scr[i]

        o = jnp.concatenate(outs, axis=0)
        mean = _split_dot(o, bd_ref[...]) * (1.0 / 64.0)
        oc = o - mean
        var = _split_dot(oc * oc, bd_ref[...]) * (1.0 / 64.0)
        y = oc * lax.rsqrt(var + RWKV_GN_EPS) * ln_g + ln_b + bonus
        o_ref[pl.ds(s, R), :] = y * g
        return (H, r_raw[R - 1:R, :], k_raw[R - 1:R, :], v_raw[R - 1:R, :], lo_raw[R - 1:R, :])

    init = (jnp.zeros((W, W), F32), jnp.zeros((1, W), F32), jnp.zeros((1, W), F32),
            jnp.zeros((1, W), F32), jnp.zeros((1, 2 * W), F32))
    lax.fori_loop(0, nsb, body, init)


def _rwkv(p, seq, mu, w0, a0, k_k, k_a, r_k, ln_g, ln_b, wd, wa, wg):
    n = p.shape[0]
    nbatch = n // seq
    npair = RWKV_WIDTH // LANES
    nb = 8 if seq % (8 * CHUNK) == 0 else 1

    def col(off):
        return pl.BlockSpec((seq, LANES), lambda b, q, off=off: (b, off + q))

    def par(off=0):
        return pl.BlockSpec((1, LANES), lambda b, q, off=off: (0, off + q))

    in_specs = [
        col(0), col(npair), col(2 * npair),
        pl.BlockSpec((seq, 2 * LANES), lambda b, q: (b, 3 * RWKV_WIDTH // (2 * LANES))),
        par(0), par(npair), par(2 * npair),
        pl.BlockSpec((1, 2 * LANES), lambda b, q: (0, 3 * RWKV_WIDTH // (2 * LANES))),
        par(), par(), par(), par(), par(), par(), par(),
        pl.BlockSpec((DECAY_LORA, LANES), lambda b, q: (0, q)),
        pl.BlockSpec((ICLR_LORA, LANES), lambda b, q: (0, q)),
        pl.BlockSpec((GATE_LORA, LANES), lambda b, q: (0, q)),
        pl.BlockSpec((LANES, LANES), lambda b, q: (0, 0)),
    ]
    head_id = jnp.arange(LANES) // RWKV_HEAD_DIM
    bd_ones = (head_id[:, None] == head_id[None, :]).astype(BF16)
    return pl.pallas_call(
        functools.partial(_rwkv_kernel, nb=nb),
        grid=(nbatch, npair),
        in_specs=in_specs,
        out_specs=pl.BlockSpec((seq, LANES), lambda b, q: (b, q)),
        out_shape=jax.ShapeDtypeStruct((n, RWKV_WIDTH), F32),
        scratch_shapes=[
            pltpu.VMEM((nb, 2 * LANES + CHUNK, LANES), BF16),
            pltpu.VMEM((nb, LANES, LANES), F32),
            pltpu.VMEM((nb, CHUNK, LANES), BF16),
            pltpu.VMEM((nb, CHUNK, LANES), F32),
            pltpu.VMEM((nb, LANES, LANES), F32),
        ],
        compiler_params=pltpu.CompilerParams(
            dimension_semantics=("parallel", "parallel"), vmem_limit_bytes=VMEM_LIMIT),
        name="rwkv",
    )(p, p, p, p, mu, mu, mu, mu, w0, a0, k_k, k_a, r_k, ln_g, ln_b, wd, wa, wg, bd_ones)


def _gdn_kernel(q_ref, k_ref, v_ref, z_ref, ab_ref, cwq_ref, cwk_ref, cwv_ref,
                alog_ref, dtb_ref, ng_ref, o_ref, lhs_scr, u_scr, qk_scr, c_scr, *, nb):
    L = CHUNK
    W = LANES
    P = 2 * L
    R = nb * P
    seq = q_ref.shape[0]
    nsb = seq // R
    hd = pl.program_id(1)

    row_r = _iota((R, W), 0)
    lane_r = _iota((R, W), 1)
    tin = row_r & (L - 1)
    row8 = _iota((8, W), 0)
    rm = _iota((W, W), 0)
    lm = _iota((W, W), 1)
    dif = rm - lm
    tin_m = rm & (L - 1)
    eye = (rm == lm).astype(F32)
    masks = _level_masks(rm, lm)

    lane1 = _iota((1, W), 1)
    a_log = jnp.sum(jnp.where(lane1 == hd, alog_ref[...], 0.0), axis=1, keepdims=True)
    dt_b = jnp.sum(jnp.where(lane1 == hd, dtb_ref[...], 0.0), axis=1, keepdims=True)
    neg_a = -jnp.exp(a_log)
    cwq, cwk, cwv = cwq_ref[...], cwk_ref[...], cwv_ref[...]
    ng = ng_ref[...]
    scale = GDN_HEAD_DIM ** -0.5
    neg_inf = jnp.float32(-jnp.inf)

    def conv(x, prev8, cw):
        acc = x * cw[CONV_WIDTH - 1:CONV_WIDTH, :]
        for j in range(1, CONV_WIDTH):
            xs = pltpu.roll(x, j, axis=0)
            ps = pltpu.roll(prev8, j, axis=0)
            head = jnp.where(row8 < j, ps, xs[0:8])
            xs = jnp.concatenate([head, xs[8:]], axis=0)
            acc = acc + xs * cw[CONV_WIDTH - 1 - j:CONV_WIDTH - j, :]
        return _silu(acc)

    def l2n(x):
        return x * lax.rsqrt(jnp.sum(x * x, axis=-1, keepdims=True) + L2_EPS)

    def body(sb, carry):
        S, pq, pk, pv = carry
        s = pl.multiple_of(sb * R, R)
        q_raw = q_ref[pl.ds(s, R), :]
        k_raw = k_ref[pl.ds(s, R), :]
        v_raw = v_ref[pl.ds(s, R), :]
        q = l2n(conv(q_raw, pq, cwq)) * scale
        k = l2n(conv(k_raw, pk, cwk))
        v = conv(v_raw, pv, cwv)

        ab = ab_ref[pl.ds(s, R), :]
        a_in = jnp.sum(jnp.where(lane_r == hd, ab, 0.0), axis=1, keepdims=True)
        b_in = jnp.sum(jnp.where(lane_r == hd + GDN_HEADS, ab, 0.0), axis=1, keepdims=True)
        beta = _sigmoid(b_in)
        g_log = neg_a * _softplus(a_in + dt_b)
        gc = _chunk_cumsum(jnp.broadcast_to(g_log, (R, W)), tin)
        e_gc = jnp.exp(gc)
        kb = k * beta
        vb = v * beta
        kbe = kb * e_gc
        qe = q * e_gc

        prs = [slice(i * P, (i + 1) * P) for i in range(nb)]
        decs = []
        for sl in prs:
            gc_p = gc[sl]
            dlog = gc_p - jnp.transpose(gc_p)
            dlog = jnp.where(dif >= 0, dlog, neg_inf)
            decs.append(jnp.exp(jnp.where(dif <= tin_m, dlog, neg_inf)))
        negAs = [jnp.where(dif > 0, _dot_nt(kb[sl], k[sl]) * dec, 0.0) * -1.0
                 for sl, dec in zip(prs, decs)]
        qks = [_dot_nt(q[sl], k[sl]) * dec for sl, dec in zip(prs, decs)]
        tinvs = _tri_inverse(negAs, eye, masks)
        uws = [_dot(t, jnp.concatenate([vb[sl], kbe[sl]], axis=1)) for t, sl in zip(tinvs, prs)]
        steps = []
        for sl, uw, qk in zip(prs, uws, qks):
            for j in range(2):
                cs = slice(sl.start + j * L, sl.start + (j + 1) * L)
                hs = slice(j * L, (j + 1) * L)
                gc_c = gc[cs]
                gc_last = gc_c[L - 1:L, :]
                k_dec = k[cs] * jnp.exp(gc_last - gc_c)
                u_c, w_c = uw[hs, 0:W], uw[hs, W:]
                m_c = eye * jnp.exp(gc_last) - _dot_tn(k_dec, w_c)
                n_st = len(steps)
                lhs_scr[n_st] = jnp.concatenate([w_c, qe[cs], m_c], axis=0).astype(BF16)
                u_scr[n_st] = u_c
                qk_scr[n_st] = qk[hs].astype(BF16)
                c_scr[n_st] = _dot_tn(k_dec, u_c)
                steps.append(n_st)

        outs = []
        for i in steps:
            big = jnp.dot(lhs_scr[i], S.astype(BF16), preferred_element_type=F32)
            v_new = u_scr[i] - big[0:L]
            outs.append(big[L:2 * L] + _dot(qk_scr[i], jnp.concatenate([v_new, v_new], axis=0)))
            S = big[2 * L:] + c_scr[i]

        o = jnp.concatenate(outs, axis=0)
        o = o * lax.rsqrt(jnp.mean(o * o, axis=-1, keepdims=True) + NORM_EPS) * ng
        o_ref[pl.ds(s, R), :] = o * _silu(z_ref[pl.ds(s, R), :])
        return (S, q_raw[R - 8:, :], k_raw[R - 8:, :], v_raw[R - 8:, :])

    init = (jnp.zeros((W, W), F32), jnp.zeros((8, W), F32), jnp.zeros((8, W), F32),
            jnp.zeros((8, W), F32))
    lax.fori_loop(0, nsb, body, init)


def _gdn(p, seq, conv_w, a_log, dt_bias, norm_g):
    n = p.shape[0]
    nbatch = n // seq
    qoff = COL_QKV // LANES
    nb = 4 if seq % (8 * CHUNK) == 0 else 1

    def col(off):
        return pl.BlockSpec((seq, LANES), lambda b, h, off=off: (b, off + h))

    def cw(off):
        return pl.BlockSpec((CONV_WIDTH, LANES), lambda b, h, off=off: (0, off + h))

    one = pl.BlockSpec((1, LANES), lambda b, h: (0, 0))
    in_specs = [
        col(qoff), col(qoff + GDN_HEADS), col(qoff + 2 * GDN_HEADS), col(COL_Z // LANES),
        pl.BlockSpec((seq, LANES), lambda b, h: (b, COL_AB // LANES)),
        cw(0), cw(GDN_HEADS), cw(2 * GDN_HEADS), one, one, one,
    ]
    return pl.pallas_call(
        functools.partial(_gdn_kernel, nb=nb),
        grid=(nbatch, GDN_HEADS),
        in_specs=in_specs,
        out_specs=pl.BlockSpec((seq, LANES), lambda b, h: (b, h)),
        out_shape=jax.ShapeDtypeStruct((n, GDN_WIDTH), F32),
        scratch_shapes=[
            pltpu.VMEM((2 * nb, 2 * CHUNK + LANES, LANES), BF16),
            pltpu.VMEM((2 * nb, CHUNK, LANES), F32),
            pltpu.VMEM((2 * nb, CHUNK, LANES), BF16),
            pltpu.VMEM((2 * nb, LANES, LANES), F32),
        ],
        compiler_params=pltpu.CompilerParams(
            dimension_semantics=("parallel", "parallel"), vmem_limit_bytes=VMEM_LIMIT),
        name="gdn",
    )(p, p, p, p, p, conv_w, conv_w, conv_w, a_log, dt_bias, norm_g)


def _mixout_kernel(x_ref, ya_ref, yb_ref, ga_ref, gb_ref, wpa_ref, wpb_ref, wout_ref, g2_ref,
                   *rest, with_router):
    if with_router:
        rt_ref, xo_ref, h_ref, comb_ref = rest
    else:
        xo_ref, h_ref = rest
    ya = _dot(ya_ref[...], wpa_ref[...])
    yb = _dot(yb_ref[...], wpb_ref[...])
    y = _sigmoid(ga_ref[...]) * ya + _sigmoid(gb_ref[...]) * yb
    xn = x_ref[...] + _dot(y, wout_ref[...])
    xo_ref[...] = xn
    h = xn * lax.rsqrt(jnp.mean(xn * xn, axis=-1, keepdims=True) + NORM_EPS) * g2_ref[...]
    h_ref[...] = h.astype(h_ref.dtype)
    if with_router:
        rt = rt_ref[...]
        h_hi = h.astype(BF16)
        h_lo = (h - h_hi.astype(F32)).astype(BF16)
        r_hi = rt.astype(BF16)
        r_lo = (rt - r_hi.astype(F32)).astype(BF16)
        logits = (jnp.dot(h_hi, r_hi, preferred_element_type=F32)
                  + jnp.dot(h_lo, r_hi, preferred_element_type=F32)
                  + jnp.dot(h_hi, r_lo, preferred_element_type=F32))
        lane = _iota(logits.shape, 1)
        neg = jnp.float32(-jnp.inf)
        logits = jnp.where(lane < N_EXPERTS, logits, neg)
        m1 = jnp.max(logits, axis=1, keepdims=True)
        i1 = jnp.min(jnp.where(logits == m1, lane, LANES), axis=1, keepdims=True)
        l2 = jnp.where(lane == i1, neg, logits)
        m2 = jnp.max(l2, axis=1, keepdims=True)
        i2 = jnp.min(jnp.where(l2 == m2, lane, LANES), axis=1, keepdims=True)
        e2 = jnp.exp(m2 - m1)
        g1 = 1.0 / (1.0 + e2)
        g2 = e2 / (1.0 + e2)
        comb_ref[...] = (jnp.where(lane == 0, i1.astype(F32), 0.0)
                         + jnp.where(lane == 1, i2.astype(F32), 0.0)
                         + jnp.where(lane == 2, g1, 0.0) + jnp.where(lane == 3, g2, 0.0))


def _mixout(x2, ya, yb, p, wpa, wpb, wout, g2, router, tm):
    n = x2.shape[0]
    with_router = router is not None
    row = lambda w: pl.BlockSpec((tm, w), lambda i: (i, 0))
    full = lambda a: pl.BlockSpec(a.shape, lambda i: (0, 0))
    in_specs = [
        row(D_MODEL), row(RWKV_WIDTH), row(GDN_WIDTH),
        pl.BlockSpec((tm, D_MODEL), lambda i: (i, COL_GATE // D_MODEL)),
        pl.BlockSpec((tm, D_MODEL), lambda i: (i, COL_GATE // D_MODEL + 1)),
        full(wpa), full(wpb), full(wout), full(g2),
    ]
    args = [x2, ya, yb, p, p, wpa, wpb, wout, g2]
    out_specs = [row(D_MODEL), row(D_MODEL)]
    out_shape = [jax.ShapeDtypeStruct((n, D_MODEL), F32),
                 jax.ShapeDtypeStruct((n, D_MODEL), F32 if with_router else BF16)]
    if with_router:
        in_specs.append(full(router))
        args.append(router)
        out_specs.append(row(LANES))
        out_shape.append(jax.ShapeDtypeStruct((n, LANES), F32))
    return pl.pallas_call(
        functools.partial(_mixout_kernel, with_router=with_router),
        grid=(n // tm,),
        in_specs=in_specs,
        out_specs=out_specs,
        out_shape=out_shape,
        compiler_params=pltpu.CompilerParams(
            dimension_semantics=("parallel",), vmem_limit_bytes=VMEM_LIMIT),
        name="mixout_router" if with_router else "mixout",
    )(*args)


def _rms_out(y, fg_ref):
    return y * lax.rsqrt(jnp.mean(y * y, axis=-1, keepdims=True) + NORM_EPS) * fg_ref[...]


def _ffn_kernel(h_ref, x_ref, wg_ref, wu_ref, wd_ref, *rest, final):
    fg_ref = rest[0] if final else None
    o_ref, acc_ref = rest[-2:]
    j = pl.program_id(1)

    @pl.when(j == 0)
    def _():
        acc_ref[...] = jnp.zeros_like(acc_ref)

    h = h_ref[...]
    act = _silu(jnp.dot(h, wg_ref[...], preferred_element_type=F32)) * jnp.dot(
        h, wu_ref[...], preferred_element_type=F32)
    acc_ref[...] += jnp.dot(act.astype(BF16), wd_ref[...], preferred_element_type=F32)

    @pl.when(j == pl.num_programs(1) - 1)
    def _():
        y = x_ref[...] + acc_ref[...]
        o_ref[...] = _rms_out(y, fg_ref) if final else y


def _ffn(h, x2, wg, wu, wd, final_g, tm, tf):
    n = x2.shape[0]
    final = final_g is not None
    row = lambda w: pl.BlockSpec((tm, w), lambda i, j: (i, 0))
    in_specs = [
        row(D_MODEL), row(D_MODEL),
        pl.BlockSpec((D_MODEL, tf), lambda i, j: (0, j)),
        pl.BlockSpec((D_MODEL, tf), lambda i, j: (0, j)),
        pl.BlockSpec((tf, D_MODEL), lambda i, j: (j, 0)),
    ]
    args = [h, x2, wg, wu, wd]
    if final:
        in_specs.append(pl.BlockSpec((1, D_MODEL), lambda i, j: (0, 0)))
        args.append(final_g)
    return pl.pallas_call(
        functools.partial(_ffn_kernel, final=final),
        grid=(n // tm, D_FF // tf),
        in_specs=in_specs,
        out_specs=row(D_MODEL),
        out_shape=jax.ShapeDtypeStruct((n, D_MODEL), F32),
        scratch_shapes=[pltpu.VMEM((tm, D_MODEL), F32)],
        compiler_params=pltpu.CompilerParams(
            dimension_semantics=("parallel", "arbitrary"), vmem_limit_bytes=VMEM_LIMIT),
        name="ffn",
    )(*args)


MOE_ROWS = 512


def _plan_kernel(sel_ref, pos_ref, tile_ref, cnt_scr, *, tm, n_tiles):
    ph = pl.program_id(0)
    i = pl.program_id(1)
    sel = sel_ref[...]
    lane = _iota((tm, LANES), 1)
    e1 = jnp.sum(jnp.where(lane == 0, sel, 0.0), axis=1, keepdims=True).astype(jnp.int32)
    e2 = jnp.sum(jnp.where(lane == 1, sel, 0.0), axis=1, keepdims=True).astype(jnp.int32)
    oh1 = lane == e1
    oh2 = lane == e2
    both = jnp.where(oh1, 1.0, 0.0) + jnp.where(oh2, 1.0, 0.0)

    @pl.when((ph == 0) & (i == 0))
    def _():
        cnt_scr[...] = jnp.zeros_like(cnt_scr)

    @pl.when(ph == 0)
    def _():
        cnt_scr[0:1, :] += jnp.sum(both, axis=0, keepdims=True)

    @pl.when((ph == 1) & (i == 0))
    def _():
        cnt = cnt_scr[0:1, :]
        padded = jnp.floor((cnt + (MOE_ROWS - 1)) * (1.0 / MOE_ROWS)) * MOE_ROWS
        rl = _iota((LANES, LANES), 0)
        cl = _iota((LANES, LANES), 1)
        upper = jnp.where(rl < cl, 1.0, 0.0)
        starts = _dot_hi(jnp.broadcast_to(padded, (8, LANES)), upper)[0:1, :]
        ends = starts + padded
        cnt_scr[0:1, :] = starts
        tl = _iota((n_tiles, LANES), 1)
        t0 = (_iota((n_tiles, LANES), 0) * MOE_ROWS).astype(F32)
        lane_ok = tl < N_EXPERTS
        owner = jnp.sum(jnp.where(lane_ok, jnp.where(t0 >= ends, 1.0, 0.0), 0.0),
                        axis=1, keepdims=True)
        used = jnp.sum(jnp.where(lane_ok, padded, 0.0), axis=1, keepdims=True)
        valid = t0[:, 0:1] < used
        last_e = jnp.max(jnp.where(lane_ok, jnp.where(padded > 0.0, tl.astype(F32), 0.0), 0.0),
                         axis=1, keepdims=True)
        ex = jnp.where(valid, jnp.minimum(owner, N_EXPERTS - 1.0), last_e)
        tile_ref[...] = (jnp.where(tl == 0, ex, 0.0)
                         + jnp.where(tl == 1, jnp.where(valid, 1.0, 0.0), 0.0)).astype(jnp.int32)

    @pl.when(ph == 1)
    def _():
        rr = _iota((tm, tm), 0)
        cc = _iota((tm, tm), 1)
        before = jnp.where(cc < rr, 1.0, 0.0).astype(BF16)
        base = cnt_scr[0:1, :] + jnp.dot(before, both.astype(BF16), preferred_element_type=F32)
        p1 = jnp.sum(jnp.where(oh1, base, 0.0), axis=1, keepdims=True)
        p2 = jnp.sum(jnp.where(oh2, base, 0.0), axis=1, keepdims=True)
        pos_ref[...] = (jnp.where(lane == 0, p1, 0.0) + jnp.where(lane == 1, p2, 0.0)
                        ).astype(jnp.int32)
        cnt_scr[0:1, :] += jnp.sum(both, axis=0, keepdims=True)


def _plan(sel, tm, n_tiles):
    n = sel.shape[0]
    return pl.pallas_call(
        functools.partial(_plan_kernel, tm=tm, n_tiles=n_tiles),
        grid=(2, n // tm),
        in_specs=[pl.BlockSpec((tm, LANES), lambda ph, i: (i, 0))],
        out_specs=[pl.BlockSpec((tm, LANES), lambda ph, i: (i * ph, 0)),
                   pl.BlockSpec((n_tiles, LANES), lambda ph, i: (0, 0))],
        out_shape=[jax.ShapeDtypeStruct((n, LANES), jnp.int32),
                   jax.ShapeDtypeStruct((n_tiles, LANES), jnp.int32)],
        scratch_shapes=[pltpu.VMEM((8, LANES), F32)],
        compiler_params=pltpu.CompilerParams(
            dimension_semantics=("arbitrary", "arbitrary"), vmem_limit_bytes=VMEM_LIMIT),
        name="moe_plan",
    )(sel)


def _row_copy(src_ref, src_row, dst_ref, dst_row, sem):
    return pltpu.make_async_copy(src_ref.at[pl.ds(src_row, 1)], dst_ref.at[pl.ds(dst_row, 1)], sem)


def _dispatch_kernel(pos_ref, h_ref, xs_in_ref, xs_ref, sem, *, tm):
    del xs_in_ref

    def start(r, c):
        _row_copy(h_ref, r, xs_ref, pos_ref[2 * r], sem).start()
        _row_copy(h_ref, r, xs_ref, pos_ref[2 * r + 1], sem).start()
        return c

    def wait(r, c):
        _row_copy(h_ref, 0, xs_ref, 0, sem).wait()
        _row_copy(h_ref, 0, xs_ref, 0, sem).wait()
        return c

    lax.fori_loop(0, tm, start, 0, unroll=8)
    lax.fori_loop(0, tm, wait, 0, unroll=8)


def _dispatch(pos_flat, h, n_rows, tm):
    n = h.shape[0]
    xs0 = jnp.zeros((n_rows, D_MODEL), F32)
    return pl.pallas_call(
        functools.partial(_dispatch_kernel, tm=tm),
        grid=(n // tm,),
        in_specs=[
            pl.BlockSpec((2 * tm,), lambda i: (i,), memory_space=pltpu.SMEM),
            pl.BlockSpec((tm, D_MODEL), lambda i: (i, 0)),
            pl.BlockSpec(memory_space=pl.ANY),
        ],
        out_specs=pl.BlockSpec(memory_space=pl.ANY),
        out_shape=jax.ShapeDtypeStruct((n_rows, D_MODEL), F32),
        scratch_shapes=[pltpu.SemaphoreType.DMA(())],
        input_output_aliases={2: 0},
        compiler_params=pltpu.CompilerParams(
            dimension_semantics=("arbitrary",), vmem_limit_bytes=VMEM_LIMIT),
        name="moe_dispatch",
    )(pos_flat, h, xs0)


def _experts_kernel(te_ref, tv_ref, xs_ref, wg_ref, wu_ref, wd_ref, ys_ref, xb_scr, acc_scr):
    t = pl.program_id(0)
    j = pl.program_id(1)
    del te_ref

    @pl.when(j == 0)
    def _():
        xb_scr[...] = xs_ref[...].astype(BF16)
        acc_scr[...] = jnp.zeros_like(acc_scr)

    @pl.when(tv_ref[t] == 1)
    def _():
        xb = xb_scr[...]
        act = _silu(jnp.dot(xb, wg_ref[0], preferred_element_type=F32)) * jnp.dot(
            xb, wu_ref[0], preferred_element_type=F32)
        acc_scr[...] += jnp.dot(act.astype(BF16), wd_ref[0], preferred_element_type=F32)

    @pl.when(j == pl.num_programs(1) - 1)
    def _():
        ys_ref[...] = acc_scr[...]


def _experts(tile_e, tile_v, xs, wg, wu, wd, tf):
    n_rows = xs.shape[0]
    grid_spec = pltpu.PrefetchScalarGridSpec(
        num_scalar_prefetch=2,
        grid=(n_rows // MOE_ROWS, D_FF // tf),
        in_specs=[
            pl.BlockSpec((MOE_ROWS, D_MODEL), lambda t, j, te, tv: (t, 0)),
            pl.BlockSpec((1, D_MODEL, tf), lambda t, j, te, tv: (te[t], 0, j)),
            pl.BlockSpec((1, D_MODEL, tf), lambda t, j, te, tv: (te[t], 0, j)),
            pl.BlockSpec((1, tf, D_MODEL), lambda t, j, te, tv: (te[t], j, 0)),
        ],
        out_specs=pl.BlockSpec((MOE_ROWS, D_MODEL), lambda t, j, te, tv: (t, 0)),
        scratch_shapes=[pltpu.VMEM((MOE_ROWS, D_MODEL), BF16), pltpu.VMEM((MOE_ROWS, D_MODEL), F32)],
    )
    return pl.pallas_call(
        _experts_kernel,
        grid_spec=grid_spec,
        out_shape=jax.ShapeDtypeStruct((n_rows, D_MODEL), F32),
        compiler_params=pltpu.CompilerParams(
            dimension_semantics=("parallel", "arbitrary"), vmem_limit_bytes=VMEM_LIMIT),
        name="moe_experts",
    )(tile_e, tile_v, xs, wg, wu, wd)


def _combine_kernel(pos_ref, x_ref, sel_ref, *rest, tm, final):
    fg_ref = rest[0] if final else None
    ys_ref, o_ref, buf, sem = rest[-4:]

    def start(r, c):
        _row_copy(ys_ref, pos_ref[2 * r], buf.at[0], r, sem).start()
        _row_copy(ys_ref, pos_ref[2 * r + 1], buf.at[1], r, sem).start()
        return c

    def wait(r, c):
        _row_copy(ys_ref, 0, buf.at[0], 0, sem).wait()
        _row_copy(ys_ref, 0, buf.at[1], 0, sem).wait()
        return c

    lax.fori_loop(0, tm, start, 0, unroll=8)
    lax.fori_loop(0, tm, wait, 0, unroll=8)
    sel = sel_ref[...]
    lane = _iota(sel.shape, 1)
    g1 = jnp.sum(jnp.where(lane == 2, sel, 0.0), axis=1, keepdims=True)
    g2 = jnp.sum(jnp.where(lane == 3, sel, 0.0), axis=1, keepdims=True)
    y = x_ref[...] + g1 * buf[0] + g2 * buf[1]
    o_ref[...] = _rms_out(y, fg_ref) if final else y


def _combine(pos_flat, x2, sel, ys, final_g, tm):
    n = x2.shape[0]
    final = final_g is not None
    in_specs = [
        pl.BlockSpec((2 * tm,), lambda i: (i,), memory_space=pltpu.SMEM),
        pl.BlockSpec((tm, D_MODEL), lambda i: (i, 0)),
        pl.BlockSpec((tm, LANES), lambda i: (i, 0)),
    ]
    args = [pos_flat, x2, sel]
    if final:
        in_specs.append(pl.BlockSpec((1, D_MODEL), lambda i: (0, 0)))
        args.append(final_g)
    in_specs.append(pl.BlockSpec(memory_space=pl.ANY))
    args.append(ys)
    return pl.pallas_call(
        functools.partial(_combine_kernel, tm=tm, final=final),
        grid=(n // tm,),
        in_specs=in_specs,
        out_specs=pl.BlockSpec((tm, D_MODEL), lambda i: (i, 0)),
        out_shape=jax.ShapeDtypeStruct((n, D_MODEL), F32),
        scratch_shapes=[pltpu.VMEM((2, tm, D_MODEL), F32), pltpu.SemaphoreType.DMA(())],
        compiler_params=pltpu.CompilerParams(
            dimension_semantics=("arbitrary",), vmem_limit_bytes=VMEM_LIMIT),
        name="moe_combine",
    )(*args)


def _moe(h, x2, sel, wg, wu, wd, final_g, tm, tf):
    n = x2.shape[0]
    n_tiles = pl.cdiv(2 * n, MOE_ROWS) + N_EXPERTS
    pos, tiles = _plan(sel, tm, n_tiles)
    pos_flat = pos[:, 0:2].reshape(2 * n)
    xs = _dispatch(pos_flat, h, n_tiles * MOE_ROWS, tm)
    ys = _experts(tiles[:, 0], tiles[:, 1], xs, wg, wu, wd, tf)
    return _combine(pos_flat, x2, sel, ys, final_g, tm)


def _tile(n, pref):
    t = min(pref, n)
    while n % t:
        t //= 2
    return t


def _pad_lanes(a, width=LANES):
    return jnp.pad(a, ((0, 0), (0, width - a.shape[-1])))


def kernel(x, norm1_g, w_in, tshift_mu, w0, w_decay_up, a0, w_iclr_up, w_gate_up, k_k, k_a, r_k,
           lnx_g, lnx_b, w_pa, conv_w, a_log, dt_bias, gdn_norm_g, w_pb, w_out, norm2_g,
           ffn_w_gate, ffn_w_up, ffn_w_down, moe_router, moe_w_gate, moe_w_up, moe_w_down, final_g):
    nb, seq, d = x.shape
    n = nb * seq
    x2 = x.reshape(n, d)
    tm_in = _tile(n, 1024)
    tm = _tile(n, 512)
    rw = 1792
    gq = 3 * GDN_WIDTH
    for layer in range(DEPTH):
        wi = w_in[layer]
        ab = wi[:, rw + gq:rw + gq + 2 * GDN_HEADS]
        zc = wi[:, rw + gq + 2 * GDN_HEADS:rw + gq + 2 * GDN_HEADS + GDN_WIDTH]
        gate = wi[:, rw + gq + 2 * GDN_HEADS + GDN_WIDTH:]
        w_cat = jnp.concatenate(
            [wi[:, :rw + gq], zc, ab, jnp.zeros((d, COL_GATE - COL_AB - 2 * GDN_HEADS), F32), gate],
            axis=1).astype(BF16)
        p = _inproj(x2, norm1_g[layer][None, :], w_cat, tm_in, 1536)
        one = lambda a: a[layer][None, :]
        ya = _rwkv(p, seq, one(tshift_mu), one(w0), one(a0), one(k_k), one(k_a), one(r_k),
                   one(lnx_g), one(lnx_b), w_decay_up[layer].astype(BF16),
                   w_iclr_up[layer].astype(BF16), w_gate_up[layer].astype(BF16))
        yb = _gdn(p, seq, conv_w[layer], _pad_lanes(one(a_log)), _pad_lanes(one(dt_bias)),
                  one(gdn_norm_g))
        j = layer // 2
        routed = layer % 2 == 1
        router = _pad_lanes(moe_router[j]) if routed else None
        res = _mixout(x2, ya, yb, p, w_pa[layer].astype(BF16), w_pb[layer].astype(BF16),
                      w_out[layer].astype(BF16), one(norm2_g), router, tm)
        fg = final_g[None, :] if layer == DEPTH - 1 else None
        if routed:
            x2, h, sel = res
            x2 = _moe(h, x2, sel, moe_w_gate[j].astype(BF16), moe_w_up[j].astype(BF16),
                      moe_w_down[j].astype(BF16), fg, tm, 1408)
        else:
            x2, h = res
            x2 = _ffn(h, x2, ffn_w_gate[j].astype(BF16), ffn_w_up[j].astype(BF16),
                      ffn_w_down[j].astype(BF16), fg, tm, 1408)
    return x2.reshape(nb, seq, d)
```

```python
import functools

import jax
import jax.numpy as jnp
from jax import lax
from jax.experimental import pallas as pl
from jax.experimental.pallas import tpu as pltpu

F32 = jnp.float32
BF16 = jnp.bfloat16
HI = lax.Precision.HIGHEST

D_MODEL = 1024
DEPTH = 2
RWKV_HEADS = 8
RWKV_HEAD_DIM = 64
RWKV_WIDTH = 512
DECAY_LORA = 64
ICLR_LORA = 64
GATE_LORA = 128
RWKV_GN_EPS = 64e-5
GDN_HEADS = 4
GDN_HEAD_DIM = 128
GDN_WIDTH = 512
CONV_WIDTH = 4
CHUNK = 64
D_FF = 2816
N_EXPERTS = 8
NORM_EPS = 1e-6
L2_EPS = 1e-6

LANES = 128
P_COLS = 6144
COL_QKV = 1792
COL_Z = 3328
COL_AB = 3840
COL_GATE = 4096
VMEM_LIMIT = 56 * 1024 * 1024


def _dot(a, b):
    return jnp.dot(a.astype(BF16), b.astype(BF16), preferred_element_type=F32)


def _dot_nt(a, b):
    return lax.dot_general(a.astype(BF16), b.astype(BF16), (((1,), (1,)), ((), ())),
                           preferred_element_type=F32)


def _dot_tn(a, b):
    return lax.dot_general(a.astype(BF16), b.astype(BF16), (((0,), (0,)), ((), ())),
                           preferred_element_type=F32)


def _dot_hi(a, b):
    return jnp.dot(a, b, preferred_element_type=F32, precision=HI)


def _softplus(x):
    return jnp.maximum(x, 0.0) + jnp.log(1.0 + jnp.exp(-jnp.abs(x)))


def _sigmoid(x):
    return 1.0 / (1.0 + jnp.exp(-x))


def _silu(x):
    return x * _sigmoid(x)


def _iota(shape, dim):
    return lax.broadcasted_iota(jnp.int32, shape, dim)


def _inproj_kernel(x_ref, g_ref, w_ref, p_ref, h_scr):
    @pl.when(pl.program_id(1) == 0)
    def _():
        x = x_ref[...]
        h = x * lax.rsqrt(jnp.mean(x * x, axis=-1, keepdims=True) + NORM_EPS) * g_ref[...]
        h_scr[...] = h.astype(BF16)

    p_ref[...] = jnp.dot(h_scr[...], w_ref[...], preferred_element_type=F32)


def _inproj(x2, g, w_cat, tm, tn):
    n = x2.shape[0]
    return pl.pallas_call(
        _inproj_kernel,
        grid=(n // tm, P_COLS // tn),
        in_specs=[
            pl.BlockSpec((tm, D_MODEL), lambda i, j: (i, 0)),
            pl.BlockSpec((1, D_MODEL), lambda i, j: (0, 0)),
            pl.BlockSpec((D_MODEL, tn), lambda i, j: (0, j)),
        ],
        out_specs=pl.BlockSpec((tm, tn), lambda i, j: (i, j)),
        out_shape=jax.ShapeDtypeStruct((n, P_COLS), F32),
        scratch_shapes=[pltpu.VMEM((tm, D_MODEL), BF16)],
        compiler_params=pltpu.CompilerParams(
            dimension_semantics=("parallel", "arbitrary"), vmem_limit_bytes=VMEM_LIMIT),
        name="inproj",
    )(x2, g, w_cat)


def _split_dot(x, m):
    hi = x.astype(BF16)
    lo = (x - hi.astype(F32)).astype(BF16)
    return (jnp.dot(hi, m, preferred_element_type=F32) + jnp.dot(lo, m, preferred_element_type=F32))


def _chunk_cumsum(x, tin):
    y = x
    sh = 1
    while sh < CHUNK:
        y = y + jnp.where(tin >= sh, pltpu.roll(y, sh, axis=0), 0.0)
        sh *= 2
    return y


def _level_masks(rm, lm):
    out = []
    for k in range(6):
        x = rm >> k
        y = lm >> k
        out.append((((x ^ y) + 2 * (1 - (x & 1))) == 1).astype(F32))
    return out


def _tri_inverse(mats, eye, masks, between=lambda: None):
    ts = [eye + a * masks[0] for a in mats]
    for k in range(1, 6):
        xs = [_dot(a * masks[k], t) for a, t in zip(mats, ts)]
        between()
        ts = [t + _dot(t, x) for t, x in zip(ts, xs)]
        between()
    return ts


def _rwkv_kernel(r_ref, k_ref, v_ref, lo_ref, mur_ref, muk_ref, muv_ref, mulo_ref,
                 w0_ref, a0_ref, kk_ref, ka_ref, rk_ref, lg_ref, lb_ref,
                 wd_ref, wa_ref, wg_ref, bd_ref, o_ref,
                 lhs_scr, u0_scr, qb_scr, ov_scr, c_scr, pre_scr, *, nb, ns):
    L = CHUNK
    W = LANES
    R = nb * L
    seq = r_ref.shape[0]
    nsb = seq // R

    row_r = _iota((R, W), 0)
    row_r2 = _iota((R, 2 * W), 0)
    row_c = _iota((L, W), 0)
    lane = _iota((L, W), 1)
    h1 = lane < 64
    rm = _iota((W, W), 0)
    lm = _iota((W, W), 1)
    tri_mask = (lm & 63) < (rm & 63) + (rm >> 6)
    blk_mask = (rm >> 6) == (lm >> 6)
    eye = (rm == lm).astype(F32)
    masks = _level_masks(rm, lm)

    mur, muk, muv, mulo = mur_ref[...], muk_ref[...], muv_ref[...], mulo_ref[...]
    w0, a0, k_k, k_a, r_k = w0_ref[...], a0_ref[...], kk_ref[...], ka_ref[...], rk_ref[...]
    ln_g, ln_b = lg_ref[...], lb_ref[...]
    wd, wa, wg = wd_ref[...], wa_ref[...], wg_ref[...]

    def shift(x, prev_row, mu, rowi):
        xp = pltpu.roll(x, 1, axis=0)
        xp = jnp.where(rowi == 0, prev_row, xp)
        return x + (xp - x) * mu

    def last_row_before(ref, s, cols):
        rows = ref[pl.ds(pl.multiple_of(jnp.maximum(s - 8, 0), 8), 8), cols]
        return rows[7:8, :] * (s > 0).astype(F32)

    keys = ("v", "k2", "bh", "cl", "g", "bonus", "rt", "at", "bt", "kt")

    class Stored:
        def __init__(self, par, j):
            self.par, self.j = par, j

        def __getitem__(self, key):
            par, j, q = self.par, self.j, keys.index(key)

            class Rows:
                def __getitem__(self, rows):
                    return pre_scr[par, j, q, rows]
            return Rows()

    def prologue_pieces(sb, par):
        sh = {}

        def shared():
            s = pl.multiple_of(sb * R, R)
            lo_raw = lo_ref[pl.ds(s, R), :]
            lo = shift(lo_raw, last_row_before(lo_ref, s, slice(None)), mulo, row_r2)
            wl = lo[:, 0:DECAY_LORA]
            al = lo[:, DECAY_LORA:DECAY_LORA + ICLR_LORA]
            gl = lo[:, DECAY_LORA + ICLR_LORA:]
            sh["z"] = w0 + _dot(jnp.tanh(wl), wd)
            sh["a"] = _sigmoid(a0 + _dot(al, wa))
            sh["g"] = _dot(_sigmoid(gl), wg)
            for j in range(ns):
                sj = slice(j * W, (j + 1) * W)
                for name, ref, mu in (("r", r_ref, mur), ("k", k_ref, muk), ("v", v_ref, muv)):
                    sh[name, j] = shift(ref[pl.ds(s, R), sj], last_row_before(ref, s, sj),
                                        mu[:, sj], row_r)

        def piece(j, c):
            sj = slice(j * W, (j + 1) * W)
            r, k, v = sh["r", j][c], sh["k", j][c], sh["v", j][c]
            w_log = -_softplus(-sh["z"][c, sj]) - 0.5
            lw = -jnp.exp(w_log)
            a = sh["a"][c, sj]
            kk = k * k_k[:, sj]
            kk = kk * lax.rsqrt(_split_dot(kk * kk, bd_ref[...]) + L2_EPS)
            k2 = k * (1.0 + (a - 1.0) * k_a[:, sj])
            bh = kk * a
            cl = _chunk_cumsum(lw, row_c)
            e_neg = jnp.exp(-cl)
            vals = dict(
                v=v, k2=k2, bh=bh, cl=cl, g=sh["g"][c, sj],
                bonus=_split_dot(r * k2 * r_k[:, sj], bd_ref[...]) * v,
                rt=r * jnp.exp(cl), at=-kk * jnp.exp(cl - lw), bt=bh * e_neg, kt=k2 * e_neg)
            for q, key in enumerate(keys):
                pre_scr[par, j, q, c] = vals[key]

        return [shared] + [functools.partial(piece, j, slice(i * L, (i + 1) * L))
                           for i in range(nb) for j in range(ns)]

    def run(par, Hs, sb, fillers):
        pre = [Stored(par, j) for j in range(ns)]

        def fill():
            if fillers:
                fillers.pop(0)()

        s = pl.multiple_of(sb * R, R)
        items = [(j, slice(i * L, (i + 1) * L)) for i in range(nb) for j in range(ns)]
        v_of = lambda it: pre[it[0]]["v"][it[1]]
        rts = [pre[j]["rt"][c] for j, c in items]
        ats = [pre[j]["at"][c] for j, c in items]
        zero = jnp.zeros((L, W), F32)
        m1s, m2s = [], []
        for (j, c), rt, at in zip(items, rts, ats):
            bt, kt = pre[j]["bt"][c], pre[j]["kt"][c]
            lhs1 = jnp.concatenate([jnp.where(h1, at, zero), jnp.where(h1, rt, zero)], axis=0)
            lhs2 = jnp.concatenate([jnp.where(h1, zero, at), jnp.where(h1, zero, rt)], axis=0)
            m1s.append(jnp.where(tri_mask, _dot_nt(lhs1, jnp.concatenate([bt, kt], axis=0)), 0.0))
            m2s.append(jnp.where(tri_mask, _dot_nt(lhs2, jnp.concatenate([kt, bt], axis=0)), 0.0))
        fill()
        a_bds = [jnp.concatenate([jnp.where(h1, m1[0:L], 0.0), jnp.where(h1, 0.0, m2[0:L])], axis=0)
                 for m1, m2 in zip(m1s, m2s)]
        tinvs = _tri_inverse(a_bds, eye, masks)
        avs, o_vs = [], []
        for it, m1, m2 in zip(items, m1s, m2s):
            vc = v_of(it)
            ak_m = jnp.concatenate(
                [jnp.where(h1, 0.0, m1[0:L]), jnp.where(h1, m2[0:L], 0.0)], axis=0)
            avs.append(_dot(ak_m, jnp.concatenate([vc, vc], axis=0)))
            qk = jnp.where(h1, m2[L:], m1[L:])
            vx = jnp.concatenate([jnp.where(h1, 0.0, vc), jnp.where(h1, vc, 0.0)], axis=0)
            o_vs.append(_dot(qk, vx))
        qbs = [jnp.where(h1, m1[L:], m2[L:]) for m1, m2 in zip(m1s, m2s)]
        u0s = [_dot(t, av) for t, av in zip(tinvs, avs)]
        tas = [jnp.where(blk_mask, _dot(t, jnp.concatenate([at, at], axis=0)), 0.0)
               for t, at in zip(tinvs, ats)]
        for i, ((j, c), ta, u0) in enumerate(zip(items, tas, u0s)):
            cl_i = pre[j]["cl"][c]
            cl_last = cl_i[L - 1:L, :]
            e_end = jnp.exp(cl_last - cl_i)
            b_end = pre[j]["bh"][c] * e_end
            b_st = jnp.concatenate([jnp.where(h1, b_end, 0.0), jnp.where(h1, 0.0, b_end)], axis=0)
            u0_bd = jnp.where(blk_mask, u0, 0.0)
            m_t = eye * jnp.exp(cl_last) + _dot_tn(b_st, ta)
            lhs_scr[i] = jnp.concatenate([m_t, ta, rts[i]], axis=0).astype(BF16)
            u0_scr[i] = u0_bd
            qb_scr[i] = qbs[i].astype(BF16)
            ov_scr[i] = o_vs[i]
            c_scr[i] = (_dot_tn(b_st, u0_bd)
                        + jnp.where(blk_mask, _dot_tn(pre[j]["k2"][c] * e_end, v_of((j, c))), 0.0))

        Hs = list(Hs)
        bigs = []
        for i, (j, c) in enumerate(items):
            big = jnp.dot(lhs_scr[i], Hs[j].astype(BF16), preferred_element_type=F32)
            Hs[j] = big[0:W] + c_scr[i]
            bigs.append(big)
            fill()
        outs = [[] for _ in range(ns)]
        for i, (j, c) in enumerate(items):
            u_bd = jnp.where(blk_mask, bigs[i][W:2 * W], 0.0) + u0_scr[i]
            outs[j].append(bigs[i][2 * W:] + _dot(qb_scr[i], u_bd) + ov_scr[i])

        for j in range(ns):
            sj = slice(j * W, (j + 1) * W)
            o = jnp.concatenate(outs[j], axis=0)
            mean = _split_dot(o, bd_ref[...]) * (1.0 / 64.0)
            oc = o - mean
            var = _split_dot(oc * oc, bd_ref[...]) * (1.0 / 64.0)
            y = (oc * lax.rsqrt(var + RWKV_GN_EPS) * ln_g[:, sj] + ln_b[:, sj]
                 + pre[j]["bonus"][slice(0, R)])
            o_ref[pl.ds(s, R), sj] = y * pre[j]["g"][slice(0, R)]
        while fillers:
            fill()
        return tuple(Hs)

    for thunk in prologue_pieces(0, 0):
        thunk()

    def body(t, Hs):
        Hs = run(0, Hs, 2 * t, prologue_pieces(2 * t + 1, 1))
        return run(1, Hs, 2 * t + 1, prologue_pieces(jnp.minimum(2 * t + 2, nsb - 1), 0))

    lax.fori_loop(0, nsb // 2, body, tuple(jnp.zeros((W, W), F32) for _ in range(ns)))


def _rwkv(p, seq, mu, w0, a0, k_k, k_a, r_k, ln_g, ln_b, wd, wa, wg):
    n = p.shape[0]
    nbatch = n // seq
    ns = 2
    wide = ns * LANES
    ngrp = RWKV_WIDTH // wide
    nb = 4 if seq % (8 * CHUNK) == 0 else 1

    def col(off):
        return pl.BlockSpec((seq, wide), lambda b, q, off=off: (b, off + q))

    def par(off=0):
        return pl.BlockSpec((1, wide), lambda b, q, off=off: (0, off + q))

    in_specs = [
        col(0), col(ngrp), col(2 * ngrp),
        pl.BlockSpec((seq, 2 * LANES), lambda b, q: (b, 3 * RWKV_WIDTH // (2 * LANES))),
        par(0), par(ngrp), par(2 * ngrp),
        pl.BlockSpec((1, 2 * LANES), lambda b, q: (0, 3 * RWKV_WIDTH // (2 * LANES))),
        par(), par(), par(), par(), par(), par(), par(),
        pl.BlockSpec((DECAY_LORA, wide), lambda b, q: (0, q)),
        pl.BlockSpec((ICLR_LORA, wide), lambda b, q: (0, q)),
        pl.BlockSpec((GATE_LORA, wide), lambda b, q: (0, q)),
        pl.BlockSpec((LANES, LANES), lambda b, q: (0, 0)),
    ]
    head_id = jnp.arange(LANES) // RWKV_HEAD_DIM
    bd_ones = (head_id[:, None] == head_id[None, :]).astype(BF16)
    return pl.pallas_call(
        functools.partial(_rwkv_kernel, nb=nb, ns=ns),
        grid=(nbatch, ngrp),
        in_specs=in_specs,
        out_specs=pl.BlockSpec((seq, wide), lambda b, q: (b, q)),
        out_shape=jax.ShapeDtypeStruct((n, RWKV_WIDTH), F32),
        scratch_shapes=[
            pltpu.VMEM((nb * ns, 2 * LANES + CHUNK, LANES), BF16),
            pltpu.VMEM((nb * ns, LANES, LANES), F32),
            pltpu.VMEM((nb * ns, CHUNK, LANES), BF16),
            pltpu.VMEM((nb * ns, CHUNK, LANES), F32),
            pltpu.VMEM((nb * ns, LANES, LANES), F32),
            pltpu.VMEM((2, ns, 10, nb * CHUNK, LANES), F32),
        ],
        compiler_params=pltpu.CompilerParams(
            dimension_semantics=("parallel", "parallel"), vmem_limit_bytes=VMEM_LIMIT),
        name="rwkv",
    )(p, p, p, p, mu, mu, mu, mu, w0, a0, k_k, k_a, r_k, ln_g, ln_b, wd, wa, wg, bd_ones)


def _gdn_kernel(q_ref, k_ref, v_ref, z_ref, ab_ref, cwq_ref, cwk_ref, cwv_ref,
                alog_ref, dtb_ref, ng_ref, o_ref, lhs_scr, u_scr, qk_scr, c_scr, *, nb, ns):
    L = CHUNK
    W = LANES
    P = 2 * L
    R = nb * P
    seq = q_ref.shape[0]
    nsb = seq // R
    hd0 = pl.program_id(1) * ns

    row_r = _iota((R, W), 0)
    lane_r = _iota((R, W), 1)
    tin = row_r & (L - 1)
    row8 = _iota((8, W), 0)
    rm = _iota((W, W), 0)
    lm = _iota((W, W), 1)
    dif = rm - lm
    tin_m = rm & (L - 1)
    eye = (rm == lm).astype(F32)
    masks = _level_masks(rm, lm)

    lane1 = _iota((1, W), 1)
    pick = lambda ref, j: jnp.sum(jnp.where(lane1 == hd0 + j, ref[...], 0.0), axis=1, keepdims=True)
    neg_a = [-jnp.exp(pick(alog_ref, j)) for j in range(ns)]
    dt_b = [pick(dtb_ref, j) for j in range(ns)]
    cwq, cwk, cwv = cwq_ref[...], cwk_ref[...], cwv_ref[...]
    ng = ng_ref[...]
    scale = GDN_HEAD_DIM ** -0.5
    neg_inf = jnp.float32(-jnp.inf)

    def conv(x, prev8, cw):
        acc = x * cw[CONV_WIDTH - 1:CONV_WIDTH, :]
        for j in range(1, CONV_WIDTH):
            xs = pltpu.roll(x, j, axis=0)
            ps = pltpu.roll(prev8, j, axis=0)
            head = jnp.where(row8 < j, ps, xs[0:8])
            xs = jnp.concatenate([head, xs[8:]], axis=0)
            acc = acc + xs * cw[CONV_WIDTH - 1 - j:CONV_WIDTH - j, :]
        return _silu(acc)

    def l2n(x):
        return x * lax.rsqrt(jnp.sum(x * x, axis=-1, keepdims=True) + L2_EPS)

    def body(sb, carry):
        Ss, pqs, pks, pvs = carry
        s = pl.multiple_of(sb * R, R)
        ab = ab_ref[pl.ds(s, R), :]

        pre, raws = [], []
        for j in range(ns):
            sj = slice(j * W, (j + 1) * W)
            hd = hd0 + j
            q_raw = q_ref[pl.ds(s, R), sj]
            k_raw = k_ref[pl.ds(s, R), sj]
            v_raw = v_ref[pl.ds(s, R), sj]
            raws.append((q_raw[R - 8:, :], k_raw[R - 8:, :], v_raw[R - 8:, :]))
            q = l2n(conv(q_raw, pqs[j], cwq[:, sj])) * scale
            k = l2n(conv(k_raw, pks[j], cwk[:, sj]))
            v = conv(v_raw, pvs[j], cwv[:, sj])
            a_in = jnp.sum(jnp.where(lane_r == hd, ab, 0.0), axis=1, keepdims=True)
            b_in = jnp.sum(jnp.where(lane_r == hd + GDN_HEADS, ab, 0.0), axis=1, keepdims=True)
            beta = _sigmoid(b_in)
            g_log = neg_a[j] * _softplus(a_in + dt_b[j])
            gc = _chunk_cumsum(jnp.broadcast_to(g_log, (R, W)), tin)
            e_gc = jnp.exp(gc)
            kb = k * beta
            pre.append(dict(q=q, k=k, gc=gc, kb=kb, vb=v * beta, kbe=kb * e_gc, qe=q * e_gc))

        items = [(j, slice(i * P, (i + 1) * P)) for i in range(nb) for j in range(ns)]
        decs = []
        for j, sl in items:
            gc_p = pre[j]["gc"][sl]
            dlog = gc_p - jnp.transpose(gc_p)
            dlog = jnp.where(dif >= 0, dlog, neg_inf)
            decs.append(jnp.exp(jnp.where(dif <= tin_m, dlog, neg_inf)))
        negAs = [jnp.where(dif > 0, _dot_nt(pre[j]["kb"][sl], pre[j]["k"][sl]) * dec, 0.0) * -1.0
                 for (j, sl), dec in zip(items, decs)]
        qks = [_dot_nt(pre[j]["q"][sl], pre[j]["k"][sl]) * dec for (j, sl), dec in zip(items, decs)]
        tinvs = _tri_inverse(negAs, eye, masks)
        uws = [_dot(t, jnp.concatenate([pre[j]["vb"][sl], pre[j]["kbe"][sl]], axis=1))
               for t, (j, sl) in zip(tinvs, items)]
        steps = []
        for i in range(nb):
            for c in range(2):
                for j in range(ns):
                    it = i * ns + j
                    sl, uw, qk = items[it][1], uws[it], qks[it]
                    cs = slice(sl.start + c * L, sl.start + (c + 1) * L)
                    hs = slice(c * L, (c + 1) * L)
                    gc_c = pre[j]["gc"][cs]
                    gc_last = gc_c[L - 1:L, :]
                    k_dec = pre[j]["k"][cs] * jnp.exp(gc_last - gc_c)
                    u_c, w_c = uw[hs, 0:W], uw[hs, W:]
                    m_c = eye * jnp.exp(gc_last) - _dot_tn(k_dec, w_c)
                    n_st = len(steps)
                    lhs_scr[n_st] = jnp.concatenate(
                        [m_c, w_c, pre[j]["qe"][cs]], axis=0).astype(BF16)
                    u_scr[n_st] = u_c
                    qk_scr[n_st] = qk[hs].astype(BF16)
                    c_scr[n_st] = _dot_tn(k_dec, u_c)
                    steps.append(j)

        Ss = list(Ss)
        bigs = []
        for i, j in enumerate(steps):
            big = jnp.dot(lhs_scr[i], Ss[j].astype(BF16), preferred_element_type=F32)
            Ss[j] = big[0:W] + c_scr[i]
            bigs.append(big)
        outs = [[] for _ in range(ns)]
        for i, j in enumerate(steps):
            v_new = u_scr[i] - bigs[i][W:W + L]
            outs[j].append(bigs[i][W + L:]
                           + _dot(qk_scr[i], jnp.concatenate([v_new, v_new], axis=0)))

        for j in range(ns):
            sj = slice(j * W, (j + 1) * W)
            o = jnp.concatenate(outs[j], axis=0)
            o = o * lax.rsqrt(jnp.mean(o * o, axis=-1, keepdims=True) + NORM_EPS) * ng
            o_ref[pl.ds(s, R), sj] = o * _silu(z_ref[pl.ds(s, R), sj])
        return (tuple(Ss), tuple(t[0] for t in raws), tuple(t[1] for t in raws),
                tuple(t[2] for t in raws))

    z8 = jnp.zeros((8, W), F32)
    init = (tuple(jnp.zeros((W, W), F32) for _ in range(ns)), (z8,) * ns, (z8,) * ns, (z8,) * ns)
    lax.fori_loop(0, nsb, body, init)


def _gdn(p, seq, conv_w, a_log, dt_bias, norm_g):
    n = p.shape[0]
    nbatch = n // seq
    ns = 2
    wide = ns * LANES
    ngrp = GDN_HEADS // ns
    qoff = COL_QKV // wide
    nb = 2 if seq % (4 * CHUNK) == 0 else 1

    def col(off):
        return pl.BlockSpec((seq, wide), lambda b, h, off=off: (b, off + h))

    def cw(off):
        return pl.BlockSpec((CONV_WIDTH, wide), lambda b, h, off=off: (0, off + h))

    one = pl.BlockSpec((1, LANES), lambda b, h: (0, 0))
    in_specs = [
        col(qoff), col(qoff + ngrp), col(qoff + 2 * ngrp), col(COL_Z // wide),
        pl.BlockSpec((seq, LANES), lambda b, h: (b, COL_AB // LANES)),
        cw(0), cw(ngrp), cw(2 * ngrp), one, one, one,
    ]
    n_steps = 2 * nb * ns
    return pl.pallas_call(
        functools.partial(_gdn_kernel, nb=nb, ns=ns),
        grid=(nbatch, ngrp),
        in_specs=in_specs,
        out_specs=pl.BlockSpec((seq, wide), lambda b, h: (b, h)),
        out_shape=jax.ShapeDtypeStruct((n, GDN_WIDTH), F32),
        scratch_shapes=[
            pltpu.VMEM((n_steps, 2 * CHUNK + LANES, LANES), BF16),
            pltpu.VMEM((n_steps, CHUNK, LANES), F32),
            pltpu.VMEM((n_steps, CHUNK, LANES), BF16),
            pltpu.VMEM((n_steps, LANES, LANES), F32),
        ],
        compiler_params=pltpu.CompilerParams(
            dimension_semantics=("parallel", "parallel"), vmem_limit_bytes=VMEM_LIMIT),
        name="gdn",
    )(p, p, p, p, p, conv_w, conv_w, conv_w, a_log, dt_bias, norm_g)


def _mixout_kernel(x_ref, ya_ref, yb_ref, ga_ref, gb_ref, wpa_ref, wpb_ref, wout_ref, g2_ref,
                   *rest, with_router):
    if with_router:
        rt_ref, xo_ref, h_ref, comb_ref = rest
    else:
        xo_ref, h_ref = rest
    ya = _dot(ya_ref[...], wpa_ref[...])
    yb = _dot(yb_ref[...], wpb_ref[...])
    y = _sigmoid(ga_ref[...]) * ya + _sigmoid(gb_ref[...]) * yb
    xn = x_ref[...] + _dot(y, wout_ref[...])
    xo_ref[...] = xn
    h = xn * lax.rsqrt(jnp.mean(xn * xn, axis=-1, keepdims=True) + NORM_EPS) * g2_ref[...]
    h_ref[...] = h.astype(h_ref.dtype)
    if with_router:
        rt = rt_ref[...]
        h_hi = h.astype(BF16)
        h_lo = (h - h_hi.astype(F32)).astype(BF16)
        r_hi = rt.astype(BF16)
        r_lo = (rt - r_hi.astype(F32)).astype(BF16)
        logits = (jnp.dot(h_hi, r_hi, preferred_element_type=F32)
                  + jnp.dot(h_lo, r_hi, preferred_element_type=F32)
                  + jnp.dot(h_hi, r_lo, preferred_element_type=F32))
        lane = _iota(logits.shape, 1)
        neg = jnp.float32(-jnp.inf)
        logits = jnp.where(lane < N_EXPERTS, logits, neg)
        m1 = jnp.max(logits, axis=1, keepdims=True)
        i1 = jnp.min(jnp.where(logits == m1, lane, LANES), axis=1, keepdims=True)
        l2 = jnp.where(lane == i1, neg, logits)
        m2 = jnp.max(l2, axis=1, keepdims=True)
        i2 = jnp.min(jnp.where(l2 == m2, lane, LANES), axis=1, keepdims=True)
        e2 = jnp.exp(m2 - m1)
        g1 = 1.0 / (1.0 + e2)
        g2 = e2 / (1.0 + e2)
        comb_ref[...] = (jnp.where(lane == 0, i1.astype(F32), 0.0)
                         + jnp.where(lane == 1, i2.astype(F32), 0.0)
                         + jnp.where(lane == 2, g1, 0.0) + jnp.where(lane == 3, g2, 0.0))


def _mixout(x2, ya, yb, p, wpa, wpb, wout, g2, router, tm):
    n = x2.shape[0]
    with_router = router is not None
    row = lambda w: pl.BlockSpec((tm, w), lambda i: (i, 0))
    full = lambda a: pl.BlockSpec(a.shape, lambda i: (0, 0))
    in_specs = [
        row(D_MODEL), row(RWKV_WIDTH), row(GDN_WIDTH),
        pl.BlockSpec((tm, D_MODEL), lambda i: (i, COL_GATE // D_MODEL)),
        pl.BlockSpec((tm, D_MODEL), lambda i: (i, COL_GATE // D_MODEL + 1)),
        full(wpa), full(wpb), full(wout), full(g2),
    ]
    args = [x2, ya, yb, p, p, wpa, wpb, wout, g2]
    out_specs = [row(D_MODEL), row(D_MODEL)]
    out_shape = [jax.ShapeDtypeStruct((n, D_MODEL), F32),
                 jax.ShapeDtypeStruct((n, D_MODEL), F32 if with_router else BF16)]
    if with_router:
        in_specs.append(full(router))
        args.append(router)
        out_specs.append(row(LANES))
        out_shape.append(jax.ShapeDtypeStruct((n, LANES), F32))
    return pl.pallas_call(
        functools.partial(_mixout_kernel, with_router=with_router),
        grid=(n // tm,),
        in_specs=in_specs,
        out_specs=out_specs,
        out_shape=out_shape,
        compiler_params=pltpu.CompilerParams(
            dimension_semantics=("parallel",), vmem_limit_bytes=VMEM_LIMIT),
        name="mixout_router" if with_router else "mixout",
    )(*args)


def _rms_out(y, fg_ref):
    return y * lax.rsqrt(jnp.mean(y * y, axis=-1, keepdims=True) + NORM_EPS) * fg_ref[...]


def _ffn_kernel(h_ref, x_ref, wg_ref, wu_ref, wd_ref, *rest, final):
    fg_ref = rest[0] if final else None
    o_ref, acc_ref = rest[-2:]
    j = pl.program_id(1)

    @pl.when(j == 0)
    def _():
        acc_ref[...] = jnp.zeros_like(acc_ref)

    h = h_ref[...]
    act = _silu(jnp.dot(h, wg_ref[...], preferred_element_type=F32)) * jnp.dot(
        h, wu_ref[...], preferred_element_type=F32)
    acc_ref[...] += jnp.dot(act.astype(BF16), wd_ref[...], preferred_element_type=F32)

    @pl.when(j == pl.num_programs(1) - 1)
    def _():
        y = x_ref[...] + acc_ref[...]
        o_ref[...] = _rms_out(y, fg_ref) if final else y


def _ffn(h, x2, wg, wu, wd, final_g, tm, tf):
    n = x2.shape[0]
    final = final_g is not None
    row = lambda w: pl.BlockSpec((tm, w), lambda i, j: (i, 0))
    in_specs = [
        row(D_MODEL), row(D_MODEL),
        pl.BlockSpec((D_MODEL, tf), lambda i, j: (0, j)),
        pl.BlockSpec((D_MODEL, tf), lambda i, j: (0, j)),
        pl.BlockSpec((tf, D_MODEL), lambda i, j: (j, 0)),
    ]
    args = [h, x2, wg, wu, wd]
    if final:
        in_specs.append(pl.BlockSpec((1, D_MODEL), lambda i, j: (0, 0)))
        args.append(final_g)
    return pl.pallas_call(
        functools.partial(_ffn_kernel, final=final),
        grid=(n // tm, D_FF // tf),
        in_specs=in_specs,
        out_specs=row(D_MODEL),
        out_shape=jax.ShapeDtypeStruct((n, D_MODEL), F32),
        scratch_shapes=[pltpu.VMEM((tm, D_MODEL), F32)],
        compiler_params=pltpu.CompilerParams(
            dimension_semantics=("parallel", "arbitrary"), vmem_limit_bytes=VMEM_LIMIT),
        name="ffn",
    )(*args)


MOE_ROWS = 512


def _plan_kernel(sel_ref, pos_ref, tile_ref, cnt_scr, *, tm, n_tiles):
    ph = pl.program_id(0)
    i = pl.program_id(1)
    sel = sel_ref[...]
    lane = _iota((tm, LANES), 1)
    e1 = jnp.sum(jnp.where(lane == 0, sel, 0.0), axis=1, keepdims=True).astype(jnp.int32)
    e2 = jnp.sum(jnp.where(lane == 1, sel, 0.0), axis=1, keepdims=True).astype(jnp.int32)
    oh1 = lane == e1
    oh2 = lane == e2
    both = jnp.where(oh1, 1.0, 0.0) + jnp.where(oh2, 1.0, 0.0)

    @pl.when((ph == 0) & (i == 0))
    def _():
        cnt_scr[...] = jnp.zeros_like(cnt_scr)

    @pl.when(ph == 0)
    def _():
        cnt_scr[0:1, :] += jnp.sum(both, axis=0, keepdims=True)

    @pl.when((ph == 1) & (i == 0))
    def _():
        cnt = cnt_scr[0:1, :]
        padded = jnp.floor((cnt + (MOE_ROWS - 1)) * (1.0 / MOE_ROWS)) * MOE_ROWS
        rl = _iota((LANES, LANES), 0)
        cl = _iota((LANES, LANES), 1)
        upper = jnp.where(rl < cl, 1.0, 0.0)
        starts = _dot_hi(jnp.broadcast_to(padded, (8, LANES)), upper)[0:1, :]
        ends = starts + padded
        cnt_scr[0:1, :] = starts
        tl = _iota((n_tiles, LANES), 1)
        t0 = (_iota((n_tiles, LANES), 0) * MOE_ROWS).astype(F32)
        lane_ok = tl < N_EXPERTS
        owner = jnp.sum(jnp.where(lane_ok, jnp.where(t0 >= ends, 1.0, 0.0), 0.0),
                        axis=1, keepdims=True)
        used = jnp.sum(jnp.where(lane_ok, padded, 0.0), axis=1, keepdims=True)
        valid = t0[:, 0:1] < used
        last_e = jnp.max(jnp.where(lane_ok, jnp.where(padded > 0.0, tl.astype(F32), 0.0), 0.0),
                         axis=1, keepdims=True)
        ex = jnp.where(valid, jnp.minimum(owner, N_EXPERTS - 1.0), last_e)
        tile_ref[...] = (jnp.where(tl == 0, ex, 0.0)
                         + jnp.where(tl == 1, jnp.where(valid, 1.0, 0.0), 0.0)).astype(jnp.int32)

    @pl.when(ph == 1)
    def _():
        rr = _iota((tm, tm), 0)
        cc = _iota((tm, tm), 1)
        before = jnp.where(cc < rr, 1.0, 0.0).astype(BF16)
        base = cnt_scr[0:1, :] + jnp.dot(before, both.astype(BF16), preferred_element_type=F32)
        p1 = jnp.sum(jnp.where(oh1, base, 0.0), axis=1, keepdims=True)
        p2 = jnp.sum(jnp.where(oh2, base, 0.0), axis=1, keepdims=True)
        pos_ref[...] = (jnp.where(lane == 0, p1, 0.0) + jnp.where(lane == 1, p2, 0.0)
                        ).astype(jnp.int32)
        cnt_scr[0:1, :] += jnp.sum(both, axis=0, keepdims=True)


def _plan(sel, tm, n_tiles):
    n = sel.shape[0]
    return pl.pallas_call(
        functools.partial(_plan_kernel, tm=tm, n_tiles=n_tiles),
        grid=(2, n // tm),
        in_specs=[pl.BlockSpec((tm, LANES), lambda ph, i: (i, 0))],
        out_specs=[pl.BlockSpec((tm, LANES), lambda ph, i: (i * ph, 0)),
                   pl.BlockSpec((n_tiles, LANES), lambda ph, i: (0, 0))],
        out_shape=[jax.ShapeDtypeStruct((n, LANES), jnp.int32),
                   jax.ShapeDtypeStruct((n_tiles, LANES), jnp.int32)],
        scratch_shapes=[pltpu.VMEM((8, LANES), F32)],
        compiler_params=pltpu.CompilerParams(
            dimension_semantics=("arbitrary", "arbitrary"), vmem_limit_bytes=VMEM_LIMIT),
        name="moe_plan",
    )(sel)


def _row_copy(src_ref, src_row, dst_ref, dst_row, sem):
    return pltpu.make_async_copy(src_ref.at[pl.ds(src_row, 1)], dst_ref.at[pl.ds(dst_row, 1)], sem)


def _dispatch_kernel(pos_ref, h_ref, xs_in_ref, xs_ref, sem, *, tm):
    del xs_in_ref

    def start(r, c):
        _row_copy(h_ref, r, xs_ref, pos_ref[2 * r], sem).start()
        _row_copy(h_ref, r, xs_ref, pos_ref[2 * r + 1], sem).start()
        return c

    def wait(r, c):
        _row_copy(h_ref, 0, xs_ref, 0, sem).wait()
        _row_copy(h_ref, 0, xs_ref, 0, sem).wait()
        return c

    lax.fori_loop(0, tm, start, 0, unroll=8)
    lax.fori_loop(0, tm, wait, 0, unroll=8)


def _dispatch(pos_flat, h, n_rows, tm):
    n = h.shape[0]
    xs0 = jnp.zeros((n_rows, D_MODEL), F32)
    return pl.pallas_call(
        functools.partial(_dispatch_kernel, tm=tm),
        grid=(n // tm,),
        in_specs=[
            pl.BlockSpec((2 * tm,), lambda i: (i,), memory_space=pltpu.SMEM),
            pl.BlockSpec((tm, D_MODEL), lambda i: (i, 0)),
            pl.BlockSpec(memory_space=pl.ANY),
        ],
        out_specs=pl.BlockSpec(memory_space=pl.ANY),
        out_shape=jax.ShapeDtypeStruct((n_rows, D_MODEL), F32),
        scratch_shapes=[pltpu.SemaphoreType.DMA(())],
        input_output_aliases={2: 0},
        compiler_params=pltpu.CompilerParams(
            dimension_semantics=("arbitrary",), vmem_limit_bytes=VMEM_LIMIT),
        name="moe_dispatch",
    )(pos_flat, h, xs0)


def _experts_kernel(te_ref, tv_ref, xs_ref, wg_ref, wu_ref, wd_ref, ys_ref, xb_scr, acc_scr):
    t = pl.program_id(0)
    j = pl.program_id(1)
    del te_ref

    @pl.when(j == 0)
    def _():
        xb_scr[...] = xs_ref[...].astype(BF16)
        acc_scr[...] = jnp.zeros_like(acc_scr)

    @pl.when(tv_ref[t] == 1)
    def _():
        xb = xb_scr[...]
        act = _silu(jnp.dot(xb, wg_ref[0], preferred_element_type=F32)) * jnp.dot(
            xb, wu_ref[0], preferred_element_type=F32)
        acc_scr[...] += jnp.dot(act.astype(BF16), wd_ref[0], preferred_element_type=F32)

    @pl.when(j == pl.num_programs(1) - 1)
    def _():
        ys_ref[...] = acc_scr[...]


def _experts(tile_e, tile_v, xs, wg, wu, wd, tf):
    n_rows = xs.shape[0]
    grid_spec = pltpu.PrefetchScalarGridSpec(
        num_scalar_prefetch=2,
        grid=(n_rows // MOE_ROWS, D_FF // tf),
        in_specs=[
            pl.BlockSpec((MOE_ROWS, D_MODEL), lambda t, j, te, tv: (t, 0)),
            pl.BlockSpec((1, D_MODEL, tf), lambda t, j, te, tv: (te[t], 0, j)),
            pl.BlockSpec((1, D_MODEL, tf), lambda t, j, te, tv: (te[t], 0, j)),
            pl.BlockSpec((1, tf, D_MODEL), lambda t, j, te, tv: (te[t], j, 0)),
        ],
        out_specs=pl.BlockSpec((MOE_ROWS, D_MODEL), lambda t, j, te, tv: (t, 0)),
        scratch_shapes=[pltpu.VMEM((MOE_ROWS, D_MODEL), BF16), pltpu.VMEM((MOE_ROWS, D_MODEL), F32)],
    )
    return pl.pallas_call(
        _experts_kernel,
        grid_spec=grid_spec,
        out_shape=jax.ShapeDtypeStruct((n_rows, D_MODEL), F32),
        compiler_params=pltpu.CompilerParams(
            dimension_semantics=("parallel", "arbitrary"), vmem_limit_bytes=VMEM_LIMIT),
        name="moe_experts",
    )(tile_e, tile_v, xs, wg, wu, wd)


def _combine_kernel(pos_ref, x_ref, sel_ref, *rest, tm, final):
    fg_ref = rest[0] if final else None
    ys_ref, o_ref, buf, sem = rest[-4:]

    def start(r, c):
        _row_copy(ys_ref, pos_ref[2 * r], buf.at[0], r, sem).start()
        _row_copy(ys_ref, pos_ref[2 * r + 1], buf.at[1], r, sem).start()
        return c

    def wait(r, c):
        _row_copy(ys_ref, 0, buf.at[0], 0, sem).wait()
        _row_copy(ys_ref, 0, buf.at[1], 0, sem).wait()
        return c

    lax.fori_loop(0, tm, start, 0, unroll=8)
    lax.fori_loop(0, tm, wait, 0, unroll=8)
    sel = sel_ref[...]
    lane = _iota(sel.shape, 1)
    g1 = jnp.sum(jnp.where(lane == 2, sel, 0.0), axis=1, keepdims=True)
    g2 = jnp.sum(jnp.where(lane == 3, sel, 0.0), axis=1, keepdims=True)
    y = x_ref[...] + g1 * buf[0] + g2 * buf[1]
    o_ref[...] = _rms_out(y, fg_ref) if final else y


def _combine(pos_flat, x2, sel, ys, final_g, tm):
    n = x2.shape[0]
    final = final_g is not None
    in_specs = [
        pl.BlockSpec((2 * tm,), lambda i: (i,), memory_space=pltpu.SMEM),
        pl.BlockSpec((tm, D_MODEL), lambda i: (i, 0)),
        pl.BlockSpec((tm, LANES), lambda i: (i, 0)),
    ]
    args = [pos_flat, x2, sel]
    if final:
        in_specs.append(pl.BlockSpec((1, D_MODEL), lambda i: (0, 0)))
        args.append(final_g)
    in_specs.append(pl.BlockSpec(memory_space=pl.ANY))
    args.append(ys)
    return pl.pallas_call(
        functools.partial(_combine_kernel, tm=tm, final=final),
        grid=(n // tm,),
        in_specs=in_specs,
        out_specs=pl.BlockSpec((tm, D_MODEL), lambda i: (i, 0)),
        out_shape=jax.ShapeDtypeStruct((n, D_MODEL), F32),
        scratch_shapes=[pltpu.VMEM((2, tm, D_MODEL), F32), pltpu.SemaphoreType.DMA(())],
        compiler_params=pltpu.CompilerParams(
            dimension_semantics=("arbitrary",), vmem_limit_bytes=VMEM_LIMIT),
        name="moe_combine",
    )(*args)


def _moe(h, x2, sel, wg, wu, wd, final_g, tm, tf):
    n = x2.shape[0]
    n_tiles = pl.cdiv(2 * n, MOE_ROWS) + N_EXPERTS
    pos, tiles = _plan(sel, tm, n_tiles)
    pos_flat = pos[:, 0:2].reshape(2 * n)
    xs = _dispatch(pos_flat, h, n_tiles * MOE_ROWS, tm)
    ys = _experts(tiles[:, 0], tiles[:, 1], xs, wg, wu, wd, tf)
    return _combine(pos_flat, x2, sel, ys, final_g, tm)


def _tile(n, pref):
    t = min(pref, n)
    while n % t:
        t //= 2
    return t


def _pad_lanes(a, width=LANES):
    return jnp.pad(a, ((0, 0), (0, width - a.shape[-1])))


def kernel(x, norm1_g, w_in, tshift_mu, w0, w_decay_up, a0, w_iclr_up, w_gate_up, k_k, k_a, r_k,
           lnx_g, lnx_b, w_pa, conv_w, a_log, dt_bias, gdn_norm_g, w_pb, w_out, norm2_g,
           ffn_w_gate, ffn_w_up, ffn_w_down, moe_router, moe_w_gate, moe_w_up, moe_w_down, final_g):
    nb, seq, d = x.shape
    n = nb * seq
    x2 = x.reshape(n, d)
    tm_in = _tile(n, 1024)
    tm = _tile(n, 512)
    rw = 1792
    gq = 3 * GDN_WIDTH
    for layer in range(DEPTH):
        wi = w_in[layer]
        ab = wi[:, rw + gq:rw + gq + 2 * GDN_HEADS]
        zc = wi[:, rw + gq + 2 * GDN_HEADS:rw + gq + 2 * GDN_HEADS + GDN_WIDTH]
        gate = wi[:, rw + gq + 2 * GDN_HEADS + GDN_WIDTH:]
        w_cat = jnp.concatenate(
            [wi[:, :rw + gq], zc, ab, jnp.zeros((d, COL_GATE - COL_AB - 2 * GDN_HEADS), F32), gate],
            axis=1).astype(BF16)
        p = _inproj(x2, norm1_g[layer][None, :], w_cat, tm_in, 1536)
        one = lambda a: a[layer][None, :]
        ya = _rwkv(p, seq, one(tshift_mu), one(w0), one(a0), one(k_k), one(k_a), one(r_k),
                   one(lnx_g), one(lnx_b), w_decay_up[layer].astype(BF16),
                   w_iclr_up[layer].astype(BF16), w_gate_up[layer].astype(BF16))
        yb = _gdn(p, seq, conv_w[layer], _pad_lanes(one(a_log)), _pad_lanes(one(dt_bias)),
                  one(gdn_norm_g))
        j = layer // 2
        routed = layer % 2 == 1
        router = _pad_lanes(moe_router[j]) if routed else None
        res = _mixout(x2, ya, yb, p, w_pa[layer].astype(BF16), w_pb[layer].astype(BF16),
                      w_out[layer].astype(BF16), one(norm2_g), router, tm)
        fg = final_g[None, :] if layer == DEPTH - 1 else None
        if routed:
            x2, h, sel = res
            x2 = _moe(h, x2, sel, moe_w_gate[j].astype(BF16), moe_w_up[j].astype(BF16),
                      moe_w_down[j].astype(BF16), fg, tm, 1408)
        else:
            x2, h = res
            x2 = _ffn(h, x2, ffn_w_gate[j].astype(BF16), ffn_w_up[j].astype(BF16),
                      ffn_w_down[j].astype(BF16), fg, tm, 1408)
    return x2.reshape(nb, seq, d)
```

```python
import functools

import jax
import jax.numpy as jnp
from jax import lax
from jax.experimental import pallas as pl
from jax.experimental.pallas import tpu as pltpu

F32 = jnp.float32
BF16 = jnp.bfloat16
HI = lax.Precision.HIGHEST

D_MODEL = 1024
DEPTH = 2
RWKV_HEADS = 8
RWKV_HEAD_DIM = 64
RWKV_WIDTH = 512
DECAY_LORA = 64
ICLR_LORA = 64
GATE_LORA = 128
RWKV_GN_EPS = 64e-5
GDN_HEADS = 4
GDN_HEAD_DIM = 128
GDN_WIDTH = 512
CONV_WIDTH = 4
CHUNK = 64
D_FF = 2816
N_EXPERTS = 8
NORM_EPS = 1e-6
L2_EPS = 1e-6

LANES = 128
P_COLS = 6144
COL_QKV = 1792
COL_Z = 3328
COL_AB = 3840
COL_GATE = 4096
VMEM_LIMIT = 56 * 1024 * 1024
FF_CHUNK = D_FF // 2


def _dot(a, b):
    return jnp.dot(a.astype(BF16), b.astype(BF16), preferred_element_type=F32)


def _dot_nt(a, b):
    return lax.dot_general(a.astype(BF16), b.astype(BF16), (((1,), (1,)), ((), ())),
                           preferred_element_type=F32)


def _dot_tn(a, b):
    return lax.dot_general(a.astype(BF16), b.astype(BF16), (((0,), (0,)), ((), ())),
                           preferred_element_type=F32)


def _dot_hi(a, b):
    return jnp.dot(a, b, preferred_element_type=F32, precision=HI)


def _softplus(x):
    return jnp.maximum(x, 0.0) + jnp.log(1.0 + jnp.exp(-jnp.abs(x)))


def _sigmoid(x):
    return 1.0 / (1.0 + jnp.exp(-x))


def _silu(x):
    return x * _sigmoid(x)


def _iota(shape, dim):
    return lax.broadcasted_iota(jnp.int32, shape, dim)


def _inproj_kernel(x_ref, g_ref, w_ref, p_ref, h_scr):
    @pl.when(pl.program_id(1) == 0)
    def _():
        x = x_ref[...]
        h = x * lax.rsqrt(jnp.mean(x * x, axis=-1, keepdims=True) + NORM_EPS) * g_ref[...]
        h_scr[...] = h.astype(BF16)

    p_ref[...] = jnp.dot(h_scr[...], w_ref[...], preferred_element_type=F32).astype(p_ref.dtype)


def _inproj(x2, g, w_cat, tm, tn):
    n = x2.shape[0]
    return pl.pallas_call(
        _inproj_kernel,
        grid=(n // tm, P_COLS // tn),
        in_specs=[
            pl.BlockSpec((tm, D_MODEL), lambda i, j: (i, 0)),
            pl.BlockSpec((1, D_MODEL), lambda i, j: (0, 0)),
            pl.BlockSpec((D_MODEL, tn), lambda i, j: (0, j)),
        ],
        out_specs=pl.BlockSpec((tm, tn), lambda i, j: (i, j)),
        out_shape=jax.ShapeDtypeStruct((n, P_COLS), BF16),
        scratch_shapes=[pltpu.VMEM((tm, D_MODEL), BF16)],
        compiler_params=pltpu.CompilerParams(
            dimension_semantics=("parallel", "arbitrary"), vmem_limit_bytes=VMEM_LIMIT),
        name="inproj",
    )(x2, g, w_cat)


def _split_dot(x, m):
    hi = x.astype(BF16)
    lo = (x - hi.astype(F32)).astype(BF16)
    return (jnp.dot(hi, m, preferred_element_type=F32) + jnp.dot(lo, m, preferred_element_type=F32))


def _chunk_cumsum(x, tin):
    y = x
    sh = 1
    while sh < CHUNK:
        y = y + jnp.where(tin >= sh, pltpu.roll(y, sh, axis=0), 0.0)
        sh *= 2
    return y


def _level_masks(rm, lm):
    out = []
    for k in range(6):
        x = rm >> k
        y = lm >> k
        out.append((((x ^ y) + 2 * (1 - (x & 1))) == 1).astype(F32))
    return out


def _tri_inverse(mats, eye, masks, between=lambda: None):
    ts = [eye + a * masks[0] for a in mats]
    for k in range(1, 6):
        xs = [_dot(a * masks[k], t) for a, t in zip(mats, ts)]
        between()
        ts = [t + _dot(t, x) for t, x in zip(ts, xs)]
        between()
    return ts


def _rwkv_kernel(r_ref, k_ref, v_ref, lo_ref, mur_ref, muk_ref, muv_ref, mulo_ref,
                 w0_ref, a0_ref, kk_ref, ka_ref, rk_ref, lg_ref, lb_ref,
                 wd_ref, wa_ref, wg_ref, bd_ref, o_ref,
                 lhs_scr, u0_scr, qb_scr, ov_scr, c_scr, pre_scr, *, nb, ns):
    L = CHUNK
    W = LANES
    R = nb * L
    seq = r_ref.shape[0]
    nsb = seq // R

    row_r = _iota((R, W), 0)
    row_r2 = _iota((R, 2 * W), 0)
    row_c = _iota((L, W), 0)
    lane = _iota((L, W), 1)
    h1 = lane < 64
    rm = _iota((W, W), 0)
    lm = _iota((W, W), 1)
    tri_mask = (lm & 63) < (rm & 63) + (rm >> 6)
    blk_mask = (rm >> 6) == (lm >> 6)
    eye = (rm == lm).astype(F32)
    masks = _level_masks(rm, lm)

    mur, muk, muv, mulo = mur_ref[...], muk_ref[...], muv_ref[...], mulo_ref[...]
    w0, a0, k_k, k_a, r_k = w0_ref[...], a0_ref[...], kk_ref[...], ka_ref[...], rk_ref[...]
    ln_g, ln_b = lg_ref[...], lb_ref[...]
    wd, wa, wg = wd_ref[...], wa_ref[...], wg_ref[...]

    def shift(x, prev_row, mu, rowi):
        xp = pltpu.roll(x, 1, axis=0)
        xp = jnp.where(rowi == 0, prev_row, xp)
        return x + (xp - x) * mu

    def last_row_before(ref, s, cols):
        rows = ref[pl.ds(pl.multiple_of(jnp.maximum(s - 16, 0), 16), 16), cols].astype(F32)
        return rows[15:16, :] * (s > 0).astype(F32)

    keys = ("v", "k2", "bh", "cl", "g", "bonus", "rt", "at", "bt", "kt")

    class Stored:
        def __init__(self, par, j):
            self.par, self.j = par, j

        def __getitem__(self, key):
            par, j, q = self.par, self.j, keys.index(key)

            class Rows:
                def __getitem__(self, rows):
                    return pre_scr[par, j, q, rows]
            return Rows()

    def prologue_pieces(sb, par):
        sh = {}

        def shared():
            s = pl.multiple_of(sb * R, R)
            lo_raw = lo_ref[pl.ds(s, R), :].astype(F32)
            lo = shift(lo_raw, last_row_before(lo_ref, s, slice(None)), mulo, row_r2)
            wl = lo[:, 0:DECAY_LORA]
            al = lo[:, DECAY_LORA:DECAY_LORA + ICLR_LORA]
            gl = lo[:, DECAY_LORA + ICLR_LORA:]
            sh["z"] = w0 + _dot(jnp.tanh(wl), wd)
            sh["a"] = _sigmoid(a0 + _dot(al, wa))
            sh["g"] = _dot(_sigmoid(gl), wg)
            for j in range(ns):
                sj = slice(j * W, (j + 1) * W)
                for name, ref, mu in (("r", r_ref, mur), ("k", k_ref, muk), ("v", v_ref, muv)):
                    sh[name, j] = shift(ref[pl.ds(s, R), sj].astype(F32), last_row_before(ref, s, sj),
                                        mu[:, sj], row_r)

        def piece(j, c):
            sj = slice(j * W, (j + 1) * W)
            r, k, v = sh["r", j][c], sh["k", j][c], sh["v", j][c]
            w_log = -_softplus(-sh["z"][c, sj]) - 0.5
            lw = -jnp.exp(w_log)
            a = sh["a"][c, sj]
            kk = k * k_k[:, sj]
            kk = kk * lax.rsqrt(_split_dot(kk * kk, bd_ref[...]) + L2_EPS)
            k2 = k * (1.0 + (a - 1.0) * k_a[:, sj])
            bh = kk * a
            cl = _chunk_cumsum(lw, row_c)
            e_neg = jnp.exp(-cl)
            vals = dict(
                v=v, k2=k2, bh=bh, cl=cl, g=sh["g"][c, sj],
                bonus=_split_dot(r * k2 * r_k[:, sj], bd_ref[...]) * v,
                rt=r * jnp.exp(cl), at=-kk * jnp.exp(cl - lw), bt=bh * e_neg, kt=k2 * e_neg)
            for q, key in enumerate(keys):
                pre_scr[par, j, q, c] = vals[key]

        return [shared] + [functools.partial(piece, j, slice(i * L, (i + 1) * L))
                           for i in range(nb) for j in range(ns)]

    def run(par, Hs, sb, fillers):
        pre = [Stored(par, j) for j in range(ns)]

        def fill():
            if fillers:
                fillers.pop(0)()

        s = pl.multiple_of(sb * R, R)
        items = [(j, slice(i * L, (i + 1) * L)) for i in range(nb) for j in range(ns)]
        v_of = lambda it: pre[it[0]]["v"][it[1]]
        rts = [pre[j]["rt"][c] for j, c in items]
        ats = [pre[j]["at"][c] for j, c in items]
        zero = jnp.zeros((L, W), F32)
        m1s, m2s = [], []
        for (j, c), rt, at in zip(items, rts, ats):
            bt, kt = pre[j]["bt"][c], pre[j]["kt"][c]
            lhs1 = jnp.concatenate([jnp.where(h1, at, zero), jnp.where(h1, rt, zero)], axis=0)
            lhs2 = jnp.concatenate([jnp.where(h1, zero, at), jnp.where(h1, zero, rt)], axis=0)
            m1s.append(jnp.where(tri_mask, _dot_nt(lhs1, jnp.concatenate([bt, kt], axis=0)), 0.0))
            m2s.append(jnp.where(tri_mask, _dot_nt(lhs2, jnp.concatenate([kt, bt], axis=0)), 0.0))
        fill()
        a_bds = [jnp.concatenate([jnp.where(h1, m1[0:L], 0.0), jnp.where(h1, 0.0, m2[0:L])], axis=0)
                 for m1, m2 in zip(m1s, m2s)]
        tinvs = _tri_inverse(a_bds, eye, masks)
        avs, o_vs = [], []
        for it, m1, m2 in zip(items, m1s, m2s):
            vc = v_of(it)
            ak_m = jnp.concatenate(
                [jnp.where(h1, 0.0, m1[0:L]), jnp.where(h1, m2[0:L], 0.0)], axis=0)
            avs.append(_dot(ak_m, jnp.concatenate([vc, vc], axis=0)))
            qk = jnp.where(h1, m2[L:], m1[L:])
            vx = jnp.concatenate([jnp.where(h1, 0.0, vc), jnp.where(h1, vc, 0.0)], axis=0)
            o_vs.append(_dot(qk, vx))
        qbs = [jnp.where(h1, m1[L:], m2[L:]) for m1, m2 in zip(m1s, m2s)]
        u0s = [_dot(t, av) for t, av in zip(tinvs, avs)]
        tas = [jnp.where(blk_mask, _dot(t, jnp.concatenate([at, at], axis=0)), 0.0)
               for t, at in zip(tinvs, ats)]
        for i, ((j, c), ta, u0) in enumerate(zip(items, tas, u0s)):
            cl_i = pre[j]["cl"][c]
            cl_last = cl_i[L - 1:L, :]
            e_end = jnp.exp(cl_last - cl_i)
            b_end = pre[j]["bh"][c] * e_end
            b_st = jnp.concatenate([jnp.where(h1, b_end, 0.0), jnp.where(h1, 0.0, b_end)], axis=0)
            u0_bd = jnp.where(blk_mask, u0, 0.0)
            m_t = eye * jnp.exp(cl_last) + _dot_tn(b_st, ta)
            lhs_scr[i] = jnp.concatenate([m_t, ta, rts[i]], axis=0).astype(BF16)
            u0_scr[i] = u0_bd
            qb_scr[i] = qbs[i].astype(BF16)
            ov_scr[i] = o_vs[i]
            c_scr[i] = (_dot_tn(b_st, u0_bd)
                        + jnp.where(blk_mask, _dot_tn(pre[j]["k2"][c] * e_end, v_of((j, c))), 0.0))

        Hs = list(Hs)
        bigs = []
        for i, (j, c) in enumerate(items):
            big = jnp.dot(lhs_scr[i], Hs[j].astype(BF16), preferred_element_type=F32)
            Hs[j] = big[0:W] + c_scr[i]
            bigs.append(big)
            fill()
        outs = [[] for _ in range(ns)]
        for i, (j, c) in enumerate(items):
            u_bd = jnp.where(blk_mask, bigs[i][W:2 * W], 0.0) + u0_scr[i]
            outs[j].append(bigs[i][2 * W:] + _dot(qb_scr[i], u_bd) + ov_scr[i])

        for j in range(ns):
            sj = slice(j * W, (j + 1) * W)
            o = jnp.concatenate(outs[j], axis=0)
            mean = _split_dot(o, bd_ref[...]) * (1.0 / 64.0)
            oc = o - mean
            var = _split_dot(oc * oc, bd_ref[...]) * (1.0 / 64.0)
            y = (oc * lax.rsqrt(var + RWKV_GN_EPS) * ln_g[:, sj] + ln_b[:, sj]
                 + pre[j]["bonus"][slice(0, R)])
            o_ref[pl.ds(s, R), sj] = (y * pre[j]["g"][slice(0, R)]).astype(o_ref.dtype)
        while fillers:
            fill()
        return tuple(Hs)

    for thunk in prologue_pieces(0, 0):
        thunk()

    def body(t, Hs):
        Hs = run(0, Hs, 2 * t, prologue_pieces(2 * t + 1, 1))
        return run(1, Hs, 2 * t + 1, prologue_pieces(jnp.minimum(2 * t + 2, nsb - 1), 0))

    lax.fori_loop(0, nsb // 2, body, tuple(jnp.zeros((W, W), F32) for _ in range(ns)))


def _rwkv(p, seq, mu, w0, a0, k_k, k_a, r_k, ln_g, ln_b, wd, wa, wg):
    n = p.shape[0]
    nbatch = n // seq
    ns = 2
    wide = ns * LANES
    ngrp = RWKV_WIDTH // wide
    nb = 4 if seq % (8 * CHUNK) == 0 else 1

    def col(off):
        return pl.BlockSpec((seq, wide), lambda b, q, off=off: (b, off + q))

    def par(off=0):
        return pl.BlockSpec((1, wide), lambda b, q, off=off: (0, off + q))

    in_specs = [
        col(0), col(ngrp), col(2 * ngrp),
        pl.BlockSpec((seq, 2 * LANES), lambda b, q: (b, 3 * RWKV_WIDTH // (2 * LANES))),
        par(0), par(ngrp), par(2 * ngrp),
        pl.BlockSpec((1, 2 * LANES), lambda b, q: (0, 3 * RWKV_WIDTH // (2 * LANES))),
        par(), par(), par(), par(), par(), par(), par(),
        pl.BlockSpec((DECAY_LORA, wide), lambda b, q: (0, q)),
        pl.BlockSpec((ICLR_LORA, wide), lambda b, q: (0, q)),
        pl.BlockSpec((GATE_LORA, wide), lambda b, q: (0, q)),
        pl.BlockSpec((LANES, LANES), lambda b, q: (0, 0)),
    ]
    head_id = jnp.arange(LANES) // RWKV_HEAD_DIM
    bd_ones = (head_id[:, None] == head_id[None, :]).astype(BF16)
    return pl.pallas_call(
        functools.partial(_rwkv_kernel, nb=nb, ns=ns),
        grid=(nbatch, ngrp),
        in_specs=in_specs,
        out_specs=pl.BlockSpec((seq, wide), lambda b, q: (b, q)),
        out_shape=jax.ShapeDtypeStruct((n, RWKV_WIDTH), BF16),
        scratch_shapes=[
            pltpu.VMEM((nb * ns, 2 * LANES + CHUNK, LANES), BF16),
            pltpu.VMEM((nb * ns, LANES, LANES), F32),
            pltpu.VMEM((nb * ns, CHUNK, LANES), BF16),
            pltpu.VMEM((nb * ns, CHUNK, LANES), F32),
            pltpu.VMEM((nb * ns, LANES, LANES), F32),
            pltpu.VMEM((2, ns, 10, nb * CHUNK, LANES), F32),
        ],
        compiler_params=pltpu.CompilerParams(
            dimension_semantics=("parallel", "parallel"), vmem_limit_bytes=VMEM_LIMIT),
        name="rwkv",
    )(p, p, p, p, mu, mu, mu, mu, w0, a0, k_k, k_a, r_k, ln_g, ln_b, wd, wa, wg, bd_ones)


def _gdn_kernel(q_ref, k_ref, v_ref, z_ref, ab_ref, cwq_ref, cwk_ref, cwv_ref,
                alog_ref, dtb_ref, ng_ref, o_ref, lhs_scr, u_scr, qk_scr, c_scr, *, nb, ns):
    L = CHUNK
    W = LANES
    P = 2 * L
    R = nb * P
    seq = q_ref.shape[0]
    nsb = seq // R
    hd0 = pl.program_id(1) * ns

    row_r = _iota((R, W), 0)
    lane_r = _iota((R, W), 1)
    tin = row_r & (L - 1)
    row8 = _iota((8, W), 0)
    rm = _iota((W, W), 0)
    lm = _iota((W, W), 1)
    dif = rm - lm
    tin_m = rm & (L - 1)
    eye = (rm == lm).astype(F32)
    masks = _level_masks(rm, lm)

    lane1 = _iota((1, W), 1)
    pick = lambda ref, j: jnp.sum(jnp.where(lane1 == hd0 + j, ref[...], 0.0), axis=1, keepdims=True)
    neg_a = [-jnp.exp(pick(alog_ref, j)) for j in range(ns)]
    dt_b = [pick(dtb_ref, j) for j in range(ns)]
    cwq, cwk, cwv = cwq_ref[...], cwk_ref[...], cwv_ref[...]
    ng = ng_ref[...]
    scale = GDN_HEAD_DIM ** -0.5
    neg_inf = jnp.float32(-jnp.inf)

    def conv(x, prev8, cw):
        acc = x * cw[CONV_WIDTH - 1:CONV_WIDTH, :]
        for j in range(1, CONV_WIDTH):
            xs = pltpu.roll(x, j, axis=0)
            ps = pltpu.roll(prev8, j, axis=0)
            head = jnp.where(row8 < j, ps, xs[0:8])
            xs = jnp.concatenate([head, xs[8:]], axis=0)
            acc = acc + xs * cw[CONV_WIDTH - 1 - j:CONV_WIDTH - j, :]
        return _silu(acc)

    def l2n(x):
        return x * lax.rsqrt(jnp.sum(x * x, axis=-1, keepdims=True) + L2_EPS)

    def body(sb, carry):
        Ss, pqs, pks, pvs = carry
        s = pl.multiple_of(sb * R, R)
        ab = ab_ref[pl.ds(s, R), :].astype(F32)

        pre, raws = [], []
        for j in range(ns):
            sj = slice(j * W, (j + 1) * W)
            hd = hd0 + j
            q_raw = q_ref[pl.ds(s, R), sj].astype(F32)
            k_raw = k_ref[pl.ds(s, R), sj].astype(F32)
            v_raw = v_ref[pl.ds(s, R), sj].astype(F32)
            raws.append((q_raw[R - 8:, :], k_raw[R - 8:, :], v_raw[R - 8:, :]))
            q = l2n(conv(q_raw, pqs[j], cwq[:, sj])) * scale
            k = l2n(conv(k_raw, pks[j], cwk[:, sj]))
            v = conv(v_raw, pvs[j], cwv[:, sj])
            a_in = jnp.sum(jnp.where(lane_r == hd, ab, 0.0), axis=1, keepdims=True)
            b_in = jnp.sum(jnp.where(lane_r == hd + GDN_HEADS, ab, 0.0), axis=1, keepdims=True)
            beta = _sigmoid(b_in)
            g_log = neg_a[j] * _softplus(a_in + dt_b[j])
            gc = _chunk_cumsum(jnp.broadcast_to(g_log, (R, W)), tin)
            e_gc = jnp.exp(gc)
            kb = k * beta
            pre.append(dict(q=q, k=k, gc=gc, kb=kb, vb=v * beta, kbe=kb * e_gc, qe=q * e_gc))

        items = [(j, slice(i * P, (i + 1) * P)) for i in range(nb) for j in range(ns)]
        decs = []
        for j, sl in items:
            gc_p = pre[j]["gc"][sl]
            dlog = gc_p - jnp.transpose(gc_p)
            dlog = jnp.where(dif >= 0, dlog, neg_inf)
            decs.append(jnp.exp(jnp.where(dif <= tin_m, dlog, neg_inf)))
        negAs = [jnp.where(dif > 0, _dot_nt(pre[j]["kb"][sl], pre[j]["k"][sl]) * dec, 0.0) * -1.0
                 for (j, sl), dec in zip(items, decs)]
        qks = [_dot_nt(pre[j]["q"][sl], pre[j]["k"][sl]) * dec for (j, sl), dec in zip(items, decs)]
        tinvs = _tri_inverse(negAs, eye, masks)
        uws = [_dot(t, jnp.concatenate([pre[j]["vb"][sl], pre[j]["kbe"][sl]], axis=1))
               for t, (j, sl) in zip(tinvs, items)]
        steps = []
        for i in range(nb):
            for c in range(2):
                for j in range(ns):
                    it = i * ns + j
                    sl, uw, qk = items[it][1], uws[it], qks[it]
                    cs = slice(sl.start + c * L, sl.start + (c + 1) * L)
                    hs = slice(c * L, (c + 1) * L)
                    gc_c = pre[j]["gc"][cs]
                    gc_last = gc_c[L - 1:L, :]
                    k_dec = pre[j]["k"][cs] * jnp.exp(gc_last - gc_c)
                    u_c, w_c = uw[hs, 0:W], uw[hs, W:]
                    m_c = eye * jnp.exp(gc_last) - _dot_tn(k_dec, w_c)
                    n_st = len(steps)
                    lhs_scr[n_st] = jnp.concatenate(
                        [m_c, w_c, pre[j]["qe"][cs]], axis=0).astype(BF16)
                    u_scr[n_st] = u_c
                    qk_scr[n_st] = qk[hs].astype(BF16)
                    c_scr[n_st] = _dot_tn(k_dec, u_c)
                    steps.append(j)

        Ss = list(Ss)
        bigs = []
        for i, j in enumerate(steps):
            big = jnp.dot(lhs_scr[i], Ss[j].astype(BF16), preferred_element_type=F32)
            Ss[j] = big[0:W] + c_scr[i]
            bigs.append(big)
        outs = [[] for _ in range(ns)]
        for i, j in enumerate(steps):
            v_new = u_scr[i] - bigs[i][W:W + L]
            outs[j].append(bigs[i][W + L:]
                           + _dot(qk_scr[i], jnp.concatenate([v_new, v_new], axis=0)))

        for j in range(ns):
            sj = slice(j * W, (j + 1) * W)
            o = jnp.concatenate(outs[j], axis=0)
            o = o * lax.rsqrt(jnp.mean(o * o, axis=-1, keepdims=True) + NORM_EPS) * ng
            o_ref[pl.ds(s, R), sj] = (o * _silu(z_ref[pl.ds(s, R), sj].astype(F32))
                                      ).astype(o_ref.dtype)
        return (tuple(Ss), tuple(t[0] for t in raws), tuple(t[1] for t in raws),
                tuple(t[2] for t in raws))

    z8 = jnp.zeros((8, W), F32)
    init = (tuple(jnp.zeros((W, W), F32) for _ in range(ns)), (z8,) * ns, (z8,) * ns, (z8,) * ns)
    lax.fori_loop(0, nsb, body, init)


def _gdn(p, seq, conv_w, a_log, dt_bias, norm_g):
    n = p.shape[0]
    nbatch = n // seq
    ns = 2
    wide = ns * LANES
    ngrp = GDN_HEADS // ns
    qoff = COL_QKV // wide
    nb = 2 if seq % (4 * CHUNK) == 0 else 1

    def col(off):
        return pl.BlockSpec((seq, wide), lambda b, h, off=off: (b, off + h))

    def cw(off):
        return pl.BlockSpec((CONV_WIDTH, wide), lambda b, h, off=off: (0, off + h))

    one = pl.BlockSpec((1, LANES), lambda b, h: (0, 0))
    in_specs = [
        col(qoff), col(qoff + ngrp), col(qoff + 2 * ngrp), col(COL_Z // wide),
        pl.BlockSpec((seq, LANES), lambda b, h: (b, COL_AB // LANES)),
        cw(0), cw(ngrp), cw(2 * ngrp), one, one, one,
    ]
    n_steps = 2 * nb * ns
    return pl.pallas_call(
        functools.partial(_gdn_kernel, nb=nb, ns=ns),
        grid=(nbatch, ngrp),
        in_specs=in_specs,
        out_specs=pl.BlockSpec((seq, wide), lambda b, h: (b, h)),
        out_shape=jax.ShapeDtypeStruct((n, GDN_WIDTH), BF16),
        scratch_shapes=[
            pltpu.VMEM((n_steps, 2 * CHUNK + LANES, LANES), BF16),
            pltpu.VMEM((n_steps, CHUNK, LANES), F32),
            pltpu.VMEM((n_steps, CHUNK, LANES), BF16),
            pltpu.VMEM((n_steps, LANES, LANES), F32),
        ],
        compiler_params=pltpu.CompilerParams(
            dimension_semantics=("parallel", "parallel"), vmem_limit_bytes=VMEM_LIMIT),
        name="gdn",
    )(p, p, p, p, p, conv_w, conv_w, conv_w, a_log, dt_bias, norm_g)


def _mixout_kernel(x_ref, ya_ref, yb_ref, ga_ref, gb_ref, wpa_ref, wpb_ref, wout_ref, g2_ref,
                   *rest, with_router):
    if with_router:
        rt_ref, xo_ref, h_ref, comb_ref = rest
    else:
        xo_ref, h_ref = rest
    ya = _dot(ya_ref[...], wpa_ref[...])
    yb = _dot(yb_ref[...], wpb_ref[...])
    y = _sigmoid(ga_ref[...].astype(F32)) * ya + _sigmoid(gb_ref[...].astype(F32)) * yb
    xn = x_ref[...] + _dot(y, wout_ref[...])
    xo_ref[...] = xn
    h = xn * lax.rsqrt(jnp.mean(xn * xn, axis=-1, keepdims=True) + NORM_EPS) * g2_ref[...]
    h_ref[...] = h.astype(h_ref.dtype)
    if with_router:
        rt = rt_ref[...]
        h_hi = h.astype(BF16)
        h_lo = (h - h_hi.astype(F32)).astype(BF16)
        r_hi = rt.astype(BF16)
        r_lo = (rt - r_hi.astype(F32)).astype(BF16)
        logits = (jnp.dot(h_hi, r_hi, preferred_element_type=F32)
                  + jnp.dot(h_lo, r_hi, preferred_element_type=F32)
                  + jnp.dot(h_hi, r_lo, preferred_element_type=F32))
        lane = _iota(logits.shape, 1)
        neg = jnp.float32(-jnp.inf)
        logits = jnp.where(lane < N_EXPERTS, logits, neg)
        m1 = jnp.max(logits, axis=1, keepdims=True)
        i1 = jnp.min(jnp.where(logits == m1, lane, LANES), axis=1, keepdims=True)
        l2 = jnp.where(lane == i1, neg, logits)
        m2 = jnp.max(l2, axis=1, keepdims=True)
        i2 = jnp.min(jnp.where(l2 == m2, lane, LANES), axis=1, keepdims=True)
        e2 = jnp.exp(m2 - m1)
        g1 = 1.0 / (1.0 + e2)
        g2 = e2 / (1.0 + e2)
        comb_ref[...] = (jnp.where(lane == 0, i1.astype(F32), 0.0)
                         + jnp.where(lane == 1, i2.astype(F32), 0.0)
                         + jnp.where(lane == 2, g1, 0.0) + jnp.where(lane == 3, g2, 0.0))


def _mixout(x2, ya, yb, p, wpa, wpb, wout, g2, router, tm):
    n = x2.shape[0]
    with_router = router is not None
    row = lambda w: pl.BlockSpec((tm, w), lambda i: (i, 0))
    full = lambda a: pl.BlockSpec(a.shape, lambda i: (0, 0))
    in_specs = [
        row(D_MODEL), row(RWKV_WIDTH), row(GDN_WIDTH),
        pl.BlockSpec((tm, D_MODEL), lambda i: (i, COL_GATE // D_MODEL)),
        pl.BlockSpec((tm, D_MODEL), lambda i: (i, COL_GATE // D_MODEL + 1)),
        full(wpa), full(wpb), full(wout), full(g2),
    ]
    args = [x2, ya, yb, p, p, wpa, wpb, wout, g2]
    out_specs = [row(D_MODEL), row(D_MODEL)]
    out_shape = [jax.ShapeDtypeStruct((n, D_MODEL), F32),
                 jax.ShapeDtypeStruct((n, D_MODEL), F32 if with_router else BF16)]
    if with_router:
        in_specs.append(full(router))
        args.append(router)
        out_specs.append(row(LANES))
        out_shape.append(jax.ShapeDtypeStruct((n, LANES), F32))
    return pl.pallas_call(
        functools.partial(_mixout_kernel, with_router=with_router),
        grid=(n // tm,),
        in_specs=in_specs,
        out_specs=out_specs,
        out_shape=out_shape,
        compiler_params=pltpu.CompilerParams(
            dimension_semantics=("parallel",), vmem_limit_bytes=VMEM_LIMIT),
        name="mixout_router" if with_router else "mixout",
    )(*args)


def _rms_out(y, fg_ref):
    return y * lax.rsqrt(jnp.mean(y * y, axis=-1, keepdims=True) + NORM_EPS) * fg_ref[...]


def _gate_up(wg, wu, tf):
    parts = []
    for j in range(D_FF // tf):
        parts += [wg[..., j * tf:(j + 1) * tf], wu[..., j * tf:(j + 1) * tf]]
    return jnp.concatenate(parts, axis=-1).astype(BF16)


def _swiglu_chunk(h, wgu, wd):
    tf = wd.shape[0]
    gu = jnp.dot(h, wgu, preferred_element_type=F32)
    act = _silu(gu[:, 0:tf]) * gu[:, tf:]
    return jnp.dot(act.astype(BF16), wd, preferred_element_type=F32)


def _ffn_kernel(h_ref, x_ref, wgu_ref, wd_ref, *rest, final):
    fg_ref = rest[0] if final else None
    o_ref, acc_ref = rest[-2:]
    j = pl.program_id(1)

    @pl.when(j == 0)
    def _():
        acc_ref[...] = jnp.zeros_like(acc_ref)

    acc_ref[...] += _swiglu_chunk(h_ref[...], wgu_ref[...], wd_ref[...])

    @pl.when(j == pl.num_programs(1) - 1)
    def _():
        y = x_ref[...] + acc_ref[...]
        o_ref[...] = _rms_out(y, fg_ref) if final else y


def _ffn(h, x2, wgu, wd, final_g, tm, tf):
    n = x2.shape[0]
    final = final_g is not None
    row = lambda w: pl.BlockSpec((tm, w), lambda i, j: (i, 0))
    in_specs = [
        row(D_MODEL), row(D_MODEL),
        pl.BlockSpec((D_MODEL, 2 * tf), lambda i, j: (0, j)),
        pl.BlockSpec((tf, D_MODEL), lambda i, j: (j, 0)),
    ]
    args = [h, x2, wgu, wd]
    if final:
        in_specs.append(pl.BlockSpec((1, D_MODEL), lambda i, j: (0, 0)))
        args.append(final_g)
    return pl.pallas_call(
        functools.partial(_ffn_kernel, final=final),
        grid=(n // tm, D_FF // tf),
        in_specs=in_specs,
        out_specs=row(D_MODEL),
        out_shape=jax.ShapeDtypeStruct((n, D_MODEL), F32),
        scratch_shapes=[pltpu.VMEM((tm, D_MODEL), F32)],
        compiler_params=pltpu.CompilerParams(
            dimension_semantics=("parallel", "arbitrary"), vmem_limit_bytes=VMEM_LIMIT),
        name="ffn",
    )(*args)


MOE_ROWS = 512


def _plan_kernel(sel_ref, pos_ref, tile_ref, cnt_scr, *, tm, n_tiles):
    ph = pl.program_id(0)
    i = pl.program_id(1)
    sel = sel_ref[...]
    lane = _iota((tm, LANES), 1)
    e1 = jnp.sum(jnp.where(lane == 0, sel, 0.0), axis=1, keepdims=True).astype(jnp.int32)
    e2 = jnp.sum(jnp.where(lane == 1, sel, 0.0), axis=1, keepdims=True).astype(jnp.int32)
    oh1 = lane == e1
    oh2 = lane == e2
    both = jnp.where(oh1, 1.0, 0.0) + jnp.where(oh2, 1.0, 0.0)

    @pl.when((ph == 0) & (i == 0))
    def _():
        cnt_scr[...] = jnp.zeros_like(cnt_scr)

    @pl.when(ph == 0)
    def _():
        cnt_scr[0:1, :] += jnp.sum(both, axis=0, keepdims=True)

    @pl.when((ph == 1) & (i == 0))
    def _():
        cnt = cnt_scr[0:1, :]
        padded = jnp.floor((cnt + (MOE_ROWS - 1)) * (1.0 / MOE_ROWS)) * MOE_ROWS
        rl = _iota((LANES, LANES), 0)
        cl = _iota((LANES, LANES), 1)
        upper = jnp.where(rl < cl, 1.0, 0.0)
        starts = _dot_hi(jnp.broadcast_to(padded, (8, LANES)), upper)[0:1, :]
        ends = starts + padded
        cnt_scr[0:1, :] = starts
        tl = _iota((n_tiles, LANES), 1)
        t0 = (_iota((n_tiles, LANES), 0) * MOE_ROWS).astype(F32)
        lane_ok = tl < N_EXPERTS
        owner = jnp.sum(jnp.where(lane_ok, jnp.where(t0 >= ends, 1.0, 0.0), 0.0),
                        axis=1, keepdims=True)
        used = jnp.sum(jnp.where(lane_ok, padded, 0.0), axis=1, keepdims=True)
        valid = t0[:, 0:1] < used
        last_e = jnp.max(jnp.where(lane_ok, jnp.where(padded > 0.0, tl.astype(F32), 0.0), 0.0),
                         axis=1, keepdims=True)
        ex = jnp.where(valid, jnp.minimum(owner, N_EXPERTS - 1.0), last_e)
        tile_ref[...] = (jnp.where(tl == 0, ex, 0.0)
                         + jnp.where(tl == 1, jnp.where(valid, 1.0, 0.0), 0.0)).astype(jnp.int32)

    @pl.when(ph == 1)
    def _():
        rr = _iota((tm, tm), 0)
        cc = _iota((tm, tm), 1)
        before = jnp.where(cc < rr, 1.0, 0.0).astype(BF16)
        base = cnt_scr[0:1, :] + jnp.dot(before, both.astype(BF16), preferred_element_type=F32)
        p1 = jnp.sum(jnp.where(oh1, base, 0.0), axis=1, keepdims=True)
        p2 = jnp.sum(jnp.where(oh2, base, 0.0), axis=1, keepdims=True)
        pos_ref[...] = (jnp.where(lane == 0, p1, 0.0) + jnp.where(lane == 1, p2, 0.0)
                        ).astype(jnp.int32)
        cnt_scr[0:1, :] += jnp.sum(both, axis=0, keepdims=True)


def _plan(sel, tm, n_tiles):
    n = sel.shape[0]
    return pl.pallas_call(
        functools.partial(_plan_kernel, tm=tm, n_tiles=n_tiles),
        grid=(2, n // tm),
        in_specs=[pl.BlockSpec((tm, LANES), lambda ph, i: (i, 0))],
        out_specs=[pl.BlockSpec((tm, LANES), lambda ph, i: (i * ph, 0)),
                   pl.BlockSpec((n_tiles, LANES), lambda ph, i: (0, 0))],
        out_shape=[jax.ShapeDtypeStruct((n, LANES), jnp.int32),
                   jax.ShapeDtypeStruct((n_tiles, LANES), jnp.int32)],
        scratch_shapes=[pltpu.VMEM((8, LANES), F32)],
        compiler_params=pltpu.CompilerParams(
            dimension_semantics=("arbitrary", "arbitrary"), vmem_limit_bytes=VMEM_LIMIT),
        name="moe_plan",
    )(sel)


def _row_copy(src_ref, src_row, dst_ref, dst_row, sem):
    return pltpu.make_async_copy(src_ref.at[pl.ds(src_row, 1)], dst_ref.at[pl.ds(dst_row, 1)], sem)


def _dispatch_kernel(pos_ref, h_ref, xs_in_ref, xs_ref, sem, *, tm):
    del xs_in_ref

    def start(r, c):
        _row_copy(h_ref, r, xs_ref, pos_ref[2 * r], sem).start()
        _row_copy(h_ref, r, xs_ref, pos_ref[2 * r + 1], sem).start()
        return c

    def wait(r, c):
        _row_copy(h_ref, 0, xs_ref, 0, sem).wait()
        _row_copy(h_ref, 0, xs_ref, 0, sem).wait()
        return c

    lax.fori_loop(0, tm, start, 0, unroll=8)
    lax.fori_loop(0, tm, wait, 0, unroll=8)


def _dispatch(pos_flat, h, n_rows, tm):
    n = h.shape[0]
    xs0 = jnp.zeros((n_rows, D_MODEL), F32)
    return pl.pallas_call(
        functools.partial(_dispatch_kernel, tm=tm),
        grid=(n // tm,),
        in_specs=[
            pl.BlockSpec((2 * tm,), lambda i: (i,), memory_space=pltpu.SMEM),
            pl.BlockSpec((tm, D_MODEL), lambda i: (i, 0)),
            pl.BlockSpec(memory_space=pl.ANY),
        ],
        out_specs=pl.BlockSpec(memory_space=pl.ANY),
        out_shape=jax.ShapeDtypeStruct((n_rows, D_MODEL), F32),
        scratch_shapes=[pltpu.SemaphoreType.DMA(())],
        input_output_aliases={2: 0},
        compiler_params=pltpu.CompilerParams(
            dimension_semantics=("arbitrary",), vmem_limit_bytes=VMEM_LIMIT),
        name="moe_dispatch",
    )(pos_flat, h, xs0)


def _experts_kernel(te_ref, tv_ref, xs_ref, wgu_ref, wd_ref, ys_ref, xb_scr, acc_scr):
    t = pl.program_id(0)
    j = pl.program_id(1)
    del te_ref

    @pl.when(j == 0)
    def _():
        xb_scr[...] = xs_ref[...].astype(BF16)
        acc_scr[...] = jnp.zeros_like(acc_scr)

    @pl.when(tv_ref[t] == 1)
    def _():
        acc_scr[...] += _swiglu_chunk(xb_scr[...], wgu_ref[0], wd_ref[0])

    @pl.when(j == pl.num_programs(1) - 1)
    def _():
        ys_ref[...] = acc_scr[...]


def _experts(tile_e, tile_v, xs, wgu, wd, tf):
    n_rows = xs.shape[0]
    grid_spec = pltpu.PrefetchScalarGridSpec(
        num_scalar_prefetch=2,
        grid=(n_rows // MOE_ROWS, D_FF // tf),
        in_specs=[
            pl.BlockSpec((MOE_ROWS, D_MODEL), lambda t, j, te, tv: (t, 0)),
            pl.BlockSpec((1, D_MODEL, 2 * tf), lambda t, j, te, tv: (te[t], 0, j)),
            pl.BlockSpec((1, tf, D_MODEL), lambda t, j, te, tv: (te[t], j, 0)),
        ],
        out_specs=pl.BlockSpec((MOE_ROWS, D_MODEL), lambda t, j, te, tv: (t, 0)),
        scratch_shapes=[pltpu.VMEM((MOE_ROWS, D_MODEL), BF16), pltpu.VMEM((MOE_ROWS, D_MODEL), F32)],
    )
    return pl.pallas_call(
        _experts_kernel,
        grid_spec=grid_spec,
        out_shape=jax.ShapeDtypeStruct((n_rows, D_MODEL), F32),
        compiler_params=pltpu.CompilerParams(
            dimension_semantics=("parallel", "arbitrary"), vmem_limit_bytes=VMEM_LIMIT),
        name="moe_experts",
    )(tile_e, tile_v, xs, wgu, wd)


def _combine_kernel(pos_ref, x_ref, sel_ref, *rest, tm, final):
    fg_ref = rest[0] if final else None
    ys_ref, o_ref, buf, sem = rest[-4:]

    def start(r, c):
        _row_copy(ys_ref, pos_ref[2 * r], buf.at[0], r, sem).start()
        _row_copy(ys_ref, pos_ref[2 * r + 1], buf.at[1], r, sem).start()
        return c

    def wait(r, c):
        _row_copy(ys_ref, 0, buf.at[0], 0, sem).wait()
        _row_copy(ys_ref, 0, buf.at[1], 0, sem).wait()
        return c

    lax.fori_loop(0, tm, start, 0, unroll=8)
    lax.fori_loop(0, tm, wait, 0, unroll=8)
    sel = sel_ref[...]
    lane = _iota(sel.shape, 1)
    g1 = jnp.sum(jnp.where(lane == 2, sel, 0.0), axis=1, keepdims=True)
    g2 = jnp.sum(jnp.where(lane == 3, sel, 0.0), axis=1, keepdims=True)
    y = x_ref[...] + g1 * buf[0] + g2 * buf[1]
    o_ref[...] = _rms_out(y, fg_ref) if final else y


def _combine(pos_flat, x2, sel, ys, final_g, tm):
    n = x2.shape[0]
    final = final_g is not None
    in_specs = [
        pl.BlockSpec((2 * tm,), lambda i: (i,), memory_space=pltpu.SMEM),
        pl.BlockSpec((tm, D_MODEL), lambda i: (i, 0)),
        pl.BlockSpec((tm, LANES), lambda i: (i, 0)),
    ]
    args = [pos_flat, x2, sel]
    if final:
        in_specs.append(pl.BlockSpec((1, D_MODEL), lambda i: (0, 0)))
        args.append(final_g)
    in_specs.append(pl.BlockSpec(memory_space=pl.ANY))
    args.append(ys)
    return pl.pallas_call(
        functools.partial(_combine_kernel, tm=tm, final=final),
        grid=(n // tm,),
        in_specs=in_specs,
        out_specs=pl.BlockSpec((tm, D_MODEL), lambda i: (i, 0)),
        out_shape=jax.ShapeDtypeStruct((n, D_MODEL), F32),
        scratch_shapes=[pltpu.VMEM((2, tm, D_MODEL), F32), pltpu.SemaphoreType.DMA(())],
        compiler_params=pltpu.CompilerParams(
            dimension_semantics=("arbitrary",), vmem_limit_bytes=VMEM_LIMIT),
        name="moe_combine",
    )(*args)


def _moe(h, x2, sel, wgu, wd, final_g, tm, tf):
    n = x2.shape[0]
    n_tiles = pl.cdiv(2 * n, MOE_ROWS) + N_EXPERTS
    pos, tiles = _plan(sel, tm, n_tiles)
    pos_flat = pos[:, 0:2].reshape(2 * n)
    xs = _dispatch(pos_flat, h, n_tiles * MOE_ROWS, tm)
    ys = _experts(tiles[:, 0], tiles[:, 1], xs, wgu, wd, tf)
    return _combine(pos_flat, x2, sel, ys, final_g, tm)


def _tile(n, pref):
    t = min(pref, n)
    while n % t:
        t //= 2
    return t


def _pad_lanes(a, width=LANES):
    return jnp.pad(a, ((0, 0), (0, width - a.shape[-1])))


def kernel(x, norm1_g, w_in, tshift_mu, w0, w_decay_up, a0, w_iclr_up, w_gate_up, k_k, k_a, r_k,
           lnx_g, lnx_b, w_pa, conv_w, a_log, dt_bias, gdn_norm_g, w_pb, w_out, norm2_g,
           ffn_w_gate, ffn_w_up, ffn_w_down, moe_router, moe_w_gate, moe_w_up, moe_w_down, final_g):
    nb, seq, d = x.shape
    n = nb * seq
    x2 = x.reshape(n, d)
    tm_in = _tile(n, 1024)
    tm = _tile(n, 512)
    rw = 1792
    gq = 3 * GDN_WIDTH
    for layer in range(DEPTH):
        wi = w_in[layer]
        ab = wi[:, rw + gq:rw + gq + 2 * GDN_HEADS]
        zc = wi[:, rw + gq + 2 * GDN_HEADS:rw + gq + 2 * GDN_HEADS + GDN_WIDTH]
        gate = wi[:, rw + gq + 2 * GDN_HEADS + GDN_WIDTH:]
        w_cat = jnp.concatenate(
            [wi[:, :rw + gq], zc, ab, jnp.zeros((d, COL_GATE - COL_AB - 2 * GDN_HEADS), F32), gate],
            axis=1).astype(BF16)
        p = _inproj(x2, norm1_g[layer][None, :], w_cat, tm_in, 1536)
        one = lambda a: a[layer][None, :]
        ya = _rwkv(p, seq, one(tshift_mu), one(w0), one(a0), one(k_k), one(k_a), one(r_k),
                   one(lnx_g), one(lnx_b), w_decay_up[layer].astype(BF16),
                   w_iclr_up[layer].astype(BF16), w_gate_up[layer].astype(BF16))
        yb = _gdn(p, seq, conv_w[layer], _pad_lanes(one(a_log)), _pad_lanes(one(dt_bias)),
                  one(gdn_norm_g))
        j = layer // 2
        routed = layer % 2 == 1
        router = _pad_lanes(moe_router[j]) if routed else None
        res = _mixout(x2, ya, yb, p, w_pa[layer].astype(BF16), w_pb[layer].astype(BF16),
                      w_out[layer].astype(BF16), one(norm2_g), router, tm)
        fg = final_g[None, :] if layer == DEPTH - 1 else None
        if routed:
            x2, h, sel = res
            x2 = _moe(h, x2, sel, _gate_up(moe_w_gate[j], moe_w_up[j], FF_CHUNK),
                      moe_w_down[j].astype(BF16), fg, tm, FF_CHUNK)
        else:
            x2, h = res
            x2 = _ffn(h, x2, _gate_up(ffn_w_gate[j], ffn_w_up[j], FF_CHUNK),
                      ffn_w_down[j].astype(BF16), fg, tm, FF_CHUNK)
    return x2.reshape(nb, seq, d)
```

```python
import functools

import jax
import jax.numpy as jnp
from jax import lax
from jax.experimental import pallas as pl
from jax.experimental.pallas import tpu as pltpu

F32 = jnp.float32
BF16 = jnp.bfloat16
HI = lax.Precision.HIGHEST

D_MODEL = 1024
DEPTH = 2
RWKV_HEADS = 8
RWKV_HEAD_DIM = 64
RWKV_WIDTH = 512
DECAY_LORA = 64
ICLR_LORA = 64
GATE_LORA = 128
RWKV_GN_EPS = 64e-5
GDN_HEADS = 4
GDN_HEAD_DIM = 128
GDN_WIDTH = 512
CONV_WIDTH = 4
CHUNK = 64
D_FF = 2816
N_EXPERTS = 8
NORM_EPS = 1e-6
L2_EPS = 1e-6

LANES = 128
P_COLS = 6144
COL_QKV = 1792
COL_Z = 3328
COL_AB = 3840
COL_GATE = 4096
VMEM_LIMIT = 56 * 1024 * 1024
FF_CHUNK = D_FF // 2


def _dot(a, b):
    return jnp.dot(a.astype(BF16), b.astype(BF16), preferred_element_type=F32)


def _dot_nt(a, b):
    return lax.dot_general(a.astype(BF16), b.astype(BF16), (((1,), (1,)), ((), ())),
                           preferred_element_type=F32)


def _dot_tn(a, b):
    return lax.dot_general(a.astype(BF16), b.astype(BF16), (((0,), (0,)), ((), ())),
                           preferred_element_type=F32)


def _dot_hi(a, b):
    return jnp.dot(a, b, preferred_element_type=F32, precision=HI)


def _softplus(x):
    return jnp.maximum(x, 0.0) + jnp.log(1.0 + jnp.exp(-jnp.abs(x)))


def _sigmoid(x):
    return 1.0 / (1.0 + jnp.exp(-x))


def _silu(x):
    return x * _sigmoid(x)


def _iota(shape, dim):
    return lax.broadcasted_iota(jnp.int32, shape, dim)


def _inproj_kernel(x_ref, g_ref, w_ref, p_ref, h_scr):
    @pl.when(pl.program_id(1) == 0)
    def _():
        x = x_ref[...]
        h = x * lax.rsqrt(jnp.mean(x * x, axis=-1, keepdims=True) + NORM_EPS) * g_ref[...]
        h_scr[...] = h.astype(BF16)

    p_ref[...] = jnp.dot(h_scr[...], w_ref[...], preferred_element_type=F32).astype(p_ref.dtype)


def _inproj(x2, g, w_cat, tm, tn):
    n = x2.shape[0]
    return pl.pallas_call(
        _inproj_kernel,
        grid=(n // tm, P_COLS // tn),
        in_specs=[
            pl.BlockSpec((tm, D_MODEL), lambda i, j: (i, 0)),
            pl.BlockSpec((1, D_MODEL), lambda i, j: (0, 0)),
            pl.BlockSpec((D_MODEL, tn), lambda i, j: (0, j)),
        ],
        out_specs=pl.BlockSpec((tm, tn), lambda i, j: (i, j)),
        out_shape=jax.ShapeDtypeStruct((n, P_COLS), BF16),
        scratch_shapes=[pltpu.VMEM((tm, D_MODEL), BF16)],
        compiler_params=pltpu.CompilerParams(
            dimension_semantics=("parallel", "arbitrary"), vmem_limit_bytes=VMEM_LIMIT),
        name="inproj",
    )(x2, g, w_cat)


def _split_dot(x, m):
    hi = x.astype(BF16)
    lo = (x - hi.astype(F32)).astype(BF16)
    return (jnp.dot(hi, m, preferred_element_type=F32) + jnp.dot(lo, m, preferred_element_type=F32))


def _chunk_cumsum(x, tin):
    y = x
    sh = 1
    while sh < CHUNK:
        y = y + jnp.where(tin >= sh, pltpu.roll(y, sh, axis=0), 0.0)
        sh *= 2
    return y


def _level_masks(rm, lm):
    out = []
    for k in range(6):
        x = rm >> k
        y = lm >> k
        out.append((((x ^ y) + 2 * (1 - (x & 1))) == 1).astype(F32))
    return out


def _tri_inverse(mats, eye, masks, between=lambda: None):
    ts = [eye + a * masks[0] for a in mats]
    for k in range(1, 6):
        xs = [_dot(a * masks[k], t) for a, t in zip(mats, ts)]
        between()
        ts = [t + _dot(t, x) for t, x in zip(ts, xs)]
        between()
    return ts


def _rwkv_kernel(r_ref, k_ref, v_ref, lo_ref, mur_ref, muk_ref, muv_ref, mulo_ref,
                 w0_ref, a0_ref, kk_ref, ka_ref, rk_ref, lg_ref, lb_ref,
                 wd_ref, wa_ref, wg_ref, bd_ref, o_ref,
                 lhs_scr, u0_scr, qb_scr, ov_scr, c_scr, pre_scr, *, nb, ns):
    L = CHUNK
    W = LANES
    R = nb * L
    seq = r_ref.shape[0]
    nsb = seq // R

    row_r = _iota((R, W), 0)
    row_r2 = _iota((R, 2 * W), 0)
    row_c = _iota((L, W), 0)
    lane = _iota((L, W), 1)
    h1 = lane < 64
    rm = _iota((W, W), 0)
    lm = _iota((W, W), 1)
    tri_mask = (lm & 63) < (rm & 63) + (rm >> 6)
    blk_mask = (rm >> 6) == (lm >> 6)
    eye = (rm == lm).astype(F32)
    masks = _level_masks(rm, lm)

    mur, muk, muv, mulo = mur_ref[...], muk_ref[...], muv_ref[...], mulo_ref[...]
    w0, a0, k_k, k_a, r_k = w0_ref[...], a0_ref[...], kk_ref[...], ka_ref[...], rk_ref[...]
    ln_g, ln_b = lg_ref[...], lb_ref[...]
    wd, wa, wg = wd_ref[...], wa_ref[...], wg_ref[...]

    def shift(x, prev_row, mu, rowi):
        xp = pltpu.roll(x, 1, axis=0)
        xp = jnp.where(rowi == 0, prev_row, xp)
        return x + (xp - x) * mu

    def last_row_before(ref, s, cols):
        rows = ref[pl.ds(pl.multiple_of(jnp.maximum(s - 16, 0), 16), 16), cols].astype(F32)
        return rows[15:16, :] * (s > 0).astype(F32)

    keys = ("v", "k2", "bh", "cl", "g", "bonus", "rt", "at", "bt", "kt")

    class Stored:
        def __init__(self, par, j):
            self.par, self.j = par, j

        def __getitem__(self, key):
            par, j, q = self.par, self.j, keys.index(key)

            class Rows:
                def __getitem__(self, rows):
                    return pre_scr[par, j, q, rows]
            return Rows()

    def prologue_pieces(sb, par):
        sh = {}

        def shared():
            s = pl.multiple_of(sb * R, R)
            lo_raw = lo_ref[pl.ds(s, R), :].astype(F32)
            lo = shift(lo_raw, last_row_before(lo_ref, s, slice(None)), mulo, row_r2)
            wl = lo[:, 0:DECAY_LORA]
            al = lo[:, DECAY_LORA:DECAY_LORA + ICLR_LORA]
            gl = lo[:, DECAY_LORA + ICLR_LORA:]
            sh["z"] = w0 + _dot(jnp.tanh(wl), wd)
            sh["a"] = _sigmoid(a0 + _dot(al, wa))
            sh["g"] = _dot(_sigmoid(gl), wg)
            for j in range(ns):
                sj = slice(j * W, (j + 1) * W)
                for name, ref, mu in (("r", r_ref, mur), ("k", k_ref, muk), ("v", v_ref, muv)):
                    sh[name, j] = shift(ref[pl.ds(s, R), sj].astype(F32), last_row_before(ref, s, sj),
                                        mu[:, sj], row_r)

        def piece(j, c):
            sj = slice(j * W, (j + 1) * W)
            r, k, v = sh["r", j][c], sh["k", j][c], sh["v", j][c]
            w_log = -_softplus(-sh["z"][c, sj]) - 0.5
            lw = -jnp.exp(w_log)
            a = sh["a"][c, sj]
            kk = k * k_k[:, sj]
            kk = kk * lax.rsqrt(_split_dot(kk * kk, bd_ref[...]) + L2_EPS)
            k2 = k * (1.0 + (a - 1.0) * k_a[:, sj])
            bh = kk * a
            cl = _chunk_cumsum(lw, row_c)
            e_neg = jnp.exp(-cl)
            vals = dict(
                v=v, k2=k2, bh=bh, cl=cl, g=sh["g"][c, sj],
                bonus=_split_dot(r * k2 * r_k[:, sj], bd_ref[...]) * v,
                rt=r * jnp.exp(cl), at=-kk * jnp.exp(cl - lw), bt=bh * e_neg, kt=k2 * e_neg)
            for q, key in enumerate(keys):
                pre_scr[par, j, q, c] = vals[key]

        return [shared] + [functools.partial(piece, j, slice(i * L, (i + 1) * L))
                           for i in range(nb) for j in range(ns)]

    def run(par, Hs, sb, fillers):
        pre = [Stored(par, j) for j in range(ns)]

        def fill():
            if fillers:
                fillers.pop(0)()

        s = pl.multiple_of(sb * R, R)
        items = [(j, slice(i * L, (i + 1) * L)) for i in range(nb) for j in range(ns)]
        v_of = lambda it: pre[it[0]]["v"][it[1]]
        rts = [pre[j]["rt"][c] for j, c in items]
        ats = [pre[j]["at"][c] for j, c in items]
        zero = jnp.zeros((L, W), F32)
        m1s, m2s = [], []
        for (j, c), rt, at in zip(items, rts, ats):
            bt, kt = pre[j]["bt"][c], pre[j]["kt"][c]
            lhs1 = jnp.concatenate([jnp.where(h1, at, zero), jnp.where(h1, rt, zero)], axis=0)
            lhs2 = jnp.concatenate([jnp.where(h1, zero, at), jnp.where(h1, zero, rt)], axis=0)
            m1s.append(jnp.where(tri_mask, _dot_nt(lhs1, jnp.concatenate([bt, kt], axis=0)), 0.0))
            m2s.append(jnp.where(tri_mask, _dot_nt(lhs2, jnp.concatenate([kt, bt], axis=0)), 0.0))
        fill()
        a_bds = [jnp.concatenate([jnp.where(h1, m1[0:L], 0.0), jnp.where(h1, 0.0, m2[0:L])], axis=0)
                 for m1, m2 in zip(m1s, m2s)]
        tinvs = _tri_inverse(a_bds, eye, masks)
        avs, o_vs = [], []
        for it, m1, m2 in zip(items, m1s, m2s):
            vc = v_of(it)
            ak_m = jnp.concatenate(
                [jnp.where(h1, 0.0, m1[0:L]), jnp.where(h1, m2[0:L], 0.0)], axis=0)
            avs.append(_dot(ak_m, jnp.concatenate([vc, vc], axis=0)))
            qk = jnp.where(h1, m2[L:], m1[L:])
            vx = jnp.concatenate([jnp.where(h1, 0.0, vc), jnp.where(h1, vc, 0.0)], axis=0)
            o_vs.append(_dot(qk, vx))
        qbs = [jnp.where(h1, m1[L:], m2[L:]) for m1, m2 in zip(m1s, m2s)]
        u0s = [_dot(t, av) for t, av in zip(tinvs, avs)]
        tas = [jnp.where(blk_mask, _dot(t, jnp.concatenate([at, at], axis=0)), 0.0)
               for t, at in zip(tinvs, ats)]
        for i, ((j, c), ta, u0) in enumerate(zip(items, tas, u0s)):
            cl_i = pre[j]["cl"][c]
            cl_last = cl_i[L - 1:L, :]
            e_end = jnp.exp(cl_last - cl_i)
            b_end = pre[j]["bh"][c] * e_end
            b_st = jnp.concatenate([jnp.where(h1, b_end, 0.0), jnp.where(h1, 0.0, b_end)], axis=0)
            u0_bd = jnp.where(blk_mask, u0, 0.0)
            m_t = eye * jnp.exp(cl_last) + _dot_tn(b_st, ta)
            lhs_scr[i] = jnp.concatenate([m_t, ta, rts[i]], axis=0).astype(BF16)
            u0_scr[i] = u0_bd
            qb_scr[i] = qbs[i].astype(BF16)
            ov_scr[i] = o_vs[i]
            c_scr[i] = (_dot_tn(b_st, u0_bd)
                        + jnp.where(blk_mask, _dot_tn(pre[j]["k2"][c] * e_end, v_of((j, c))), 0.0))

        Hs = list(Hs)
        bigs = []
        for i, (j, c) in enumerate(items):
            big = jnp.dot(lhs_scr[i], Hs[j].astype(BF16), preferred_element_type=F32)
            Hs[j] = big[0:W] + c_scr[i]
            bigs.append(big)
            fill()
        outs = [[] for _ in range(ns)]
        for i, (j, c) in enumerate(items):
            u_bd = jnp.where(blk_mask, bigs[i][W:2 * W], 0.0) + u0_scr[i]
            outs[j].append(bigs[i][2 * W:] + _dot(qb_scr[i], u_bd) + ov_scr[i])

        for j in range(ns):
            sj = slice(j * W, (j + 1) * W)
            o = jnp.concatenate(outs[j], axis=0)
            mean = _split_dot(o, bd_ref[...]) * (1.0 / 64.0)
            oc = o - mean
            var = _split_dot(oc * oc, bd_ref[...]) * (1.0 / 64.0)
            y = (oc * lax.rsqrt(var + RWKV_GN_EPS) * ln_g[:, sj] + ln_b[:, sj]
                 + pre[j]["bonus"][slice(0, R)])
            o_ref[pl.ds(s, R), sj] = (y * pre[j]["g"][slice(0, R)]).astype(o_ref.dtype)
        while fillers:
            fill()
        return tuple(Hs)

    for thunk in prologue_pieces(0, 0):
        thunk()

    def body(t, Hs):
        Hs = run(0, Hs, 2 * t, prologue_pieces(2 * t + 1, 1))
        return run(1, Hs, 2 * t + 1, prologue_pieces(jnp.minimum(2 * t + 2, nsb - 1), 0))

    lax.fori_loop(0, nsb // 2, body, tuple(jnp.zeros((W, W), F32) for _ in range(ns)))


def _rwkv(p, seq, mu, w0, a0, k_k, k_a, r_k, ln_g, ln_b, wd, wa, wg):
    n = p.shape[0]
    nbatch = n // seq
    ns = 2
    wide = ns * LANES
    ngrp = RWKV_WIDTH // wide
    nb = 4 if seq % (8 * CHUNK) == 0 else 1

    def col(off):
        return pl.BlockSpec((seq, wide), lambda b, q, off=off: (b, off + q))

    def par(off=0):
        return pl.BlockSpec((1, wide), lambda b, q, off=off: (0, off + q))

    in_specs = [
        col(0), col(ngrp), col(2 * ngrp),
        pl.BlockSpec((seq, 2 * LANES), lambda b, q: (b, 3 * RWKV_WIDTH // (2 * LANES))),
        par(0), par(ngrp), par(2 * ngrp),
        pl.BlockSpec((1, 2 * LANES), lambda b, q: (0, 3 * RWKV_WIDTH // (2 * LANES))),
        par(), par(), par(), par(), par(), par(), par(),
        pl.BlockSpec((DECAY_LORA, wide), lambda b, q: (0, q)),
        pl.BlockSpec((ICLR_LORA, wide), lambda b, q: (0, q)),
        pl.BlockSpec((GATE_LORA, wide), lambda b, q: (0, q)),
        pl.BlockSpec((LANES, LANES), lambda b, q: (0, 0)),
    ]
    head_id = jnp.arange(LANES) // RWKV_HEAD_DIM
    bd_ones = (head_id[:, None] == head_id[None, :]).astype(BF16)
    return pl.pallas_call(
        functools.partial(_rwkv_kernel, nb=nb, ns=ns),
        grid=(nbatch, ngrp),
        in_specs=in_specs,
        out_specs=pl.BlockSpec((seq, wide), lambda b, q: (b, q)),
        out_shape=jax.ShapeDtypeStruct((n, RWKV_WIDTH), BF16),
        scratch_shapes=[
            pltpu.VMEM((nb * ns, 2 * LANES + CHUNK, LANES), BF16),
            pltpu.VMEM((nb * ns, LANES, LANES), F32),
            pltpu.VMEM((nb * ns, CHUNK, LANES), BF16),
            pltpu.VMEM((nb * ns, CHUNK, LANES), F32),
            pltpu.VMEM((nb * ns, LANES, LANES), F32),
            pltpu.VMEM((2, ns, 10, nb * CHUNK, LANES), F32),
        ],
        compiler_params=pltpu.CompilerParams(
            dimension_semantics=("parallel", "parallel"), vmem_limit_bytes=VMEM_LIMIT),
        name="rwkv",
    )(p, p, p, p, mu, mu, mu, mu, w0, a0, k_k, k_a, r_k, ln_g, ln_b, wd, wa, wg, bd_ones)


def _gdn_kernel(q_ref, k_ref, v_ref, z_ref, ab_ref, cwq_ref, cwk_ref, cwv_ref,
                alog_ref, dtb_ref, ng_ref, o_ref, lhs_scr, u_scr, qk_scr, c_scr, *, nb, ns):
    L = CHUNK
    W = LANES
    P = 2 * L
    R = nb * P
    seq = q_ref.shape[0]
    nsb = seq // R
    hd0 = pl.program_id(1) * ns

    row_r = _iota((R, W), 0)
    lane_r = _iota((R, W), 1)
    tin = row_r & (L - 1)
    row8 = _iota((8, W), 0)
    rm = _iota((W, W), 0)
    lm = _iota((W, W), 1)
    dif = rm - lm
    tin_m = rm & (L - 1)
    eye = (rm == lm).astype(F32)
    masks = _level_masks(rm, lm)

    lane1 = _iota((1, W), 1)
    pick = lambda ref, j: jnp.sum(jnp.where(lane1 == hd0 + j, ref[...], 0.0), axis=1, keepdims=True)
    neg_a = [-jnp.exp(pick(alog_ref, j)) for j in range(ns)]
    dt_b = [pick(dtb_ref, j) for j in range(ns)]
    cwq, cwk, cwv = cwq_ref[...], cwk_ref[...], cwv_ref[...]
    ng = ng_ref[...]
    scale = GDN_HEAD_DIM ** -0.5
    neg_inf = jnp.float32(-jnp.inf)

    def conv(x, prev8, cw):
        acc = x * cw[CONV_WIDTH - 1:CONV_WIDTH, :]
        for j in range(1, CONV_WIDTH):
            xs = pltpu.roll(x, j, axis=0)
            ps = pltpu.roll(prev8, j, axis=0)
            head = jnp.where(row8 < j, ps, xs[0:8])
            xs = jnp.concatenate([head, xs[8:]], axis=0)
            acc = acc + xs * cw[CONV_WIDTH - 1 - j:CONV_WIDTH - j, :]
        return _silu(acc)

    def l2n(x):
        return x * lax.rsqrt(jnp.sum(x * x, axis=-1, keepdims=True) + L2_EPS)

    def body(sb, carry):
        Ss, pqs, pks, pvs = carry
        s = pl.multiple_of(sb * R, R)
        ab = ab_ref[pl.ds(s, R), :].astype(F32)

        pre, raws = [], []
        for j in range(ns):
            sj = slice(j * W, (j + 1) * W)
            hd = hd0 + j
            q_raw = q_ref[pl.ds(s, R), sj].astype(F32)
            k_raw = k_ref[pl.ds(s, R), sj].astype(F32)
            v_raw = v_ref[pl.ds(s, R), sj].astype(F32)
            raws.append((q_raw[R - 8:, :], k_raw[R - 8:, :], v_raw[R - 8:, :]))
            q = l2n(conv(q_raw, pqs[j], cwq[:, sj])) * scale
            k = l2n(conv(k_raw, pks[j], cwk[:, sj]))
            v = conv(v_raw, pvs[j], cwv[:, sj])
            a_in = jnp.sum(jnp.where(lane_r == hd, ab, 0.0), axis=1, keepdims=True)
            b_in = jnp.sum(jnp.where(lane_r == hd + GDN_HEADS, ab, 0.0), axis=1, keepdims=True)
            beta = _sigmoid(b_in)
            g_log = neg_a[j] * _softplus(a_in + dt_b[j])
            gc = _chunk_cumsum(jnp.broadcast_to(g_log, (R, W)), tin)
            e_gc = jnp.exp(gc)
            kb = k * beta
            pre.append(dict(q=q, k=k, gc=gc, kb=kb, vb=v * beta, kbe=kb * e_gc, qe=q * e_gc))

        items = [(j, slice(i * P, (i + 1) * P)) for i in range(nb) for j in range(ns)]
        decs = []
        for j, sl in items:
            gc_p = pre[j]["gc"][sl]
            dlog = gc_p - jnp.transpose(gc_p)
            dlog = jnp.where(dif >= 0, dlog, neg_inf)
            decs.append(jnp.exp(jnp.where(dif <= tin_m, dlog, neg_inf)))
        negAs = [jnp.where(dif > 0, _dot_nt(pre[j]["kb"][sl], pre[j]["k"][sl]) * dec, 0.0) * -1.0
                 for (j, sl), dec in zip(items, decs)]
        qks = [_dot_nt(pre[j]["q"][sl], pre[j]["k"][sl]) * dec for (j, sl), dec in zip(items, decs)]
        tinvs = _tri_inverse(negAs, eye, masks)
        uws = [_dot(t, jnp.concatenate([pre[j]["vb"][sl], pre[j]["kbe"][sl]], axis=1))
               for t, (j, sl) in zip(tinvs, items)]
        steps = []
        for i in range(nb):
            for c in range(2):
                for j in range(ns):
                    it = i * ns + j
                    sl, uw, qk = items[it][1], uws[it], qks[it]
                    cs = slice(sl.start + c * L, sl.start + (c + 1) * L)
                    hs = slice(c * L, (c + 1) * L)
                    gc_c = pre[j]["gc"][cs]
                    gc_last = gc_c[L - 1:L, :]
                    k_dec = pre[j]["k"][cs] * jnp.exp(gc_last - gc_c)
                    u_c, w_c = uw[hs, 0:W], uw[hs, W:]
                    m_c = eye * jnp.exp(gc_last) - _dot_tn(k_dec, w_c)
                    n_st = len(steps)
                    lhs_scr[n_st] = jnp.concatenate(
                        [m_c, w_c, pre[j]["qe"][cs]], axis=0).astype(BF16)
                    u_scr[n_st] = u_c
                    qk_scr[n_st] = qk[hs].astype(BF16)
                    c_scr[n_st] = _dot_tn(k_dec, u_c)
                    steps.append(j)

        Ss = list(Ss)
        bigs = []
        for i, j in enumerate(steps):
            big = jnp.dot(lhs_scr[i], Ss[j].astype(BF16), preferred_element_type=F32)
            Ss[j] = big[0:W] + c_scr[i]
            bigs.append(big)
        outs = [[] for _ in range(ns)]
        for i, j in enumerate(steps):
            v_new = u_scr[i] - bigs[i][W:W + L]
            outs[j].append(bigs[i][W + L:]
                           + _dot(qk_scr[i], jnp.concatenate([v_new, v_new], axis=0)))

        for j in range(ns):
            sj = slice(j * W, (j + 1) * W)
            o = jnp.concatenate(outs[j], axis=0)
            o = o * lax.rsqrt(jnp.mean(o * o, axis=-1, keepdims=True) + NORM_EPS) * ng
            o_ref[pl.ds(s, R), sj] = (o * _silu(z_ref[pl.ds(s, R), sj].astype(F32))
                                      ).astype(o_ref.dtype)
        return (tuple(Ss), tuple(t[0] for t in raws), tuple(t[1] for t in raws),
                tuple(t[2] for t in raws))

    z8 = jnp.zeros((8, W), F32)
    init = (tuple(jnp.zeros((W, W), F32) for _ in range(ns)), (z8,) * ns, (z8,) * ns, (z8,) * ns)
    lax.fori_loop(0, nsb, body, init, unroll=4)


def _gdn(p, seq, conv_w, a_log, dt_bias, norm_g):
    n = p.shape[0]
    nbatch = n // seq
    ns = 2
    wide = ns * LANES
    ngrp = GDN_HEADS // ns
    qoff = COL_QKV // wide
    nb = 2 if seq % (4 * CHUNK) == 0 else 1

    def col(off):
        return pl.BlockSpec((seq, wide), lambda b, h, off=off: (b, off + h))

    def cw(off):
        return pl.BlockSpec((CONV_WIDTH, wide), lambda b, h, off=off: (0, off + h))

    one = pl.BlockSpec((1, LANES), lambda b, h: (0, 0))
    in_specs = [
        col(qoff), col(qoff + ngrp), col(qoff + 2 * ngrp), col(COL_Z // wide),
        pl.BlockSpec((seq, LANES), lambda b, h: (b, COL_AB // LANES)),
        cw(0), cw(ngrp), cw(2 * ngrp), one, one, one,
    ]
    n_steps = 2 * nb * ns
    return pl.pallas_call(
        functools.partial(_gdn_kernel, nb=nb, ns=ns),
        grid=(nbatch, ngrp),
        in_specs=in_specs,
        out_specs=pl.BlockSpec((seq, wide), lambda b, h: (b, h)),
        out_shape=jax.ShapeDtypeStruct((n, GDN_WIDTH), BF16),
        scratch_shapes=[
            pltpu.VMEM((n_steps, 2 * CHUNK + LANES, LANES), BF16),
            pltpu.VMEM((n_steps, CHUNK, LANES), F32),
            pltpu.VMEM((n_steps, CHUNK, LANES), BF16),
            pltpu.VMEM((n_steps, LANES, LANES), F32),
        ],
        compiler_params=pltpu.CompilerParams(
            dimension_semantics=("parallel", "parallel"), vmem_limit_bytes=VMEM_LIMIT),
        name="gdn",
    )(p, p, p, p, p, conv_w, conv_w, conv_w, a_log, dt_bias, norm_g)


def _mixout_kernel(x_ref, ya_ref, yb_ref, ga_ref, gb_ref, wpa_ref, wpb_ref, wout_ref, g2_ref,
                   *rest, with_router):
    if with_router:
        rt_ref, xo_ref, h_ref, comb_ref = rest
    else:
        xo_ref, h_ref = rest
    tm = x_ref.shape[0]
    ng = 2 if tm % 32 == 0 else 1
    grp = [slice(q * (tm // ng), (q + 1) * (tm // ng)) for q in range(ng)]
    yas = [_dot(ya_ref[r, :], wpa_ref[...]) for r in grp]
    ybs = [_dot(yb_ref[r, :], wpb_ref[...]) for r in grp]
    ys = [_sigmoid(ga_ref[r, :].astype(F32)) * ya + _sigmoid(gb_ref[r, :].astype(F32)) * yb
          for r, ya, yb in zip(grp, yas, ybs)]
    xns = [x_ref[r, :] + _dot(y, wout_ref[...]) for r, y in zip(grp, ys)]
    hs = []
    for r, xn in zip(grp, xns):
        xo_ref[r, :] = xn
        h = xn * lax.rsqrt(jnp.mean(xn * xn, axis=-1, keepdims=True) + NORM_EPS) * g2_ref[...]
        h_ref[r, :] = h.astype(h_ref.dtype)
        hs.append(h)
    if with_router:
        rt = rt_ref[...]
        r_hi = rt.astype(BF16)
        r_lo = (rt - r_hi.astype(F32)).astype(BF16)
        neg = jnp.float32(-jnp.inf)
        for r, h in zip(grp, hs):
            h_hi = h.astype(BF16)
            h_lo = (h - h_hi.astype(F32)).astype(BF16)
            logits = (jnp.dot(h_hi, r_hi, preferred_element_type=F32)
                      + jnp.dot(h_lo, r_hi, preferred_element_type=F32)
                      + jnp.dot(h_hi, r_lo, preferred_element_type=F32))
            lane = _iota(logits.shape, 1)
            logits = jnp.where(lane < N_EXPERTS, logits, neg)
            m1 = jnp.max(logits, axis=1, keepdims=True)
            i1 = jnp.min(jnp.where(logits == m1, lane, LANES), axis=1, keepdims=True)
            l2 = jnp.where(lane == i1, neg, logits)
            m2 = jnp.max(l2, axis=1, keepdims=True)
            i2 = jnp.min(jnp.where(l2 == m2, lane, LANES), axis=1, keepdims=True)
            e2 = jnp.exp(m2 - m1)
            g1 = 1.0 / (1.0 + e2)
            g2 = e2 / (1.0 + e2)
            comb_ref[r, :] = (jnp.where(lane == 0, i1.astype(F32), 0.0)
                              + jnp.where(lane == 1, i2.astype(F32), 0.0)
                              + jnp.where(lane == 2, g1, 0.0) + jnp.where(lane == 3, g2, 0.0))


def _mixout(x2, ya, yb, p, wpa, wpb, wout, g2, router, tm):
    n = x2.shape[0]
    with_router = router is not None
    row = lambda w: pl.BlockSpec((tm, w), lambda i: (i, 0))
    full = lambda a: pl.BlockSpec(a.shape, lambda i: (0, 0))
    in_specs = [
        row(D_MODEL), row(RWKV_WIDTH), row(GDN_WIDTH),
        pl.BlockSpec((tm, D_MODEL), lambda i: (i, COL_GATE // D_MODEL)),
        pl.BlockSpec((tm, D_MODEL), lambda i: (i, COL_GATE // D_MODEL + 1)),
        full(wpa), full(wpb), full(wout), full(g2),
    ]
    args = [x2, ya, yb, p, p, wpa, wpb, wout, g2]
    out_specs = [row(D_MODEL), row(D_MODEL)]
    out_shape = [jax.ShapeDtypeStruct((n, D_MODEL), F32),
                 jax.ShapeDtypeStruct((n, D_MODEL), F32 if with_router else BF16)]
    if with_router:
        in_specs.append(full(router))
        args.append(router)
        out_specs.append(row(LANES))
        out_shape.append(jax.ShapeDtypeStruct((n, LANES), F32))
    return pl.pallas_call(
        functools.partial(_mixout_kernel, with_router=with_router),
        grid=(n // tm,),
        in_specs=in_specs,
        out_specs=out_specs,
        out_shape=out_shape,
        compiler_params=pltpu.CompilerParams(
            dimension_semantics=("parallel",), vmem_limit_bytes=VMEM_LIMIT),
        name="mixout_router" if with_router else "mixout",
    )(*args)


def _rms_out(y, fg_ref):
    return y * lax.rsqrt(jnp.mean(y * y, axis=-1, keepdims=True) + NORM_EPS) * fg_ref[...]


def _gate_up(wg, wu, tf):
    parts = []
    for j in range(D_FF // tf):
        parts += [wg[..., j * tf:(j + 1) * tf], wu[..., j * tf:(j + 1) * tf]]
    return jnp.concatenate(parts, axis=-1).astype(BF16)


def _swiglu_chunk(h, wgu, wd):
    tf = wd.shape[0]
    gu = jnp.dot(h, wgu, preferred_element_type=F32)
    act = _silu(gu[:, 0:tf]) * gu[:, tf:]
    return jnp.dot(act.astype(BF16), wd, preferred_element_type=F32)


def _ffn_kernel(h_ref, x_ref, wgu_ref, wd_ref, *rest, final):
    fg_ref = rest[0] if final else None
    o_ref, acc_ref = rest[-2:]
    j = pl.program_id(1)

    @pl.when(j == 0)
    def _():
        acc_ref[...] = jnp.zeros_like(acc_ref)

    acc_ref[...] += _swiglu_chunk(h_ref[...], wgu_ref[...], wd_ref[...])

    @pl.when(j == pl.num_programs(1) - 1)
    def _():
        y = x_ref[...] + acc_ref[...]
        o_ref[...] = _rms_out(y, fg_ref) if final else y


def _ffn(h, x2, wgu, wd, final_g, tm, tf):
    n = x2.shape[0]
    final = final_g is not None
    row = lambda w: pl.BlockSpec((tm, w), lambda i, j: (i, 0))
    in_specs = [
        row(D_MODEL), row(D_MODEL),
        pl.BlockSpec((D_MODEL, 2 * tf), lambda i, j: (0, j)),
        pl.BlockSpec((tf, D_MODEL), lambda i, j: (j, 0)),
    ]
    args = [h, x2, wgu, wd]
    if final:
        in_specs.append(pl.BlockSpec((1, D_MODEL), lambda i, j: (0, 0)))
        args.append(final_g)
    return pl.pallas_call(
        functools.partial(_ffn_kernel, final=final),
        grid=(n // tm, D_FF // tf),
        in_specs=in_specs,
        out_specs=row(D_MODEL),
        out_shape=jax.ShapeDtypeStruct((n, D_MODEL), F32),
        scratch_shapes=[pltpu.VMEM((tm, D_MODEL), F32)],
        compiler_params=pltpu.CompilerParams(
            dimension_semantics=("parallel", "arbitrary"), vmem_limit_bytes=VMEM_LIMIT),
        name="ffn",
    )(*args)


MOE_ROWS = 512


def _plan_kernel(sel_ref, pos_ref, tile_ref, cnt_scr, *, tm, n_tiles):
    ph = pl.program_id(0)
    i = pl.program_id(1)
    sel = sel_ref[...]
    lane = _iota((tm, LANES), 1)
    e1 = jnp.sum(jnp.where(lane == 0, sel, 0.0), axis=1, keepdims=True).astype(jnp.int32)
    e2 = jnp.sum(jnp.where(lane == 1, sel, 0.0), axis=1, keepdims=True).astype(jnp.int32)
    oh1 = lane == e1
    oh2 = lane == e2
    both = jnp.where(oh1, 1.0, 0.0) + jnp.where(oh2, 1.0, 0.0)

    @pl.when((ph == 0) & (i == 0))
    def _():
        cnt_scr[...] = jnp.zeros_like(cnt_scr)

    @pl.when(ph == 0)
    def _():
        cnt_scr[0:1, :] += jnp.sum(both, axis=0, keepdims=True)

    @pl.when((ph == 1) & (i == 0))
    def _():
        cnt = cnt_scr[0:1, :]
        padded = jnp.floor((cnt + (MOE_ROWS - 1)) * (1.0 / MOE_ROWS)) * MOE_ROWS
        rl = _iota((LANES, LANES), 0)
        cl = _iota((LANES, LANES), 1)
        upper = jnp.where(rl < cl, 1.0, 0.0)
        starts = _dot_hi(jnp.broadcast_to(padded, (8, LANES)), upper)[0:1, :]
        ends = starts + padded
        cnt_scr[0:1, :] = starts
        tl = _iota((n_tiles, LANES), 1)
        t0 = (_iota((n_tiles, LANES), 0) * MOE_ROWS).astype(F32)
        lane_ok = tl < N_EXPERTS
        owner = jnp.sum(jnp.where(lane_ok, jnp.where(t0 >= ends, 1.0, 0.0), 0.0),
                        axis=1, keepdims=True)
        used = jnp.sum(jnp.where(lane_ok, padded, 0.0), axis=1, keepdims=True)
        valid = t0[:, 0:1] < used
        last_e = jnp.max(jnp.where(lane_ok, jnp.where(padded > 0.0, tl.astype(F32), 0.0), 0.0),
                         axis=1, keepdims=True)
        ex = jnp.where(valid, jnp.minimum(owner, N_EXPERTS - 1.0), last_e)
        tile_ref[...] = (jnp.where(tl == 0, ex, 0.0)
                         + jnp.where(tl == 1, jnp.where(valid, 1.0, 0.0), 0.0)).astype(jnp.int32)

    @pl.when(ph == 1)
    def _():
        rr = _iota((tm, tm), 0)
        cc = _iota((tm, tm), 1)
        before = jnp.where(cc < rr, 1.0, 0.0).astype(BF16)
        base = cnt_scr[0:1, :] + jnp.dot(before, both.astype(BF16), preferred_element_type=F32)
        p1 = jnp.sum(jnp.where(oh1, base, 0.0), axis=1, keepdims=True)
        p2 = jnp.sum(jnp.where(oh2, base, 0.0), axis=1, keepdims=True)
        pos_ref[...] = (jnp.where(lane == 0, p1, 0.0) + jnp.where(lane == 1, p2, 0.0)
                        ).astype(jnp.int32)
        cnt_scr[0:1, :] += jnp.sum(both, axis=0, keepdims=True)


def _plan(sel, tm, n_tiles):
    n = sel.shape[0]
    return pl.pallas_call(
        functools.partial(_plan_kernel, tm=tm, n_tiles=n_tiles),
        grid=(2, n // tm),
        in_specs=[pl.BlockSpec((tm, LANES), lambda ph, i: (i, 0))],
        out_specs=[pl.BlockSpec((tm, LANES), lambda ph, i: (i * ph, 0)),
                   pl.BlockSpec((n_tiles, LANES), lambda ph, i: (0, 0))],
        out_shape=[jax.ShapeDtypeStruct((n, LANES), jnp.int32),
                   jax.ShapeDtypeStruct((n_tiles, LANES), jnp.int32)],
        scratch_shapes=[pltpu.VMEM((8, LANES), F32)],
        compiler_params=pltpu.CompilerParams(
            dimension_semantics=("arbitrary", "arbitrary"), vmem_limit_bytes=VMEM_LIMIT),
        name="moe_plan",
    )(sel)


def _row_copy(src_ref, src_row, dst_ref, dst_row, sem):
    return pltpu.make_async_copy(src_ref.at[pl.ds(src_row, 1)], dst_ref.at[pl.ds(dst_row, 1)], sem)


def _dispatch_kernel(pos_ref, h_ref, xs_in_ref, xs_ref, sem, *, tm):
    del xs_in_ref

    def start(r, c):
        _row_copy(h_ref, r, xs_ref, pos_ref[2 * r], sem).start()
        _row_copy(h_ref, r, xs_ref, pos_ref[2 * r + 1], sem).start()
        return c

    def wait(r, c):
        _row_copy(h_ref, 0, xs_ref, 0, sem).wait()
        _row_copy(h_ref, 0, xs_ref, 0, sem).wait()
        return c

    lax.fori_loop(0, tm, start, 0, unroll=8)
    lax.fori_loop(0, tm, wait, 0, unroll=8)


def _dispatch(pos_flat, h, n_rows, tm):
    n = h.shape[0]
    xs0 = jnp.zeros((n_rows, D_MODEL), F32)
    return pl.pallas_call(
        functools.partial(_dispatch_kernel, tm=tm),
        grid=(n // tm,),
        in_specs=[
            pl.BlockSpec((2 * tm,), lambda i: (i,), memory_space=pltpu.SMEM),
            pl.BlockSpec((tm, D_MODEL), lambda i: (i, 0)),
            pl.BlockSpec(memory_space=pl.ANY),
        ],
        out_specs=pl.BlockSpec(memory_space=pl.ANY),
        out_shape=jax.ShapeDtypeStruct((n_rows, D_MODEL), F32),
        scratch_shapes=[pltpu.SemaphoreType.DMA(())],
        input_output_aliases={2: 0},
        compiler_params=pltpu.CompilerParams(
            dimension_semantics=("arbitrary",), vmem_limit_bytes=VMEM_LIMIT),
        name="moe_dispatch",
    )(pos_flat, h, xs0)


def _experts_kernel(te_ref, tv_ref, xs_ref, wgu_ref, wd_ref, ys_ref, xb_scr, acc_scr):
    t = pl.program_id(0)
    j = pl.program_id(1)
    del te_ref

    @pl.when(j == 0)
    def _():
        xb_scr[...] = xs_ref[...].astype(BF16)
        acc_scr[...] = jnp.zeros_like(acc_scr)

    @pl.when(tv_ref[t] == 1)
    def _():
        acc_scr[...] += _swiglu_chunk(xb_scr[...], wgu_ref[0], wd_ref[0])

    @pl.when(j == pl.num_programs(1) - 1)
    def _():
        ys_ref[...] = acc_scr[...]


def _experts(tile_e, tile_v, xs, wgu, wd, tf):
    n_rows = xs.shape[0]
    grid_spec = pltpu.PrefetchScalarGridSpec(
        num_scalar_prefetch=2,
        grid=(n_rows // MOE_ROWS, D_FF // tf),
        in_specs=[
            pl.BlockSpec((MOE_ROWS, D_MODEL), lambda t, j, te, tv: (t, 0)),
            pl.BlockSpec((1, D_MODEL, 2 * tf), lambda t, j, te, tv: (te[t], 0, j)),
            pl.BlockSpec((1, tf, D_MODEL), lambda t, j, te, tv: (te[t], j, 0)),
        ],
        out_specs=pl.BlockSpec((MOE_ROWS, D_MODEL), lambda t, j, te, tv: (t, 0)),
        scratch_shapes=[pltpu.VMEM((MOE_ROWS, D_MODEL), BF16), pltpu.VMEM((MOE_ROWS, D_MODEL), F32)],
    )
    return pl.pallas_call(
        _experts_kernel,
        grid_spec=grid_spec,
        out_shape=jax.ShapeDtypeStruct((n_rows, D_MODEL), F32),
        compiler_params=pltpu.CompilerParams(
            dimension_semantics=("parallel", "arbitrary"), vmem_limit_bytes=VMEM_LIMIT),
        name="moe_experts",
    )(tile_e, tile_v, xs, wgu, wd)


def _combine_kernel(pos_ref, x_ref, sel_ref, *rest, tm, final):
    fg_ref = rest[0] if final else None
    ys_ref, o_ref, buf, sem = rest[-4:]

    def start(r, c):
        _row_copy(ys_ref, pos_ref[2 * r], buf.at[0], r, sem).start()
        _row_copy(ys_ref, pos_ref[2 * r + 1], buf.at[1], r, sem).start()
        return c

    def wait(r, c):
        _row_copy(ys_ref, 0, buf.at[0], 0, sem).wait()
        _row_copy(ys_ref, 0, buf.at[1], 0, sem).wait()
        return c

    lax.fori_loop(0, tm, start, 0, unroll=8)
    lax.fori_loop(0, tm, wait, 0, unroll=8)
    sel = sel_ref[...]
    lane = _iota(sel.shape, 1)
    g1 = jnp.sum(jnp.where(lane == 2, sel, 0.0), axis=1, keepdims=True)
    g2 = jnp.sum(jnp.where(lane == 3, sel, 0.0), axis=1, keepdims=True)
    y = x_ref[...] + g1 * buf[0] + g2 * buf[1]
    o_ref[...] = _rms_out(y, fg_ref) if final else y


def _combine(pos_flat, x2, sel, ys, final_g, tm):
    n = x2.shape[0]
    final = final_g is not None
    in_specs = [
        pl.BlockSpec((2 * tm,), lambda i: (i,), memory_space=pltpu.SMEM),
        pl.BlockSpec((tm, D_MODEL), lambda i: (i, 0)),
        pl.BlockSpec((tm, LANES), lambda i: (i, 0)),
    ]
    args = [pos_flat, x2, sel]
    if final:
        in_specs.append(pl.BlockSpec((1, D_MODEL), lambda i: (0, 0)))
        args.append(final_g)
    in_specs.append(pl.BlockSpec(memory_space=pl.ANY))
    args.append(ys)
    return pl.pallas_call(
        functools.partial(_combine_kernel, tm=tm, final=final),
        grid=(n // tm,),
        in_specs=in_specs,
        out_specs=pl.BlockSpec((tm, D_MODEL), lambda i: (i, 0)),
        out_shape=jax.ShapeDtypeStruct((n, D_MODEL), F32),
        scratch_shapes=[pltpu.VMEM((2, tm, D_MODEL), F32), pltpu.SemaphoreType.DMA(())],
        compiler_params=pltpu.CompilerParams(
            dimension_semantics=("arbitrary",), vmem_limit_bytes=VMEM_LIMIT),
        name="moe_combine",
    )(*args)


def _moe(h, x2, sel, wgu, wd, final_g, tm, tf):
    n = x2.shape[0]
    n_tiles = pl.cdiv(2 * n, MOE_ROWS) + N_EXPERTS
    pos, tiles = _plan(sel, tm, n_tiles)
    pos_flat = pos[:, 0:2].reshape(2 * n)
    xs = _dispatch(pos_flat, h, n_tiles * MOE_ROWS, tm)
    ys = _experts(tiles[:, 0], tiles[:, 1], xs, wgu, wd, tf)
    return _combine(pos_flat, x2, sel, ys, final_g, tm)


def _tile(n, pref):
    t = min(pref, n)
    while n % t:
        t //= 2
    return t


def _pad_lanes(a, width=LANES):
    return jnp.pad(a, ((0, 0), (0, width - a.shape[-1])))


def kernel(x, norm1_g, w_in, tshift_mu, w0, w_decay_up, a0, w_iclr_up, w_gate_up, k_k, k_a, r_k,
           lnx_g, lnx_b, w_pa, conv_w, a_log, dt_bias, gdn_norm_g, w_pb, w_out, norm2_g,
           ffn_w_gate, ffn_w_up, ffn_w_down, moe_router, moe_w_gate, moe_w_up, moe_w_down, final_g):
    nb, seq, d = x.shape
    n = nb * seq
    x2 = x.reshape(n, d)
    tm_in = _tile(n, 1024)
    tm = _tile(n, 512)
    rw = 1792
    gq = 3 * GDN_WIDTH
    for layer in range(DEPTH):
        wi = w_in[layer]
        ab = wi[:, rw + gq:rw + gq + 2 * GDN_HEADS]
        zc = wi[:, rw + gq + 2 * GDN_HEADS:rw + gq + 2 * GDN_HEADS + GDN_WIDTH]
        gate = wi[:, rw + gq + 2 * GDN_HEADS + GDN_WIDTH:]
        w_cat = jnp.concatenate(
            [wi[:, :rw + gq], zc, ab, jnp.zeros((d, COL_GATE - COL_AB - 2 * GDN_HEADS), F32), gate],
            axis=1).astype(BF16)
        p = _inproj(x2, norm1_g[layer][None, :], w_cat, tm_in, 1536)
        one = lambda a: a[layer][None, :]
        ya = _rwkv(p, seq, one(tshift_mu), one(w0), one(a0), one(k_k), one(k_a), one(r_k),
                   one(lnx_g), one(lnx_b), w_decay_up[layer].astype(BF16),
                   w_iclr_up[layer].astype(BF16), w_gate_up[layer].astype(BF16))
        yb = _gdn(p, seq, conv_w[layer], _pad_lanes(one(a_log)), _pad_lanes(one(dt_bias)),
                  one(gdn_norm_g))
        j = layer // 2
        routed = layer % 2 == 1
        router = _pad_lanes(moe_router[j]) if routed else None
        res = _mixout(x2, ya, yb, p, w_pa[layer].astype(BF16), w_pb[layer].astype(BF16),
                      w_out[layer].astype(BF16), one(norm2_g), router, tm)
        fg = final_g[None, :] if layer == DEPTH - 1 else None
        if routed:
            x2, h, sel = res
            x2 = _moe(h, x2, sel, _gate_up(moe_w_gate[j], moe_w_up[j], FF_CHUNK),
                      moe_w_down[j].astype(BF16), fg, tm, FF_CHUNK)
        else:
            x2, h = res
            x2 = _ffn(h, x2, _gate_up(ffn_w_gate[j], ffn_w_up[j], FF_CHUNK),
                      ffn_w_down[j].astype(BF16), fg, tm, FF_CHUNK)
    return x2.reshape(nb, seq, d)
```

```python
import functools

import jax
import jax.numpy as jnp
from jax import lax
from jax.experimental import pallas as pl
from jax.experimental.pallas import tpu as pltpu

F32 = jnp.float32
BF16 = jnp.bfloat16
HI = lax.Precision.HIGHEST

D_MODEL = 1024
DEPTH = 2
RWKV_HEADS = 8
RWKV_HEAD_DIM = 64
RWKV_WIDTH = 512
DECAY_LORA = 64
ICLR_LORA = 64
GATE_LORA = 128
RWKV_GN_EPS = 64e-5
GDN_HEADS = 4
GDN_HEAD_DIM = 128
GDN_WIDTH = 512
CONV_WIDTH = 4
CHUNK = 64
D_FF = 2816
N_EXPERTS = 8
NORM_EPS = 1e-6
L2_EPS = 1e-6

LANES = 128
P_COLS = 6144
COL_AB = 1792
COL_QKV = 2048
COL_Z = 3584
COL_GATE = 4096
VMEM_LIMIT = 56 * 1024 * 1024
FF_CHUNK = D_FF // 2


def _dot(a, b):
    return jnp.dot(a.astype(BF16), b.astype(BF16), preferred_element_type=F32)


def _dot_nt(a, b):
    return lax.dot_general(a.astype(BF16), b.astype(BF16), (((1,), (1,)), ((), ())),
                           preferred_element_type=F32)


def _dot_tn(a, b):
    return lax.dot_general(a.astype(BF16), b.astype(BF16), (((0,), (0,)), ((), ())),
                           preferred_element_type=F32)


def _dot_hi(a, b):
    return jnp.dot(a, b, preferred_element_type=F32, precision=HI)


def _softplus(x):
    return jnp.maximum(x, 0.0) + jnp.log(1.0 + jnp.exp(-jnp.abs(x)))


def _sigmoid(x):
    return 1.0 / (1.0 + jnp.exp(-x))


def _silu(x):
    return x * _sigmoid(x)


def _iota(shape, dim):
    return lax.broadcasted_iota(jnp.int32, shape, dim)


def _inproj_kernel(x_ref, g_ref, w_ref, p_ref, h_scr):
    @pl.when(pl.program_id(1) == 0)
    def _():
        x = x_ref[...]
        h = x * lax.rsqrt(jnp.mean(x * x, axis=-1, keepdims=True) + NORM_EPS) * g_ref[...]
        h_scr[...] = h.astype(BF16)

    p_ref[...] = jnp.dot(h_scr[...], w_ref[...], preferred_element_type=F32).astype(p_ref.dtype)


def _inproj(x2, g, w_cat, tm, tn):
    n = x2.shape[0]
    return pl.pallas_call(
        _inproj_kernel,
        grid=(n // tm, P_COLS // tn),
        in_specs=[
            pl.BlockSpec((tm, D_MODEL), lambda i, j: (i, 0)),
            pl.BlockSpec((1, D_MODEL), lambda i, j: (0, 0)),
            pl.BlockSpec((D_MODEL, tn), lambda i, j: (0, j)),
        ],
        out_specs=pl.BlockSpec((tm, tn), lambda i, j: (i, j)),
        out_shape=jax.ShapeDtypeStruct((n, P_COLS), BF16),
        scratch_shapes=[pltpu.VMEM((tm, D_MODEL), BF16)],
        compiler_params=pltpu.CompilerParams(
            dimension_semantics=("parallel", "arbitrary"), vmem_limit_bytes=VMEM_LIMIT),
        name="inproj",
    )(x2, g, w_cat)


def _split_dot(x, m):
    hi = x.astype(BF16)
    lo = (x - hi.astype(F32)).astype(BF16)
    return (jnp.dot(hi, m, preferred_element_type=F32) + jnp.dot(lo, m, preferred_element_type=F32))


def _chunk_cumsum(x, tin):
    y = x
    sh = 1
    while sh < CHUNK:
        y = y + jnp.where(tin >= sh, pltpu.roll(y, sh, axis=0), 0.0)
        sh *= 2
    return y


def _level_masks(rm, lm):
    out = []
    for k in range(6):
        x = rm >> k
        y = lm >> k
        out.append((((x ^ y) + 2 * (1 - (x & 1))) == 1).astype(F32))
    return out


def _tri_inverse(mats, eye, masks, between=lambda: None):
    ts = [eye + a * masks[0] for a in mats]
    for k in range(1, 6):
        xs = [_dot(a * masks[k], t) for a, t in zip(mats, ts)]
        between()
        ts = [t + _dot(t, x) for t, x in zip(ts, xs)]
        between()
    return ts


def _rwkv_kernel(r_ref, k_ref, v_ref, lo_ref, mur_ref, muk_ref, muv_ref, mulo_ref,
                 w0_ref, a0_ref, kk_ref, ka_ref, rk_ref, lg_ref, lb_ref,
                 wd_ref, wa_ref, wg_ref, bd_ref, o_ref,
                 lhs_scr, u0_scr, qb_scr, ov_scr, c_scr, pre_scr, *, nb, ns):
    L = CHUNK
    W = LANES
    R = nb * L
    seq = r_ref.shape[0]
    nsb = seq // R

    row_r = _iota((R, W), 0)
    row_r2 = _iota((R, 2 * W), 0)
    row_c = _iota((L, W), 0)
    lane = _iota((L, W), 1)
    h1 = lane < 64
    rm = _iota((W, W), 0)
    lm = _iota((W, W), 1)
    tri_mask = (lm & 63) < (rm & 63) + (rm >> 6)
    blk_mask = (rm >> 6) == (lm >> 6)
    eye = (rm == lm).astype(F32)
    masks = _level_masks(rm, lm)

    mur, muk, muv, mulo = mur_ref[...], muk_ref[...], muv_ref[...], mulo_ref[...]
    w0, a0, k_k, k_a, r_k = w0_ref[...], a0_ref[...], kk_ref[...], ka_ref[...], rk_ref[...]
    ln_g, ln_b = lg_ref[...], lb_ref[...]
    wd, wa, wg = wd_ref[...], wa_ref[...], wg_ref[...]

    def shift(x, prev_row, mu, rowi):
        xp = pltpu.roll(x, 1, axis=0)
        xp = jnp.where(rowi == 0, prev_row, xp)
        return x + (xp - x) * mu

    def last_row_before(ref, s, cols):
        rows = ref[pl.ds(pl.multiple_of(jnp.maximum(s - 16, 0), 16), 16), cols].astype(F32)
        return rows[15:16, :] * (s > 0).astype(F32)

    keys = ("v", "k2", "bh", "cl", "g", "bonus", "rt", "at", "bt", "kt")

    class Stored:
        def __init__(self, par, j):
            self.par, self.j = par, j

        def __getitem__(self, key):
            par, j, q = self.par, self.j, keys.index(key)

            class Rows:
                def __getitem__(self, rows):
                    return pre_scr[par, j, q, rows]
            return Rows()

    def prologue_pieces(sb, par):
        sh = {}

        def shared():
            s = pl.multiple_of(sb * R, R)
            lo_raw = lo_ref[pl.ds(s, R), :].astype(F32)
            lo = shift(lo_raw, last_row_before(lo_ref, s, slice(None)), mulo, row_r2)
            wl = lo[:, 0:DECAY_LORA]
            al = lo[:, DECAY_LORA:DECAY_LORA + ICLR_LORA]
            gl = lo[:, DECAY_LORA + ICLR_LORA:]
            sh["z"] = w0 + _dot(jnp.tanh(wl), wd)
            sh["a"] = _sigmoid(a0 + _dot(al, wa))
            sh["g"] = _dot(_sigmoid(gl), wg)
            for j in range(ns):
                sj = slice(j * W, (j + 1) * W)
                for name, ref, mu in (("r", r_ref, mur), ("k", k_ref, muk), ("v", v_ref, muv)):
                    sh[name, j] = shift(ref[pl.ds(s, R), sj].astype(F32), last_row_before(ref, s, sj),
                                        mu[:, sj], row_r)

        def piece(j, c):
            sj = slice(j * W, (j + 1) * W)
            r, k, v = sh["r", j][c], sh["k", j][c], sh["v", j][c]
            w_log = -_softplus(-sh["z"][c, sj]) - 0.5
            lw = -jnp.exp(w_log)
            a = sh["a"][c, sj]
            kk = k * k_k[:, sj]
            kk = kk * lax.rsqrt(_split_dot(kk * kk, bd_ref[...]) + L2_EPS)
            k2 = k * (1.0 + (a - 1.0) * k_a[:, sj])
            bh = kk * a
            cl = _chunk_cumsum(lw, row_c)
            e_neg = jnp.exp(-cl)
            vals = dict(
                v=v, k2=k2, bh=bh, cl=cl, g=sh["g"][c, sj],
                bonus=_split_dot(r * k2 * r_k[:, sj], bd_ref[...]) * v,
                rt=r * jnp.exp(cl), at=-kk * jnp.exp(cl - lw), bt=bh * e_neg, kt=k2 * e_neg)
            for q, key in enumerate(keys):
                pre_scr[par, j, q, c] = vals[key]

        return [shared] + [functools.partial(piece, j, slice(i * L, (i + 1) * L))
                           for i in range(nb) for j in range(ns)]

    def run(par, Hs, sb, fillers):
        pre = [Stored(par, j) for j in range(ns)]

        def fill():
            if fillers:
                fillers.pop(0)()

        s = pl.multiple_of(sb * R, R)
        items = [(j, slice(i * L, (i + 1) * L)) for i in range(nb) for j in range(ns)]
        v_of = lambda it: pre[it[0]]["v"][it[1]]
        rts = [pre[j]["rt"][c] for j, c in items]
        ats = [pre[j]["at"][c] for j, c in items]
        zero = jnp.zeros((L, W), F32)
        m1s, m2s = [], []
        for (j, c), rt, at in zip(items, rts, ats):
            bt, kt = pre[j]["bt"][c], pre[j]["kt"][c]
            lhs1 = jnp.concatenate([jnp.where(h1, at, zero), jnp.where(h1, rt, zero)], axis=0)
            lhs2 = jnp.concatenate([jnp.where(h1, zero, at), jnp.where(h1, zero, rt)], axis=0)
            m1s.append(jnp.where(tri_mask, _dot_nt(lhs1, jnp.concatenate([bt, kt], axis=0)), 0.0))
            m2s.append(jnp.where(tri_mask, _dot_nt(lhs2, jnp.concatenate([kt, bt], axis=0)), 0.0))
        fill()
        a_bds = [jnp.concatenate([jnp.where(h1, m1[0:L], 0.0), jnp.where(h1, 0.0, m2[0:L])], axis=0)
                 for m1, m2 in zip(m1s, m2s)]
        tinvs = _tri_inverse(a_bds, eye, masks)
        avs, o_vs = [], []
        for it, m1, m2 in zip(items, m1s, m2s):
            vc = v_of(it)
            ak_m = jnp.concatenate(
                [jnp.where(h1, 0.0, m1[0:L]), jnp.where(h1, m2[0:L], 0.0)], axis=0)
            avs.append(_dot(ak_m, jnp.concatenate([vc, vc], axis=0)))
            qk = jnp.where(h1, m2[L:], m1[L:])
            vx = jnp.concatenate([jnp.where(h1, 0.0, vc), jnp.where(h1, vc, 0.0)], axis=0)
            o_vs.append(_dot(qk, vx))
        qbs = [jnp.where(h1, m1[L:], m2[L:]) for m1, m2 in zip(m1s, m2s)]
        u0s = [_dot(t, av) for t, av in zip(tinvs, avs)]
        tas = [jnp.where(blk_mask, _dot(t, jnp.concatenate([at, at], axis=0)), 0.0)
               for t, at in zip(tinvs, ats)]
        for i, ((j, c), ta, u0) in enumerate(zip(items, tas, u0s)):
            cl_i = pre[j]["cl"][c]
            cl_last = cl_i[L - 1:L, :]
            e_end = jnp.exp(cl_last - cl_i)
            b_end = pre[j]["bh"][c] * e_end
            b_st = jnp.concatenate([jnp.where(h1, b_end, 0.0), jnp.where(h1, 0.0, b_end)], axis=0)
            u0_bd = jnp.where(blk_mask, u0, 0.0)
            m_t = eye * jnp.exp(cl_last) + _dot_tn(b_st, ta)
            lhs_scr[i] = jnp.concatenate([m_t, ta, rts[i]], axis=0).astype(BF16)
            u0_scr[i] = u0_bd
            qb_scr[i] = qbs[i].astype(BF16)
            ov_scr[i] = o_vs[i]
            c_scr[i] = (_dot_tn(b_st, u0_bd)
                        + jnp.where(blk_mask, _dot_tn(pre[j]["k2"][c] * e_end, v_of((j, c))), 0.0))

        Hs = list(Hs)
        bigs = []
        for i, (j, c) in enumerate(items):
            big = jnp.dot(lhs_scr[i], Hs[j].astype(BF16), preferred_element_type=F32)
            Hs[j] = big[0:W] + c_scr[i]
            bigs.append(big)
            fill()
        outs = [[] for _ in range(ns)]
        for i, (j, c) in enumerate(items):
            u_bd = jnp.where(blk_mask, bigs[i][W:2 * W], 0.0) + u0_scr[i]
            outs[j].append(bigs[i][2 * W:] + _dot(qb_scr[i], u_bd) + ov_scr[i])

        for j in range(ns):
            sj = slice(j * W, (j + 1) * W)
            o = jnp.concatenate(outs[j], axis=0)
            mean = _split_dot(o, bd_ref[...]) * (1.0 / 64.0)
            oc = o - mean
            var = _split_dot(oc * oc, bd_ref[...]) * (1.0 / 64.0)
            y = (oc * lax.rsqrt(var + RWKV_GN_EPS) * ln_g[:, sj] + ln_b[:, sj]
                 + pre[j]["bonus"][slice(0, R)])
            o_ref[pl.ds(s, R), sj] = (y * pre[j]["g"][slice(0, R)]).astype(o_ref.dtype)
        while fillers:
            fill()
        return tuple(Hs)

    for thunk in prologue_pieces(0, 0):
        thunk()

    def body(t, Hs):
        Hs = run(0, Hs, 2 * t, prologue_pieces(2 * t + 1, 1))
        return run(1, Hs, 2 * t + 1, prologue_pieces(jnp.minimum(2 * t + 2, nsb - 1), 0))

    lax.fori_loop(0, nsb // 2, body, tuple(jnp.zeros((W, W), F32) for _ in range(ns)))


def _rwkv(p, seq, mu, w0, a0, k_k, k_a, r_k, ln_g, ln_b, wd, wa, wg):
    n = p.shape[0]
    nbatch = n // seq
    ns = 4
    wide = ns * LANES
    ngrp = RWKV_WIDTH // wide
    nb = 4 if seq % (8 * CHUNK) == 0 else 1

    def col(off):
        return pl.BlockSpec((seq, wide), lambda b, q, off=off: (b, off + q))

    def par(off=0):
        return pl.BlockSpec((1, wide), lambda b, q, off=off: (0, off + q))

    in_specs = [
        col(0), col(ngrp), col(2 * ngrp),
        pl.BlockSpec((seq, 2 * LANES), lambda b, q: (b, 3 * RWKV_WIDTH // (2 * LANES))),
        par(0), par(ngrp), par(2 * ngrp),
        pl.BlockSpec((1, 2 * LANES), lambda b, q: (0, 3 * RWKV_WIDTH // (2 * LANES))),
        par(), par(), par(), par(), par(), par(), par(),
        pl.BlockSpec((DECAY_LORA, wide), lambda b, q: (0, q)),
        pl.BlockSpec((ICLR_LORA, wide), lambda b, q: (0, q)),
        pl.BlockSpec((GATE_LORA, wide), lambda b, q: (0, q)),
        pl.BlockSpec((LANES, LANES), lambda b, q: (0, 0)),
    ]
    head_id = jnp.arange(LANES) // RWKV_HEAD_DIM
    bd_ones = (head_id[:, None] == head_id[None, :]).astype(BF16)
    return pl.pallas_call(
        functools.partial(_rwkv_kernel, nb=nb, ns=ns),
        grid=(nbatch, ngrp),
        in_specs=in_specs,
        out_specs=pl.BlockSpec((seq, wide), lambda b, q: (b, q)),
        out_shape=jax.ShapeDtypeStruct((n, RWKV_WIDTH), BF16),
        scratch_shapes=[
            pltpu.VMEM((nb * ns, 2 * LANES + CHUNK, LANES), BF16),
            pltpu.VMEM((nb * ns, LANES, LANES), F32),
            pltpu.VMEM((nb * ns, CHUNK, LANES), BF16),
            pltpu.VMEM((nb * ns, CHUNK, LANES), F32),
            pltpu.VMEM((nb * ns, LANES, LANES), F32),
            pltpu.VMEM((2, ns, 10, nb * CHUNK, LANES), F32),
        ],
        compiler_params=pltpu.CompilerParams(
            dimension_semantics=("parallel", "parallel"), vmem_limit_bytes=VMEM_LIMIT),
        name="rwkv",
    )(p, p, p, p, mu, mu, mu, mu, w0, a0, k_k, k_a, r_k, ln_g, ln_b, wd, wa, wg, bd_ones)


def _gdn_kernel(q_ref, k_ref, v_ref, z_ref, ab_ref, cwq_ref, cwk_ref, cwv_ref,
                alog_ref, dtb_ref, ng_ref, o_ref, lhs_scr, u_scr, qk_scr, c_scr, *, nb, ns):
    L = CHUNK
    W = LANES
    P = 2 * L
    R = nb * P
    seq = q_ref.shape[0]
    nsb = seq // R
    hd0 = pl.program_id(1) * ns

    row_r = _iota((R, W), 0)
    lane_r = _iota((R, W), 1)
    tin = row_r & (L - 1)
    row8 = _iota((8, W), 0)
    rm = _iota((W, W), 0)
    lm = _iota((W, W), 1)
    dif = rm - lm
    tin_m = rm & (L - 1)
    eye = (rm == lm).astype(F32)
    masks = _level_masks(rm, lm)

    lane1 = _iota((1, W), 1)
    pick = lambda ref, j: jnp.sum(jnp.where(lane1 == hd0 + j, ref[...], 0.0), axis=1, keepdims=True)
    neg_a = [-jnp.exp(pick(alog_ref, j)) for j in range(ns)]
    dt_b = [pick(dtb_ref, j) for j in range(ns)]
    cwq, cwk, cwv = cwq_ref[...], cwk_ref[...], cwv_ref[...]
    ng = ng_ref[...]
    scale = GDN_HEAD_DIM ** -0.5
    neg_inf = jnp.float32(-jnp.inf)

    def conv(x, prev8, cw):
        acc = x * cw[CONV_WIDTH - 1:CONV_WIDTH, :]
        for j in range(1, CONV_WIDTH):
            xs = pltpu.roll(x, j, axis=0)
            ps = pltpu.roll(prev8, j, axis=0)
            head = jnp.where(row8 < j, ps, xs[0:8])
            xs = jnp.concatenate([head, xs[8:]], axis=0)
            acc = acc + xs * cw[CONV_WIDTH - 1 - j:CONV_WIDTH - j, :]
        return _silu(acc)

    def l2n(x):
        return x * lax.rsqrt(jnp.sum(x * x, axis=-1, keepdims=True) + L2_EPS)

    def body(sb, carry):
        Ss, pqs, pks, pvs = carry
        s = pl.multiple_of(sb * R, R)
        ab = ab_ref[pl.ds(s, R), :].astype(F32)

        pre, raws = [], []
        for j in range(ns):
            sj = slice(j * W, (j + 1) * W)
            hd = hd0 + j
            q_raw = q_ref[pl.ds(s, R), sj].astype(F32)
            k_raw = k_ref[pl.ds(s, R), sj].astype(F32)
            v_raw = v_ref[pl.ds(s, R), sj].astype(F32)
            raws.append((q_raw[R - 8:, :], k_raw[R - 8:, :], v_raw[R - 8:, :]))
            q = l2n(conv(q_raw, pqs[j], cwq[:, sj])) * scale
            k = l2n(conv(k_raw, pks[j], cwk[:, sj]))
            v = conv(v_raw, pvs[j], cwv[:, sj])
            a_in = jnp.sum(jnp.where(lane_r == hd, ab, 0.0), axis=1, keepdims=True)
            b_in = jnp.sum(jnp.where(lane_r == hd + GDN_HEADS, ab, 0.0), axis=1, keepdims=True)
            beta = _sigmoid(b_in)
            g_log = neg_a[j] * _softplus(a_in + dt_b[j])
            gc = _chunk_cumsum(jnp.broadcast_to(g_log, (R, W)), tin)
            e_gc = jnp.exp(gc)
            kb = k * beta
            pre.append(dict(q=q, k=k, gc=gc, kb=kb, vb=v * beta, kbe=kb * e_gc, qe=q * e_gc))

        items = [(j, slice(i * P, (i + 1) * P)) for i in range(nb) for j in range(ns)]
        decs = []
        for j, sl in items:
            gc_p = pre[j]["gc"][sl]
            dlog = gc_p - jnp.transpose(gc_p)
            dlog = jnp.where(dif >= 0, dlog, neg_inf)
            decs.append(jnp.exp(jnp.where(dif <= tin_m, dlog, neg_inf)))
        negAs = [jnp.where(dif > 0, _dot_nt(pre[j]["kb"][sl], pre[j]["k"][sl]) * dec, 0.0) * -1.0
                 for (j, sl), dec in zip(items, decs)]
        qks = [_dot_nt(pre[j]["q"][sl], pre[j]["k"][sl]) * dec for (j, sl), dec in zip(items, decs)]
        tinvs = _tri_inverse(negAs, eye, masks)
        uws = [_dot(t, jnp.concatenate([pre[j]["vb"][sl], pre[j]["kbe"][sl]], axis=1))
               for t, (j, sl) in zip(tinvs, items)]
        steps = []
        for i in range(nb):
            for c in range(2):
                for j in range(ns):
                    it = i * ns + j
                    sl, uw, qk = items[it][1], uws[it], qks[it]
                    cs = slice(sl.start + c * L, sl.start + (c + 1) * L)
                    hs = slice(c * L, (c + 1) * L)
                    gc_c = pre[j]["gc"][cs]
                    gc_last = gc_c[L - 1:L, :]
                    k_dec = pre[j]["k"][cs] * jnp.exp(gc_last - gc_c)
                    u_c, w_c = uw[hs, 0:W], uw[hs, W:]
                    m_c = eye * jnp.exp(gc_last) - _dot_tn(k_dec, w_c)
                    n_st = len(steps)
                    lhs_scr[n_st] = jnp.concatenate(
                        [m_c, w_c, pre[j]["qe"][cs]], axis=0).astype(BF16)
                    u_scr[n_st] = u_c
                    qk_scr[n_st] = qk[hs].astype(BF16)
                    c_scr[n_st] = _dot_tn(k_dec, u_c)
                    steps.append(j)

        Ss = list(Ss)
        bigs = []
        for i, j in enumerate(steps):
            big = jnp.dot(lhs_scr[i], Ss[j].astype(BF16), preferred_element_type=F32)
            Ss[j] = big[0:W] + c_scr[i]
            bigs.append(big)
        outs = [[] for _ in range(ns)]
        for i, j in enumerate(steps):
            v_new = u_scr[i] - bigs[i][W:W + L]
            outs[j].append(bigs[i][W + L:]
                           + _dot(qk_scr[i], jnp.concatenate([v_new, v_new], axis=0)))

        for j in range(ns):
            sj = slice(j * W, (j + 1) * W)
            o = jnp.concatenate(outs[j], axis=0)
            o = o * lax.rsqrt(jnp.mean(o * o, axis=-1, keepdims=True) + NORM_EPS) * ng
            o_ref[pl.ds(s, R), sj] = (o * _silu(z_ref[pl.ds(s, R), sj].astype(F32))
                                      ).astype(o_ref.dtype)
        return (tuple(Ss), tuple(t[0] for t in raws), tuple(t[1] for t in raws),
                tuple(t[2] for t in raws))

    z8 = jnp.zeros((8, W), F32)
    init = (tuple(jnp.zeros((W, W), F32) for _ in range(ns)), (z8,) * ns, (z8,) * ns, (z8,) * ns)
    lax.fori_loop(0, nsb, body, init, unroll=2)


def _gdn(p, seq, conv_w, a_log, dt_bias, norm_g):
    n = p.shape[0]
    nbatch = n // seq
    ns = 4
    wide = ns * LANES
    ngrp = GDN_HEADS // ns
    assert COL_QKV % wide == 0 and COL_Z % wide == 0 and GDN_WIDTH % wide == 0
    qoff = COL_QKV // wide
    nb = 2 if seq % (4 * CHUNK) == 0 else 1

    def col(off):
        return pl.BlockSpec((seq, wide), lambda b, h, off=off: (b, off + h))

    def cw(off):
        return pl.BlockSpec((CONV_WIDTH, wide), lambda b, h, off=off: (0, off + h))

    one = pl.BlockSpec((1, LANES), lambda b, h: (0, 0))
    in_specs = [
        col(qoff), col(qoff + ngrp), col(qoff + 2 * ngrp), col(COL_Z // wide),
        pl.BlockSpec((seq, LANES), lambda b, h: (b, COL_AB // LANES)),
        cw(0), cw(ngrp), cw(2 * ngrp), one, one, one,
    ]
    n_steps = 2 * nb * ns
    return pl.pallas_call(
        functools.partial(_gdn_kernel, nb=nb, ns=ns),
        grid=(nbatch, ngrp),
        in_specs=in_specs,
        out_specs=pl.BlockSpec((seq, wide), lambda b, h: (b, h)),
        out_shape=jax.ShapeDtypeStruct((n, GDN_WIDTH), BF16),
        scratch_shapes=[
            pltpu.VMEM((n_steps, 2 * CHUNK + LANES, LANES), BF16),
            pltpu.VMEM((n_steps, CHUNK, LANES), F32),
            pltpu.VMEM((n_steps, CHUNK, LANES), BF16),
            pltpu.VMEM((n_steps, LANES, LANES), F32),
        ],
        compiler_params=pltpu.CompilerParams(
            dimension_semantics=("parallel", "parallel"), vmem_limit_bytes=VMEM_LIMIT),
        name="gdn",
    )(p, p, p, p, p, conv_w, conv_w, conv_w, a_log, dt_bias, norm_g)


def _mixout_kernel(x_ref, ya_ref, yb_ref, ga_ref, gb_ref, wpa_ref, wpb_ref, wout_ref, g2_ref,
                   *rest, with_router):
    if with_router:
        rt_ref, xo_ref, h_ref, comb_ref = rest
    else:
        xo_ref, h_ref = rest
    tm = x_ref.shape[0]
    ng = 2 if tm % 32 == 0 else 1
    grp = [slice(q * (tm // ng), (q + 1) * (tm // ng)) for q in range(ng)]
    yas = [_dot(ya_ref[r, :], wpa_ref[...]) for r in grp]
    ybs = [_dot(yb_ref[r, :], wpb_ref[...]) for r in grp]
    ys = [_sigmoid(ga_ref[r, :].astype(F32)) * ya + _sigmoid(gb_ref[r, :].astype(F32)) * yb
          for r, ya, yb in zip(grp, yas, ybs)]
    xns = [x_ref[r, :] + _dot(y, wout_ref[...]) for r, y in zip(grp, ys)]
    hs = []
    for r, xn in zip(grp, xns):
        xo_ref[r, :] = xn
        h = xn * lax.rsqrt(jnp.mean(xn * xn, axis=-1, keepdims=True) + NORM_EPS) * g2_ref[...]
        h_ref[r, :] = h.astype(h_ref.dtype)
        hs.append(h)
    if with_router:
        rt = rt_ref[...]
        r_hi = rt.astype(BF16)
        r_lo = (rt - r_hi.astype(F32)).astype(BF16)
        neg = jnp.float32(-jnp.inf)
        for r, h in zip(grp, hs):
            h_hi = h.astype(BF16)
            h_lo = (h - h_hi.astype(F32)).astype(BF16)
            logits = (jnp.dot(h_hi, r_hi, preferred_element_type=F32)
                      + jnp.dot(h_lo, r_hi, preferred_element_type=F32)
                      + jnp.dot(h_hi, r_lo, preferred_element_type=F32))
            lane = _iota(logits.shape, 1)
            logits = jnp.where(lane < N_EXPERTS, logits, neg)
            m1 = jnp.max(logits, axis=1, keepdims=True)
            i1 = jnp.min(jnp.where(logits == m1, lane, LANES), axis=1, keepdims=True)
            l2 = jnp.where(lane == i1, neg, logits)
            m2 = jnp.max(l2, axis=1, keepdims=True)
            i2 = jnp.min(jnp.where(l2 == m2, lane, LANES), axis=1, keepdims=True)
            e2 = jnp.exp(m2 - m1)
            g1 = 1.0 / (1.0 + e2)
            g2 = e2 / (1.0 + e2)
            comb_ref[r, :] = (jnp.where(lane == 0, i1.astype(F32), 0.0)
                              + jnp.where(lane == 1, i2.astype(F32), 0.0)
                              + jnp.where(lane == 2, g1, 0.0) + jnp.where(lane == 3, g2, 0.0))


def _mixout(x2, ya, yb, p, wpa, wpb, wout, g2, router, tm):
    n = x2.shape[0]
    with_router = router is not None
    row = lambda w: pl.BlockSpec((tm, w), lambda i: (i, 0))
    full = lambda a: pl.BlockSpec(a.shape, lambda i: (0, 0))
    in_specs = [
        row(D_MODEL), row(RWKV_WIDTH), row(GDN_WIDTH),
        pl.BlockSpec((tm, D_MODEL), lambda i: (i, COL_GATE // D_MODEL)),
        pl.BlockSpec((tm, D_MODEL), lambda i: (i, COL_GATE // D_MODEL + 1)),
        full(wpa), full(wpb), full(wout), full(g2),
    ]
    args = [x2, ya, yb, p, p, wpa, wpb, wout, g2]
    out_specs = [row(D_MODEL), row(D_MODEL)]
    out_shape = [jax.ShapeDtypeStruct((n, D_MODEL), F32),
                 jax.ShapeDtypeStruct((n, D_MODEL), F32 if with_router else BF16)]
    if with_router:
        in_specs.append(full(router))
        args.append(router)
        out_specs.append(row(LANES))
        out_shape.append(jax.ShapeDtypeStruct((n, LANES), F32))
    return pl.pallas_call(
        functools.partial(_mixout_kernel, with_router=with_router),
        grid=(n // tm,),
        in_specs=in_specs,
        out_specs=out_specs,
        out_shape=out_shape,
        compiler_params=pltpu.CompilerParams(
            dimension_semantics=("parallel",), vmem_limit_bytes=VMEM_LIMIT),
        name="mixout_router" if with_router else "mixout",
    )(*args)


def _rms_out(y, fg_ref):
    return y * lax.rsqrt(jnp.mean(y * y, axis=-1, keepdims=True) + NORM_EPS) * fg_ref[...]


def _gate_up(wg, wu, tf):
    parts = []
    for j in range(D_FF // tf):
        parts += [wg[..., j * tf:(j + 1) * tf], wu[..., j * tf:(j + 1) * tf]]
    return jnp.concatenate(parts, axis=-1).astype(BF16)


def _swiglu_chunk(h, wgu, wd):
    tf = wd.shape[0]
    gu = jnp.dot(h, wgu, preferred_element_type=F32)
    act = _silu(gu[:, 0:tf]) * gu[:, tf:]
    return jnp.dot(act.astype(BF16), wd, preferred_element_type=F32)


def _ffn_kernel(h_ref, x_ref, wgu_ref, wd_ref, *rest, final):
    fg_ref = rest[0] if final else None
    o_ref, acc_ref = rest[-2:]
    j = pl.program_id(1)

    @pl.when(j == 0)
    def _():
        acc_ref[...] = jnp.zeros_like(acc_ref)

    acc_ref[...] += _swiglu_chunk(h_ref[...], wgu_ref[...], wd_ref[...])

    @pl.when(j == pl.num_programs(1) - 1)
    def _():
        y = x_ref[...] + acc_ref[...]
        o_ref[...] = _rms_out(y, fg_ref) if final else y


def _ffn(h, x2, wgu, wd, final_g, tm, tf):
    n = x2.shape[0]
    final = final_g is not None
    row = lambda w: pl.BlockSpec((tm, w), lambda i, j: (i, 0))
    in_specs = [
        row(D_MODEL), row(D_MODEL),
        pl.BlockSpec((D_MODEL, 2 * tf), lambda i, j: (0, j)),
        pl.BlockSpec((tf, D_MODEL), lambda i, j: (j, 0)),
    ]
    args = [h, x2, wgu, wd]
    if final:
        in_specs.append(pl.BlockSpec((1, D_MODEL), lambda i, j: (0, 0)))
        args.append(final_g)
    return pl.pallas_call(
        functools.partial(_ffn_kernel, final=final),
        grid=(n // tm, D_FF // tf),
        in_specs=in_specs,
        out_specs=row(D_MODEL),
        out_shape=jax.ShapeDtypeStruct((n, D_MODEL), F32),
        scratch_shapes=[pltpu.VMEM((tm, D_MODEL), F32)],
        compiler_params=pltpu.CompilerParams(
            dimension_semantics=("parallel", "arbitrary"), vmem_limit_bytes=VMEM_LIMIT),
        name="ffn",
    )(*args)


MOE_ROWS = 512


def _plan_kernel(sel_ref, pos_ref, tile_ref, cnt_scr, *, tm, n_tiles):
    ph = pl.program_id(0)
    i = pl.program_id(1)
    sel = sel_ref[...]
    lane = _iota((tm, LANES), 1)
    e1 = jnp.sum(jnp.where(lane == 0, sel, 0.0), axis=1, keepdims=True).astype(jnp.int32)
    e2 = jnp.sum(jnp.where(lane == 1, sel, 0.0), axis=1, keepdims=True).astype(jnp.int32)
    oh1 = lane == e1
    oh2 = lane == e2
    both = jnp.where(oh1, 1.0, 0.0) + jnp.where(oh2, 1.0, 0.0)

    @pl.when((ph == 0) & (i == 0))
    def _():
        cnt_scr[...] = jnp.zeros_like(cnt_scr)

    @pl.when(ph == 0)
    def _():
        cnt_scr[0:1, :] += jnp.sum(both, axis=0, keepdims=True)

    @pl.when((ph == 1) & (i == 0))
    def _():
        cnt = cnt_scr[0:1, :]
        padded = jnp.floor((cnt + (MOE_ROWS - 1)) * (1.0 / MOE_ROWS)) * MOE_ROWS
        rl = _iota((LANES, LANES), 0)
        cl = _iota((LANES, LANES), 1)
        upper = jnp.where(rl < cl, 1.0, 0.0)
        starts = _dot_hi(jnp.broadcast_to(padded, (8, LANES)), upper)[0:1, :]
        ends = starts + padded
        cnt_scr[0:1, :] = starts
        tl = _iota((n_tiles, LANES), 1)
        t0 = (_iota((n_tiles, LANES), 0) * MOE_ROWS).astype(F32)
        lane_ok = tl < N_EXPERTS
        owner = jnp.sum(jnp.where(lane_ok, jnp.where(t0 >= ends, 1.0, 0.0), 0.0),
                        axis=1, keepdims=True)
        used = jnp.sum(jnp.where(lane_ok, padded, 0.0), axis=1, keepdims=True)
        valid = t0[:, 0:1] < used
        last_e = jnp.max(jnp.where(lane_ok, jnp.where(padded > 0.0, tl.astype(F32), 0.0), 0.0),
                         axis=1, keepdims=True)
        ex = jnp.where(valid, jnp.minimum(owner, N_EXPERTS - 1.0), last_e)
        tile_ref[...] = (jnp.where(tl == 0, ex, 0.0)
                         + jnp.where(tl == 1, jnp.where(valid, 1.0, 0.0), 0.0)).astype(jnp.int32)

    @pl.when(ph == 1)
    def _():
        rr = _iota((tm, tm), 0)
        cc = _iota((tm, tm), 1)
        before = jnp.where(cc < rr, 1.0, 0.0).astype(BF16)
        base = cnt_scr[0:1, :] + jnp.dot(before, both.astype(BF16), preferred_element_type=F32)
        p1 = jnp.sum(jnp.where(oh1, base, 0.0), axis=1, keepdims=True)
        p2 = jnp.sum(jnp.where(oh2, base, 0.0), axis=1, keepdims=True)
        pos_ref[...] = (jnp.where(lane == 0, p1, 0.0) + jnp.where(lane == 1, p2, 0.0)
                        ).astype(jnp.int32)
        cnt_scr[0:1, :] += jnp.sum(both, axis=0, keepdims=True)


def _plan(sel, tm, n_tiles):
    n = sel.shape[0]
    return pl.pallas_call(
        functools.partial(_plan_kernel, tm=tm, n_tiles=n_tiles),
        grid=(2, n // tm),
        in_specs=[pl.BlockSpec((tm, LANES), lambda ph, i: (i, 0))],
        out_specs=[pl.BlockSpec((tm, LANES), lambda ph, i: (i * ph, 0)),
                   pl.BlockSpec((n_tiles, LANES), lambda ph, i: (0, 0))],
        out_shape=[jax.ShapeDtypeStruct((n, LANES), jnp.int32),
                   jax.ShapeDtypeStruct((n_tiles, LANES), jnp.int32)],
        scratch_shapes=[pltpu.VMEM((8, LANES), F32)],
        compiler_params=pltpu.CompilerParams(
            dimension_semantics=("arbitrary", "arbitrary"), vmem_limit_bytes=VMEM_LIMIT),
        name="moe_plan",
    )(sel)


def _row_copy(src_ref, src_row, dst_ref, dst_row, sem):
    return pltpu.make_async_copy(src_ref.at[pl.ds(src_row, 1)], dst_ref.at[pl.ds(dst_row, 1)], sem)


def _dispatch_kernel(pos_ref, h_ref, xs_in_ref, xs_ref, sem, *, tm):
    del xs_in_ref

    def start(r, c):
        _row_copy(h_ref, r, xs_ref, pos_ref[2 * r], sem).start(priority=0)
        _row_copy(h_ref, r, xs_ref, pos_ref[2 * r + 1], sem).start(priority=1)
        return c

    def wait(r, c):
        _row_copy(h_ref, 0, xs_ref, 0, sem).wait()
        _row_copy(h_ref, 0, xs_ref, 0, sem).wait()
        return c

    lax.fori_loop(0, tm, start, 0, unroll=8)
    lax.fori_loop(0, tm, wait, 0, unroll=8)


def _dispatch(pos_flat, h, n_rows, tm):
    n = h.shape[0]
    xs0 = jnp.zeros((n_rows, D_MODEL), F32)
    return pl.pallas_call(
        functools.partial(_dispatch_kernel, tm=tm),
        grid=(n // tm,),
        in_specs=[
            pl.BlockSpec((2 * tm,), lambda i: (i,), memory_space=pltpu.SMEM),
            pl.BlockSpec((tm, D_MODEL), lambda i: (i, 0)),
            pl.BlockSpec(memory_space=pl.ANY),
        ],
        out_specs=pl.BlockSpec(memory_space=pl.ANY),
        out_shape=jax.ShapeDtypeStruct((n_rows, D_MODEL), F32),
        scratch_shapes=[pltpu.SemaphoreType.DMA(())],
        input_output_aliases={2: 0},
        compiler_params=pltpu.CompilerParams(
            dimension_semantics=("arbitrary",), vmem_limit_bytes=VMEM_LIMIT),
        name="moe_dispatch",
    )(pos_flat, h, xs0)


def _experts_kernel(te_ref, tv_ref, xs_ref, wgu_ref, wd_ref, ys_ref, xb_scr, acc_scr):
    t = pl.program_id(0)
    j = pl.program_id(1)
    del te_ref

    @pl.when(j == 0)
    def _():
        xb_scr[...] = xs_ref[...].astype(BF16)
        acc_scr[...] = jnp.zeros_like(acc_scr)

    @pl.when(tv_ref[t] == 1)
    def _():
        acc_scr[...] += _swiglu_chunk(xb_scr[...], wgu_ref[0], wd_ref[0])

    @pl.when(j == pl.num_programs(1) - 1)
    def _():
        ys_ref[...] = acc_scr[...]


def _experts(tile_e, tile_v, xs, wgu, wd, tf):
    n_rows = xs.shape[0]
    grid_spec = pltpu.PrefetchScalarGridSpec(
        num_scalar_prefetch=2,
        grid=(n_rows // MOE_ROWS, D_FF // tf),
        in_specs=[
            pl.BlockSpec((MOE_ROWS, D_MODEL), lambda t, j, te, tv: (t, 0)),
            pl.BlockSpec((1, D_MODEL, 2 * tf), lambda t, j, te, tv: (te[t], 0, j)),
            pl.BlockSpec((1, tf, D_MODEL), lambda t, j, te, tv: (te[t], j, 0)),
        ],
        out_specs=pl.BlockSpec((MOE_ROWS, D_MODEL), lambda t, j, te, tv: (t, 0)),
        scratch_shapes=[pltpu.VMEM((MOE_ROWS, D_MODEL), BF16), pltpu.VMEM((MOE_ROWS, D_MODEL), F32)],
    )
    return pl.pallas_call(
        _experts_kernel,
        grid_spec=grid_spec,
        out_shape=jax.ShapeDtypeStruct((n_rows, D_MODEL), F32),
        compiler_params=pltpu.CompilerParams(
            dimension_semantics=("parallel", "arbitrary"), vmem_limit_bytes=VMEM_LIMIT),
        name="moe_experts",
    )(tile_e, tile_v, xs, wgu, wd)


def _combine_kernel(pos_ref, x_ref, sel_ref, *rest, tm, final):
    fg_ref = rest[0] if final else None
    ys_ref, o_ref, buf, sem = rest[-4:]

    def start(r, c):
        _row_copy(ys_ref, pos_ref[2 * r], buf.at[0], r, sem).start(priority=0)
        _row_copy(ys_ref, pos_ref[2 * r + 1], buf.at[1], r, sem).start(priority=1)
        return c

    def wait(r, c):
        _row_copy(ys_ref, 0, buf.at[0], 0, sem).wait()
        _row_copy(ys_ref, 0, buf.at[1], 0, sem).wait()
        return c

    lax.fori_loop(0, tm, start, 0, unroll=8)
    lax.fori_loop(0, tm, wait, 0, unroll=8)
    sel = sel_ref[...]
    lane = _iota(sel.shape, 1)
    g1 = jnp.sum(jnp.where(lane == 2, sel, 0.0), axis=1, keepdims=True)
    g2 = jnp.sum(jnp.where(lane == 3, sel, 0.0), axis=1, keepdims=True)
    y = x_ref[...] + g1 * buf[0] + g2 * buf[1]
    o_ref[...] = _rms_out(y, fg_ref) if final else y


def _combine(pos_flat, x2, sel, ys, final_g, tm):
    n = x2.shape[0]
    final = final_g is not None
    in_specs = [
        pl.BlockSpec((2 * tm,), lambda i: (i,), memory_space=pltpu.SMEM),
        pl.BlockSpec((tm, D_MODEL), lambda i: (i, 0)),
        pl.BlockSpec((tm, LANES), lambda i: (i, 0)),
    ]
    args = [pos_flat, x2, sel]
    if final:
        in_specs.append(pl.BlockSpec((1, D_MODEL), lambda i: (0, 0)))
        args.append(final_g)
    in_specs.append(pl.BlockSpec(memory_space=pl.ANY))
    args.append(ys)
    return pl.pallas_call(
        functools.partial(_combine_kernel, tm=tm, final=final),
        grid=(n // tm,),
        in_specs=in_specs,
        out_specs=pl.BlockSpec((tm, D_MODEL), lambda i: (i, 0)),
        out_shape=jax.ShapeDtypeStruct((n, D_MODEL), F32),
        scratch_shapes=[pltpu.VMEM((2, tm, D_MODEL), F32), pltpu.SemaphoreType.DMA(())],
        compiler_params=pltpu.CompilerParams(
            dimension_semantics=("arbitrary",), vmem_limit_bytes=VMEM_LIMIT),
        name="moe_combine",
    )(*args)


def _moe(h, x2, sel, wgu, wd, final_g, tm, tf):
    n = x2.shape[0]
    n_tiles = pl.cdiv(2 * n, MOE_ROWS) + N_EXPERTS
    pos, tiles = _plan(sel, tm, n_tiles)
    pos_flat = pos[:, 0:2].reshape(2 * n)
    xs = _dispatch(pos_flat, h, n_tiles * MOE_ROWS, tm)
    ys = _experts(tiles[:, 0], tiles[:, 1], xs, wgu, wd, tf)
    return _combine(pos_flat, x2, sel, ys, final_g, tm)


def _tile(n, pref):
    t = min(pref, n)
    while n % t:
        t //= 2
    return t


def _pad_lanes(a, width=LANES):
    return jnp.pad(a, ((0, 0), (0, width - a.shape[-1])))


def kernel(x, norm1_g, w_in, tshift_mu, w0, w_decay_up, a0, w_iclr_up, w_gate_up, k_k, k_a, r_k,
           lnx_g, lnx_b, w_pa, conv_w, a_log, dt_bias, gdn_norm_g, w_pb, w_out, norm2_g,
           ffn_w_gate, ffn_w_up, ffn_w_down, moe_router, moe_w_gate, moe_w_up, moe_w_down, final_g):
    nb, seq, d = x.shape
    n = nb * seq
    x2 = x.reshape(n, d)
    tm_in = _tile(n, 1024)
    tm = _tile(n, 512)
    rw = 1792
    gq = 3 * GDN_WIDTH
    for layer in range(DEPTH):
        wi = w_in[layer]
        ab = wi[:, rw + gq:rw + gq + 2 * GDN_HEADS]
        zc = wi[:, rw + gq + 2 * GDN_HEADS:rw + gq + 2 * GDN_HEADS + GDN_WIDTH]
        gate = wi[:, rw + gq + 2 * GDN_HEADS + GDN_WIDTH:]
        w_cat = jnp.concatenate(
            [wi[:, :rw], ab, jnp.zeros((d, COL_QKV - COL_AB - 2 * GDN_HEADS), F32),
             wi[:, rw:rw + gq], zc, gate],
            axis=1).astype(BF16)
        p = _inproj(x2, norm1_g[layer][None, :], w_cat, tm_in, 1536)
        one = lambda a: a[layer][None, :]
        ya = _rwkv(p, seq, one(tshift_mu), one(w0), one(a0), one(k_k), one(k_a), one(r_k),
                   one(lnx_g), one(lnx_b), w_decay_up[layer].astype(BF16),
                   w_iclr_up[layer].astype(BF16), w_gate_up[layer].astype(BF16))
        yb = _gdn(p, seq, conv_w[layer], _pad_lanes(one(a_log)), _pad_lanes(one(dt_bias)),
                  one(gdn_norm_g))
        j = layer // 2
        routed = layer % 2 == 1
        router = _pad_lanes(moe_router[j]) if routed else None
        res = _mixout(x2, ya, yb, p, w_pa[layer].astype(BF16), w_pb[layer].astype(BF16),
                      w_out[layer].astype(BF16), one(norm2_g), router, tm)
        fg = final_g[None, :] if layer == DEPTH - 1 else None
        if routed:
            x2, h, sel = res
            x2 = _moe(h, x2, sel, _gate_up(moe_w_gate[j], moe_w_up[j], FF_CHUNK),
                      moe_w_down[j].astype(BF16), fg, tm, FF_CHUNK)
        else:
            x2, h = res
            x2 = _ffn(h, x2, _gate_up(ffn_w_gate[j], ffn_w_up[j], FF_CHUNK),
                      ffn_w_down[j].astype(BF16), fg, tm, FF_CHUNK)
    return x2.reshape(nb, seq, d)
```

```python
import functools

import jax
import jax.numpy as jnp
from jax import lax
from jax.experimental import pallas as pl
from jax.experimental.pallas import tpu as pltpu

F32 = jnp.float32
BF16 = jnp.bfloat16
HI = lax.Precision.HIGHEST

D_MODEL = 1024
DEPTH = 2
RWKV_HEADS = 8
RWKV_HEAD_DIM = 64
RWKV_WIDTH = 512
DECAY_LORA = 64
ICLR_LORA = 64
GATE_LORA = 128
RWKV_GN_EPS = 64e-5
GDN_HEADS = 4
GDN_HEAD_DIM = 128
GDN_WIDTH = 512
CONV_WIDTH = 4
CHUNK = 64
D_FF = 2816
N_EXPERTS = 8
NORM_EPS = 1e-6
L2_EPS = 1e-6

LANES = 128
P_COLS = 6144
COL_AB = 1792
COL_QKV = 2048
COL_Z = 3584
COL_GATE = 4096
VMEM_LIMIT = 56 * 1024 * 1024
FF_CHUNK = D_FF // 2


def _dot(a, b):
    return jnp.dot(a.astype(BF16), b.astype(BF16), preferred_element_type=F32)


def _dot_nt(a, b):
    return lax.dot_general(a.astype(BF16), b.astype(BF16), (((1,), (1,)), ((), ())),
                           preferred_element_type=F32)


def _dot_tn(a, b):
    return lax.dot_general(a.astype(BF16), b.astype(BF16), (((0,), (0,)), ((), ())),
                           preferred_element_type=F32)


def _dot_hi(a, b):
    return jnp.dot(a, b, preferred_element_type=F32, precision=HI)


def _softplus(x):
    return jnp.maximum(x, 0.0) + jnp.log(1.0 + jnp.exp(-jnp.abs(x)))


def _sigmoid(x):
    return 1.0 / (1.0 + jnp.exp(-x))


def _silu(x):
    return x * _sigmoid(x)


def _iota(shape, dim):
    return lax.broadcasted_iota(jnp.int32, shape, dim)


def _inproj_kernel(x_ref, g_ref, w_ref, p_ref, h_scr):
    @pl.when(pl.program_id(1) == 0)
    def _():
        x = x_ref[...]
        h = x * lax.rsqrt(jnp.mean(x * x, axis=-1, keepdims=True) + NORM_EPS) * g_ref[...]
        h_scr[...] = h.astype(BF16)

    p_ref[...] = jnp.dot(h_scr[...], w_ref[...], preferred_element_type=F32).astype(p_ref.dtype)


def _inproj(x2, g, w_cat, tm, tn):
    n = x2.shape[0]
    return pl.pallas_call(
        _inproj_kernel,
        grid=(n // tm, P_COLS // tn),
        in_specs=[
            pl.BlockSpec((tm, D_MODEL), lambda i, j: (i, 0)),
            pl.BlockSpec((1, D_MODEL), lambda i, j: (0, 0)),
            pl.BlockSpec((D_MODEL, tn), lambda i, j: (0, j)),
        ],
        out_specs=pl.BlockSpec((tm, tn), lambda i, j: (i, j)),
        out_shape=jax.ShapeDtypeStruct((n, P_COLS), BF16),
        scratch_shapes=[pltpu.VMEM((tm, D_MODEL), BF16)],
        compiler_params=pltpu.CompilerParams(
            dimension_semantics=("parallel", "arbitrary"), vmem_limit_bytes=VMEM_LIMIT),
        name="inproj",
    )(x2, g, w_cat)


def _split_dot(x, m):
    hi = x.astype(BF16)
    lo = (x - hi.astype(F32)).astype(BF16)
    return (jnp.dot(hi, m, preferred_element_type=F32) + jnp.dot(lo, m, preferred_element_type=F32))


def _chunk_cumsum(x, tin):
    y = x
    sh = 1
    while sh < CHUNK:
        y = y + jnp.where(tin >= sh, pltpu.roll(y, sh, axis=0), 0.0)
        sh *= 2
    return y


def _level_masks(rm, lm):
    out = []
    for k in range(6):
        x = rm >> k
        y = lm >> k
        out.append((((x ^ y) + 2 * (1 - (x & 1))) == 1).astype(F32))
    return out


def _tri_inverse(mats, eye, masks):
    ts = [eye + a * masks[0] for a in mats]
    for k in range(1, 6):
        xs = [_dot(a * masks[k], t) for a, t in zip(mats, ts)]
        ts = [t + _dot(t, x) for t, x in zip(ts, xs)]
    return ts


def _rwkv_kernel(r_ref, k_ref, v_ref, lo_ref, mur_ref, muk_ref, muv_ref, mulo_ref,
                 w0_ref, a0_ref, kk_ref, ka_ref, rk_ref, lg_ref, lb_ref,
                 wd_ref, wa_ref, wg_ref, bd_ref, o_ref,
                 lhs_scr, u0_scr, qb_scr, ov_scr, c_scr, pre_scr, *, nb, ns):
    L = CHUNK
    W = LANES
    R = nb * L
    seq = r_ref.shape[0]
    nsb = seq // R

    row_r = _iota((R, W), 0)
    row_r2 = _iota((R, 2 * W), 0)
    row_c = _iota((L, W), 0)
    lane = _iota((L, W), 1)
    h1 = lane < 64
    rm = _iota((W, W), 0)
    lm = _iota((W, W), 1)
    tri_mask = (lm & 63) < (rm & 63) + (rm >> 6)
    blk_mask = (rm >> 6) == (lm >> 6)
    eye = (rm == lm).astype(F32)
    masks = _level_masks(rm, lm)

    mur, muk, muv, mulo = mur_ref[...], muk_ref[...], muv_ref[...], mulo_ref[...]
    w0, a0, k_k, k_a, r_k = w0_ref[...], a0_ref[...], kk_ref[...], ka_ref[...], rk_ref[...]
    ln_g, ln_b = lg_ref[...], lb_ref[...]
    wd, wa, wg = wd_ref[...], wa_ref[...], wg_ref[...]

    def shift(x, prev_row, mu, rowi):
        xp = pltpu.roll(x, 1, axis=0)
        xp = jnp.where(rowi == 0, prev_row, xp)
        return x + (xp - x) * mu

    def last_row_before(ref, s, cols):
        rows = ref[pl.ds(pl.multiple_of(jnp.maximum(s - 16, 0), 16), 16), cols].astype(F32)
        return rows[15:16, :] * (s > 0).astype(F32)

    keys = ("v", "k2", "bh", "cl", "g", "bonus", "rt", "at", "bt", "kt")

    class Stored:
        def __init__(self, par, j):
            self.par, self.j = par, j

        def __getitem__(self, key):
            par, j, q = self.par, self.j, keys.index(key)

            class Rows:
                def __getitem__(self, rows):
                    return pre_scr[par, j, q, rows]
            return Rows()

    def prologue_pieces(sb, par):
        sh = {}

        def shared():
            s = pl.multiple_of(sb * R, R)
            lo_raw = lo_ref[pl.ds(s, R), :].astype(F32)
            lo = shift(lo_raw, last_row_before(lo_ref, s, slice(None)), mulo, row_r2)
            wl = lo[:, 0:DECAY_LORA]
            al = lo[:, DECAY_LORA:DECAY_LORA + ICLR_LORA]
            gl = lo[:, DECAY_LORA + ICLR_LORA:]
            sh["z"] = w0 + _dot(jnp.tanh(wl), wd)
            sh["a"] = _sigmoid(a0 + _dot(al, wa))
            sh["g"] = _dot(_sigmoid(gl), wg)
            for j in range(ns):
                sj = slice(j * W, (j + 1) * W)
                for name, ref, mu in (("r", r_ref, mur), ("k", k_ref, muk), ("v", v_ref, muv)):
                    sh[name, j] = shift(ref[pl.ds(s, R), sj].astype(F32), last_row_before(ref, s, sj),
                                        mu[:, sj], row_r)

        def piece(j, c):
            sj = slice(j * W, (j + 1) * W)
            r, k, v = sh["r", j][c], sh["k", j][c], sh["v", j][c]
            w_log = -_softplus(-sh["z"][c, sj]) - 0.5
            lw = -jnp.exp(w_log)
            a = sh["a"][c, sj]
            kk = k * k_k[:, sj]
            kk = kk * lax.rsqrt(_split_dot(kk * kk, bd_ref[...]) + L2_EPS)
            k2 = k * (1.0 + (a - 1.0) * k_a[:, sj])
            bh = kk * a
            cl = _chunk_cumsum(lw, row_c)
            e_neg = jnp.exp(-cl)
            vals = dict(
                v=v, k2=k2, bh=bh, cl=cl, g=sh["g"][c, sj],
                bonus=_split_dot(r * k2 * r_k[:, sj], bd_ref[...]) * v,
                rt=r * jnp.exp(cl), at=-kk * jnp.exp(cl - lw), bt=bh * e_neg, kt=k2 * e_neg)
            for q, key in enumerate(keys):
                pre_scr[par, j, q, c] = vals[key]

        return [shared] + [functools.partial(piece, j, slice(i * L, (i + 1) * L))
                           for i in range(nb) for j in range(ns)]

    def run(par, Hs, sb, fillers):
        pre = [Stored(par, j) for j in range(ns)]

        def fill():
            if fillers:
                fillers.pop(0)()

        s = pl.multiple_of(sb * R, R)
        items = [(j, slice(i * L, (i + 1) * L)) for i in range(nb) for j in range(ns)]
        v_of = lambda it: pre[it[0]]["v"][it[1]]
        rts = [pre[j]["rt"][c] for j, c in items]
        ats = [pre[j]["at"][c] for j, c in items]
        zero = jnp.zeros((L, W), F32)
        m1s, m2s = [], []
        for (j, c), rt, at in zip(items, rts, ats):
            bt, kt = pre[j]["bt"][c], pre[j]["kt"][c]
            lhs1 = jnp.concatenate([jnp.where(h1, at, zero), jnp.where(h1, rt, zero)], axis=0)
            lhs2 = jnp.concatenate([jnp.where(h1, zero, at), jnp.where(h1, zero, rt)], axis=0)
            m1s.append(jnp.where(tri_mask, _dot_nt(lhs1, jnp.concatenate([bt, kt], axis=0)), 0.0))
            m2s.append(jnp.where(tri_mask, _dot_nt(lhs2, jnp.concatenate([kt, bt], axis=0)), 0.0))
        fill()
        a_bds = [jnp.concatenate([jnp.where(h1, m1[0:L], 0.0), jnp.where(h1, 0.0, m2[0:L])], axis=0)
                 for m1, m2 in zip(m1s, m2s)]
        tinvs = _tri_inverse(a_bds, eye, masks)
        avs, o_vs = [], []
        for it, m1, m2 in zip(items, m1s, m2s):
            vc = v_of(it)
            ak_m = jnp.concatenate(
                [jnp.where(h1, 0.0, m1[0:L]), jnp.where(h1, m2[0:L], 0.0)], axis=0)
            avs.append(_dot(ak_m, jnp.concatenate([vc, vc], axis=0)))
            qk = jnp.where(h1, m2[L:], m1[L:])
            vx = jnp.concatenate([jnp.where(h1, 0.0, vc), jnp.where(h1, vc, 0.0)], axis=0)
            o_vs.append(_dot(qk, vx))
        qbs = [jnp.where(h1, m1[L:], m2[L:]) for m1, m2 in zip(m1s, m2s)]
        u0s = [_dot(t, av) for t, av in zip(tinvs, avs)]
        tas = [jnp.where(blk_mask, _dot(t, jnp.concatenate([at, at], axis=0)), 0.0)
               for t, at in zip(tinvs, ats)]
        for i, ((j, c), ta, u0) in enumerate(zip(items, tas, u0s)):
            cl_i = pre[j]["cl"][c]
            cl_last = cl_i[L - 1:L, :]
            e_end = jnp.exp(cl_last - cl_i)
            b_end = pre[j]["bh"][c] * e_end
            b_st = jnp.concatenate([jnp.where(h1, b_end, 0.0), jnp.where(h1, 0.0, b_end)], axis=0)
            u0_bd = jnp.where(blk_mask, u0, 0.0)
            m_t = eye * jnp.exp(cl_last) + _dot_tn(b_st, ta)
            lhs_scr[i] = jnp.concatenate([m_t, ta, rts[i]], axis=0).astype(BF16)
            u0_scr[i] = u0_bd
            qb_scr[i] = qbs[i].astype(BF16)
            ov_scr[i] = o_vs[i]
            c_scr[i] = (_dot_tn(b_st, u0_bd)
                        + jnp.where(blk_mask, _dot_tn(pre[j]["k2"][c] * e_end, v_of((j, c))), 0.0))

        Hs = list(Hs)
        bigs = []
        for i, (j, c) in enumerate(items):
            big = jnp.dot(lhs_scr[i], Hs[j].astype(BF16), preferred_element_type=F32)
            Hs[j] = big[0:W] + c_scr[i]
            bigs.append(big)
            fill()
        outs = [[] for _ in range(ns)]
        for i, (j, c) in enumerate(items):
            u_bd = jnp.where(blk_mask, bigs[i][W:2 * W], 0.0) + u0_scr[i]
            outs[j].append(bigs[i][2 * W:] + _dot(qb_scr[i], u_bd) + ov_scr[i])

        for j in range(ns):
            sj = slice(j * W, (j + 1) * W)
            o = jnp.concatenate(outs[j], axis=0)
            mean = _split_dot(o, bd_ref[...]) * (1.0 / 64.0)
            oc = o - mean
            var = _split_dot(oc * oc, bd_ref[...]) * (1.0 / 64.0)
            y = (oc * lax.rsqrt(var + RWKV_GN_EPS) * ln_g[:, sj] + ln_b[:, sj]
                 + pre[j]["bonus"][slice(0, R)])
            o_ref[pl.ds(s, R), sj] = (y * pre[j]["g"][slice(0, R)]).astype(o_ref.dtype)
        while fillers:
            fill()
        return tuple(Hs)

    for thunk in prologue_pieces(0, 0):
        thunk()

    def body(t, Hs):
        Hs = run(0, Hs, 2 * t, prologue_pieces(2 * t + 1, 1))
        return run(1, Hs, 2 * t + 1, prologue_pieces(jnp.minimum(2 * t + 2, nsb - 1), 0))

    lax.fori_loop(0, nsb // 2, body, tuple(jnp.zeros((W, W), F32) for _ in range(ns)))


def _rwkv(p, seq, mu, w0, a0, k_k, k_a, r_k, ln_g, ln_b, wd, wa, wg):
    n = p.shape[0]
    nbatch = n // seq
    ns = 4
    wide = ns * LANES
    ngrp = RWKV_WIDTH // wide
    nb = 4 if seq % (8 * CHUNK) == 0 else 1

    def col(off):
        return pl.BlockSpec((seq, wide), lambda b, q, off=off: (b, off + q))

    def par(off=0):
        return pl.BlockSpec((1, wide), lambda b, q, off=off: (0, off + q))

    in_specs = [
        col(0), col(ngrp), col(2 * ngrp),
        pl.BlockSpec((seq, 2 * LANES), lambda b, q: (b, 3 * RWKV_WIDTH // (2 * LANES))),
        par(0), par(ngrp), par(2 * ngrp),
        pl.BlockSpec((1, 2 * LANES), lambda b, q: (0, 3 * RWKV_WIDTH // (2 * LANES))),
        par(), par(), par(), par(), par(), par(), par(),
        pl.BlockSpec((DECAY_LORA, wide), lambda b, q: (0, q)),
        pl.BlockSpec((ICLR_LORA, wide), lambda b, q: (0, q)),
        pl.BlockSpec((GATE_LORA, wide), lambda b, q: (0, q)),
        pl.BlockSpec((LANES, LANES), lambda b, q: (0, 0)),
    ]
    head_id = jnp.arange(LANES) // RWKV_HEAD_DIM
    bd_ones = (head_id[:, None] == head_id[None, :]).astype(BF16)
    return pl.pallas_call(
        functools.partial(_rwkv_kernel, nb=nb, ns=ns),
        grid=(nbatch, ngrp),
        in_specs=in_specs,
        out_specs=pl.BlockSpec((seq, wide), lambda b, q: (b, q)),
        out_shape=jax.ShapeDtypeStruct((n, RWKV_WIDTH), BF16),
        scratch_shapes=[
            pltpu.VMEM((nb * ns, 2 * LANES + CHUNK, LANES), BF16),
            pltpu.VMEM((nb * ns, LANES, LANES), F32),
            pltpu.VMEM((nb * ns, CHUNK, LANES), BF16),
            pltpu.VMEM((nb * ns, CHUNK, LANES), F32),
            pltpu.VMEM((nb * ns, LANES, LANES), F32),
            pltpu.VMEM((2, ns, 10, nb * CHUNK, LANES), F32),
        ],
        compiler_params=pltpu.CompilerParams(
            dimension_semantics=("parallel", "parallel"), vmem_limit_bytes=VMEM_LIMIT),
        name="rwkv",
    )(p, p, p, p, mu, mu, mu, mu, w0, a0, k_k, k_a, r_k, ln_g, ln_b, wd, wa, wg, bd_ones)


def _gdn_kernel(q_ref, k_ref, v_ref, z_ref, ab_ref, cwq_ref, cwk_ref, cwv_ref,
                alog_ref, dtb_ref, ng_ref, o_ref, lhs_scr, u_scr, qk_scr, c_scr, *, nb, ns):
    L = CHUNK
    W = LANES
    P = 2 * L
    R = nb * P
    seq = q_ref.shape[0]
    nsb = seq // R
    hd0 = pl.program_id(1) * ns

    row_r = _iota((R, W), 0)
    lane_r = _iota((R, W), 1)
    tin = row_r & (L - 1)
    row8 = _iota((8, W), 0)
    rm = _iota((W, W), 0)
    lm = _iota((W, W), 1)
    dif = rm - lm
    tin_m = rm & (L - 1)
    eye = (rm == lm).astype(F32)
    masks = _level_masks(rm, lm)

    lane1 = _iota((1, W), 1)
    pick = lambda ref, j: jnp.sum(jnp.where(lane1 == hd0 + j, ref[...], 0.0), axis=1, keepdims=True)
    neg_a = [-jnp.exp(pick(alog_ref, j)) for j in range(ns)]
    dt_b = [pick(dtb_ref, j) for j in range(ns)]
    cwq, cwk, cwv = cwq_ref[...], cwk_ref[...], cwv_ref[...]
    ng = ng_ref[...]
    scale = GDN_HEAD_DIM ** -0.5
    neg_inf = jnp.float32(-jnp.inf)

    def conv(x, prev8, cw):
        acc = x * cw[CONV_WIDTH - 1:CONV_WIDTH, :]
        for j in range(1, CONV_WIDTH):
            xs = pltpu.roll(x, j, axis=0)
            ps = pltpu.roll(prev8, j, axis=0)
            head = jnp.where(row8 < j, ps, xs[0:8])
            xs = jnp.concatenate([head, xs[8:]], axis=0)
            acc = acc + xs * cw[CONV_WIDTH - 1 - j:CONV_WIDTH - j, :]
        return _silu(acc)

    def l2n(x):
        return x * lax.rsqrt(jnp.sum(x * x, axis=-1, keepdims=True) + L2_EPS)

    def body(sb, carry):
        Ss, pqs, pks, pvs = carry
        s = pl.multiple_of(sb * R, R)
        ab = ab_ref[pl.ds(s, R), :].astype(F32)

        pre, raws = [], []
        for j in range(ns):
            sj = slice(j * W, (j + 1) * W)
            hd = hd0 + j
            q_raw = q_ref[pl.ds(s, R), sj].astype(F32)
            k_raw = k_ref[pl.ds(s, R), sj].astype(F32)
            v_raw = v_ref[pl.ds(s, R), sj].astype(F32)
            raws.append((q_raw[R - 8:, :], k_raw[R - 8:, :], v_raw[R - 8:, :]))
            q = l2n(conv(q_raw, pqs[j], cwq[:, sj])) * scale
            k = l2n(conv(k_raw, pks[j], cwk[:, sj]))
            v = conv(v_raw, pvs[j], cwv[:, sj])
            a_in = jnp.sum(jnp.where(lane_r == hd, ab, 0.0), axis=1, keepdims=True)
            b_in = jnp.sum(jnp.where(lane_r == hd + GDN_HEADS, ab, 0.0), axis=1, keepdims=True)
            beta = _sigmoid(b_in)
            g_log = neg_a[j] * _softplus(a_in + dt_b[j])
            gc = _chunk_cumsum(jnp.broadcast_to(g_log, (R, W)), tin)
            e_gc = jnp.exp(gc)
            kb = k * beta
            pre.append(dict(q=q, k=k, gc=gc, kb=kb, vb=v * beta, kbe=kb * e_gc, qe=q * e_gc))

        items = [(j, slice(i * P, (i + 1) * P)) for i in range(nb) for j in range(ns)]
        decs = []
        for j, sl in items:
            gc_p = pre[j]["gc"][sl]
            dlog = gc_p - jnp.transpose(gc_p)
            dlog = jnp.where(dif >= 0, dlog, neg_inf)
            decs.append(jnp.exp(jnp.where(dif <= tin_m, dlog, neg_inf)))
        negAs = [jnp.where(dif > 0, _dot_nt(pre[j]["kb"][sl], pre[j]["k"][sl]) * dec, 0.0) * -1.0
                 for (j, sl), dec in zip(items, decs)]
        qks = [_dot_nt(pre[j]["q"][sl], pre[j]["k"][sl]) * dec for (j, sl), dec in zip(items, decs)]
        tinvs = _tri_inverse(negAs, eye, masks)
        uws = [_dot(t, jnp.concatenate([pre[j]["vb"][sl], pre[j]["kbe"][sl]], axis=1))
               for t, (j, sl) in zip(tinvs, items)]
        steps = []
        for i in range(nb):
            for c in range(2):
                for j in range(ns):
                    it = i * ns + j
                    sl, uw, qk = items[it][1], uws[it], qks[it]
                    cs = slice(sl.start + c * L, sl.start + (c + 1) * L)
                    hs = slice(c * L, (c + 1) * L)
                    gc_c = pre[j]["gc"][cs]
                    gc_last = gc_c[L - 1:L, :]
                    k_dec = pre[j]["k"][cs] * jnp.exp(gc_last - gc_c)
                    u_c, w_c = uw[hs, 0:W], uw[hs, W:]
                    m_c = eye * jnp.exp(gc_last) - _dot_tn(k_dec, w_c)
                    n_st = len(steps)
                    lhs_scr[n_st] = jnp.concatenate(
                        [m_c, w_c, pre[j]["qe"][cs]], axis=0).astype(BF16)
                    u_scr[n_st] = u_c
                    qk_scr[n_st] = qk[hs].astype(BF16)
                    c_scr[n_st] = _dot_tn(k_dec, u_c)
                    steps.append(j)

        Ss = list(Ss)
        bigs = []
        for i, j in enumerate(steps):
            big = jnp.dot(lhs_scr[i], Ss[j].astype(BF16), preferred_element_type=F32)
            Ss[j] = big[0:W] + c_scr[i]
            bigs.append(big)
        outs = [[] for _ in range(ns)]
        for i, j in enumerate(steps):
            v_new = u_scr[i] - bigs[i][W:W + L]
            outs[j].append(bigs[i][W + L:]
                           + _dot(qk_scr[i], jnp.concatenate([v_new, v_new], axis=0)))

        for j in range(ns):
            sj = slice(j * W, (j + 1) * W)
            o = jnp.concatenate(outs[j], axis=0)
            o = o * lax.rsqrt(jnp.mean(o * o, axis=-1, keepdims=True) + NORM_EPS) * ng
            o_ref[pl.ds(s, R), sj] = (o * _silu(z_ref[pl.ds(s, R), sj].astype(F32))
                                      ).astype(o_ref.dtype)
        return (tuple(Ss), tuple(t[0] for t in raws), tuple(t[1] for t in raws),
                tuple(t[2] for t in raws))

    z8 = jnp.zeros((8, W), F32)
    init = (tuple(jnp.zeros((W, W), F32) for _ in range(ns)), (z8,) * ns, (z8,) * ns, (z8,) * ns)
    lax.fori_loop(0, nsb, body, init, unroll=2)


def _gdn(p, seq, conv_w, a_log, dt_bias, norm_g):
    n = p.shape[0]
    nbatch = n // seq
    ns = 4
    wide = ns * LANES
    ngrp = GDN_HEADS // ns
    assert COL_QKV % wide == 0 and COL_Z % wide == 0 and GDN_WIDTH % wide == 0
    qoff = COL_QKV // wide
    nb = 2 if seq % (4 * CHUNK) == 0 else 1

    def col(off):
        return pl.BlockSpec((seq, wide), lambda b, h, off=off: (b, off + h))

    def cw(off):
        return pl.BlockSpec((CONV_WIDTH, wide), lambda b, h, off=off: (0, off + h))

    one = pl.BlockSpec((1, LANES), lambda b, h: (0, 0))
    in_specs = [
        col(qoff), col(qoff + ngrp), col(qoff + 2 * ngrp), col(COL_Z // wide),
        pl.BlockSpec((seq, LANES), lambda b, h: (b, COL_AB // LANES)),
        cw(0), cw(ngrp), cw(2 * ngrp), one, one, one,
    ]
    n_steps = 2 * nb * ns
    return pl.pallas_call(
        functools.partial(_gdn_kernel, nb=nb, ns=ns),
        grid=(nbatch, ngrp),
        in_specs=in_specs,
        out_specs=pl.BlockSpec((seq, wide), lambda b, h: (b, h)),
        out_shape=jax.ShapeDtypeStruct((n, GDN_WIDTH), BF16),
        scratch_shapes=[
            pltpu.VMEM((n_steps, 2 * CHUNK + LANES, LANES), BF16),
            pltpu.VMEM((n_steps, CHUNK, LANES), F32),
            pltpu.VMEM((n_steps, CHUNK, LANES), BF16),
            pltpu.VMEM((n_steps, LANES, LANES), F32),
        ],
        compiler_params=pltpu.CompilerParams(
            dimension_semantics=("parallel", "parallel"), vmem_limit_bytes=VMEM_LIMIT),
        name="gdn",
    )(p, p, p, p, p, conv_w, conv_w, conv_w, a_log, dt_bias, norm_g)


def _mixout_kernel(x_ref, ya_ref, yb_ref, ga_ref, gb_ref, wpa_ref, wpb_ref, wout_ref, g2_ref,
                   *rest, with_router):
    if with_router:
        rt_ref, xo_ref, h_ref, comb_ref = rest
    else:
        xo_ref, h_ref = rest
    tm = x_ref.shape[0]
    ng = 2 if tm % 32 == 0 else 1
    grp = [slice(q * (tm // ng), (q + 1) * (tm // ng)) for q in range(ng)]
    yas = [_dot(ya_ref[r, :], wpa_ref[...]) for r in grp]
    ybs = [_dot(yb_ref[r, :], wpb_ref[...]) for r in grp]
    ys = [_sigmoid(ga_ref[r, :].astype(F32)) * ya + _sigmoid(gb_ref[r, :].astype(F32)) * yb
          for r, ya, yb in zip(grp, yas, ybs)]
    xns = [x_ref[r, :] + _dot(y, wout_ref[...]) for r, y in zip(grp, ys)]
    hs = []
    for r, xn in zip(grp, xns):
        xo_ref[r, :] = xn
        h = xn * lax.rsqrt(jnp.mean(xn * xn, axis=-1, keepdims=True) + NORM_EPS) * g2_ref[...]
        h_ref[r, :] = h.astype(h_ref.dtype)
        hs.append(h)
    if with_router:
        rt = rt_ref[...]
        r_hi = rt.astype(BF16)
        r_lo = (rt - r_hi.astype(F32)).astype(BF16)
        neg = jnp.float32(-jnp.inf)
        for r, h in zip(grp, hs):
            h_hi = h.astype(BF16)
            h_lo = (h - h_hi.astype(F32)).astype(BF16)
            logits = (jnp.dot(h_hi, r_hi, preferred_element_type=F32)
                      + jnp.dot(h_lo, r_hi, preferred_element_type=F32)
                      + jnp.dot(h_hi, r_lo, preferred_element_type=F32))
            lane = _iota(logits.shape, 1)
            logits = jnp.where(lane < N_EXPERTS, logits, neg)
            m1 = jnp.max(logits, axis=1, keepdims=True)
            i1 = jnp.min(jnp.where(logits == m1, lane, LANES), axis=1, keepdims=True)
            l2 = jnp.where(lane == i1, neg, logits)
            m2 = jnp.max(l2, axis=1, keepdims=True)
            i2 = jnp.min(jnp.where(l2 == m2, lane, LANES), axis=1, keepdims=True)
            e2 = jnp.exp(m2 - m1)
            g1 = 1.0 / (1.0 + e2)
            g2 = e2 / (1.0 + e2)
            comb_ref[r, :] = (jnp.where(lane == 0, i1.astype(F32), 0.0)
                              + jnp.where(lane == 1, i2.astype(F32), 0.0)
                              + jnp.where(lane == 2, g1, 0.0) + jnp.where(lane == 3, g2, 0.0))


def _mixout(x2, ya, yb, p, wpa, wpb, wout, g2, router, tm):
    n = x2.shape[0]
    with_router = router is not None
    row = lambda w: pl.BlockSpec((tm, w), lambda i: (i, 0))
    full = lambda a: pl.BlockSpec(a.shape, lambda i: (0, 0))
    in_specs = [
        row(D_MODEL), row(RWKV_WIDTH), row(GDN_WIDTH),
        pl.BlockSpec((tm, D_MODEL), lambda i: (i, COL_GATE // D_MODEL)),
        pl.BlockSpec((tm, D_MODEL), lambda i: (i, COL_GATE // D_MODEL + 1)),
        full(wpa), full(wpb), full(wout), full(g2),
    ]
    args = [x2, ya, yb, p, p, wpa, wpb, wout, g2]
    out_specs = [row(D_MODEL), row(D_MODEL)]
    out_shape = [jax.ShapeDtypeStruct((n, D_MODEL), F32),
                 jax.ShapeDtypeStruct((n, D_MODEL), F32 if with_router else BF16)]
    if with_router:
        in_specs.append(full(router))
        args.append(router)
        out_specs.append(row(LANES))
        out_shape.append(jax.ShapeDtypeStruct((n, LANES), F32))
    return pl.pallas_call(
        functools.partial(_mixout_kernel, with_router=with_router),
        grid=(n // tm,),
        in_specs=in_specs,
        out_specs=out_specs,
        out_shape=out_shape,
        compiler_params=pltpu.CompilerParams(
            dimension_semantics=("parallel",), vmem_limit_bytes=VMEM_LIMIT),
        name="mixout_router" if with_router else "mixout",
    )(*args)


def _rms_out(y, fg_ref):
    return y * lax.rsqrt(jnp.mean(y * y, axis=-1, keepdims=True) + NORM_EPS) * fg_ref[...]


def _gate_up(wg, wu, tf):
    parts = []
    for j in range(D_FF // tf):
        parts += [wg[..., j * tf:(j + 1) * tf], wu[..., j * tf:(j + 1) * tf]]
    return jnp.concatenate(parts, axis=-1).astype(BF16)


def _swiglu_chunk(h, wgu, wd):
    tf = wd.shape[0]
    gu = jnp.dot(h, wgu, preferred_element_type=F32)
    act = _silu(gu[:, 0:tf]) * gu[:, tf:]
    return jnp.dot(act.astype(BF16), wd, preferred_element_type=F32)


def _ffn_kernel(h_ref, x_ref, wgu_ref, wd_ref, *rest, final):
    fg_ref = rest[0] if final else None
    o_ref, acc_ref = rest[-2:]
    j = pl.program_id(1)

    @pl.when(j == 0)
    def _():
        acc_ref[...] = jnp.zeros_like(acc_ref)

    acc_ref[...] += _swiglu_chunk(h_ref[...], wgu_ref[...], wd_ref[...])

    @pl.when(j == pl.num_programs(1) - 1)
    def _():
        y = x_ref[...] + acc_ref[...]
        o_ref[...] = _rms_out(y, fg_ref) if final else y


def _ffn(h, x2, wgu, wd, final_g, tm, tf):
    n = x2.shape[0]
    final = final_g is not None
    row = lambda w: pl.BlockSpec((tm, w), lambda i, j: (i, 0))
    in_specs = [
        row(D_MODEL), row(D_MODEL),
        pl.BlockSpec((D_MODEL, 2 * tf), lambda i, j: (0, j)),
        pl.BlockSpec((tf, D_MODEL), lambda i, j: (j, 0)),
    ]
    args = [h, x2, wgu, wd]
    if final:
        in_specs.append(pl.BlockSpec((1, D_MODEL), lambda i, j: (0, 0)))
        args.append(final_g)
    return pl.pallas_call(
        functools.partial(_ffn_kernel, final=final),
        grid=(n // tm, D_FF // tf),
        in_specs=in_specs,
        out_specs=row(D_MODEL),
        out_shape=jax.ShapeDtypeStruct((n, D_MODEL), F32),
        scratch_shapes=[pltpu.VMEM((tm, D_MODEL), F32)],
        compiler_params=pltpu.CompilerParams(
            dimension_semantics=("parallel", "arbitrary"), vmem_limit_bytes=VMEM_LIMIT),
        name="ffn",
    )(*args)


MOE_ROWS = 512


def _plan_kernel(sel_ref, pos_ref, tile_ref, cnt_scr, *, tm, n_tiles):
    ph = pl.program_id(0)
    i = pl.program_id(1)
    sel = sel_ref[...]
    lane = _iota((tm, LANES), 1)
    e1 = jnp.sum(jnp.where(lane == 0, sel, 0.0), axis=1, keepdims=True).astype(jnp.int32)
    e2 = jnp.sum(jnp.where(lane == 1, sel, 0.0), axis=1, keepdims=True).astype(jnp.int32)
    oh1 = lane == e1
    oh2 = lane == e2
    both = jnp.where(oh1, 1.0, 0.0) + jnp.where(oh2, 1.0, 0.0)

    @pl.when((ph == 0) & (i == 0))
    def _():
        cnt_scr[...] = jnp.zeros_like(cnt_scr)

    @pl.when(ph == 0)
    def _():
        cnt_scr[0:1, :] += jnp.sum(both, axis=0, keepdims=True)

    @pl.when((ph == 1) & (i == 0))
    def _():
        cnt = cnt_scr[0:1, :]
        padded = jnp.floor((cnt + (MOE_ROWS - 1)) * (1.0 / MOE_ROWS)) * MOE_ROWS
        rl = _iota((LANES, LANES), 0)
        cl = _iota((LANES, LANES), 1)
        upper = jnp.where(rl < cl, 1.0, 0.0)
        starts = _dot_hi(jnp.broadcast_to(padded, (8, LANES)), upper)[0:1, :]
        ends = starts + padded
        cnt_scr[0:1, :] = starts
        tl = _iota((n_tiles, LANES), 1)
        t0 = (_iota((n_tiles, LANES), 0) * MOE_ROWS).astype(F32)
        lane_ok = tl < N_EXPERTS
        owner = jnp.sum(jnp.where(lane_ok, jnp.where(t0 >= ends, 1.0, 0.0), 0.0),
                        axis=1, keepdims=True)
        used = jnp.sum(jnp.where(lane_ok, padded, 0.0), axis=1, keepdims=True)
        valid = t0[:, 0:1] < used
        last_e = jnp.max(jnp.where(lane_ok, jnp.where(padded > 0.0, tl.astype(F32), 0.0), 0.0),
                         axis=1, keepdims=True)
        ex = jnp.where(valid, jnp.minimum(owner, N_EXPERTS - 1.0), last_e)
        tile_ref[...] = (jnp.where(tl == 0, ex, 0.0)
                         + jnp.where(tl == 1, jnp.where(valid, 1.0, 0.0), 0.0)).astype(jnp.int32)

    @pl.when(ph == 1)
    def _():
        rr = _iota((tm, tm), 0)
        cc = _iota((tm, tm), 1)
        before = jnp.where(cc < rr, 1.0, 0.0).astype(BF16)
        base = cnt_scr[0:1, :] + jnp.dot(before, both.astype(BF16), preferred_element_type=F32)
        p1 = jnp.sum(jnp.where(oh1, base, 0.0), axis=1, keepdims=True)
        p2 = jnp.sum(jnp.where(oh2, base, 0.0), axis=1, keepdims=True)
        pos_ref[...] = (jnp.where(lane == 0, p1, 0.0) + jnp.where(lane == 1, p2, 0.0)
                        ).astype(jnp.int32)
        cnt_scr[0:1, :] += jnp.sum(both, axis=0, keepdims=True)


def _plan(sel, tm, n_tiles):
    n = sel.shape[0]
    return pl.pallas_call(
        functools.partial(_plan_kernel, tm=tm, n_tiles=n_tiles),
        grid=(2, n // tm),
        in_specs=[pl.BlockSpec((tm, LANES), lambda ph, i: (i, 0))],
        out_specs=[pl.BlockSpec((tm, LANES), lambda ph, i: (i * ph, 0)),
                   pl.BlockSpec((n_tiles, LANES), lambda ph, i: (0, 0))],
        out_shape=[jax.ShapeDtypeStruct((n, LANES), jnp.int32),
                   jax.ShapeDtypeStruct((n_tiles, LANES), jnp.int32)],
        scratch_shapes=[pltpu.VMEM((8, LANES), F32)],
        compiler_params=pltpu.CompilerParams(
            dimension_semantics=("arbitrary", "arbitrary"), vmem_limit_bytes=VMEM_LIMIT),
        name="moe_plan",
    )(sel)


def _row_copy(src_ref, src_row, dst_ref, dst_row, sem):
    return pltpu.make_async_copy(src_ref.at[pl.ds(src_row, 1)], dst_ref.at[pl.ds(dst_row, 1)], sem)


def _dispatch_kernel(pos_ref, h_ref, xs_in_ref, xs_ref, sem, *, tm):
    del xs_in_ref

    def start(r, c):
        _row_copy(h_ref, r, xs_ref, pos_ref[2 * r], sem).start(priority=0)
        _row_copy(h_ref, r, xs_ref, pos_ref[2 * r + 1], sem).start(priority=1)
        return c

    def wait(r, c):
        _row_copy(h_ref, 0, xs_ref, 0, sem).wait()
        _row_copy(h_ref, 0, xs_ref, 0, sem).wait()
        return c

    for r in range(tm):
        start(r, 0)
    lax.fori_loop(0, tm, wait, 0, unroll=8)


def _dispatch(pos_flat, h, n_rows, tm):
    n = h.shape[0]
    xs0 = jnp.zeros((n_rows, D_MODEL), F32)
    return pl.pallas_call(
        functools.partial(_dispatch_kernel, tm=tm),
        grid=(n // tm,),
        in_specs=[
            pl.BlockSpec((2 * tm,), lambda i: (i,), memory_space=pltpu.SMEM),
            pl.BlockSpec((tm, D_MODEL), lambda i: (i, 0)),
            pl.BlockSpec(memory_space=pl.ANY),
        ],
        out_specs=pl.BlockSpec(memory_space=pl.ANY),
        out_shape=jax.ShapeDtypeStruct((n_rows, D_MODEL), F32),
        scratch_shapes=[pltpu.SemaphoreType.DMA(())],
        input_output_aliases={2: 0},
        compiler_params=pltpu.CompilerParams(
            dimension_semantics=("arbitrary",), vmem_limit_bytes=VMEM_LIMIT),
        name="moe_dispatch",
    )(pos_flat, h, xs0)


def _experts_kernel(te_ref, tv_ref, xs_ref, wgu_ref, wd_ref, ys_ref, xb_scr, acc_scr):
    t = pl.program_id(0)
    j = pl.program_id(1)
    del te_ref

    @pl.when(j == 0)
    def _():
        xb_scr[...] = xs_ref[...].astype(BF16)
        acc_scr[...] = jnp.zeros_like(acc_scr)

    @pl.when(tv_ref[t] == 1)
    def _():
        acc_scr[...] += _swiglu_chunk(xb_scr[...], wgu_ref[0], wd_ref[0])

    @pl.when(j == pl.num_programs(1) - 1)
    def _():
        ys_ref[...] = acc_scr[...]


def _experts(tile_e, tile_v, xs, wgu, wd, tf):
    n_rows = xs.shape[0]
    grid_spec = pltpu.PrefetchScalarGridSpec(
        num_scalar_prefetch=2,
        grid=(n_rows // MOE_ROWS, D_FF // tf),
        in_specs=[
            pl.BlockSpec((MOE_ROWS, D_MODEL), lambda t, j, te, tv: (t, 0)),
            pl.BlockSpec((1, D_MODEL, 2 * tf), lambda t, j, te, tv: (te[t], 0, j)),
            pl.BlockSpec((1, tf, D_MODEL), lambda t, j, te, tv: (te[t], j, 0)),
        ],
        out_specs=pl.BlockSpec((MOE_ROWS, D_MODEL), lambda t, j, te, tv: (t, 0)),
        scratch_shapes=[pltpu.VMEM((MOE_ROWS, D_MODEL), BF16), pltpu.VMEM((MOE_ROWS, D_MODEL), F32)],
    )
    return pl.pallas_call(
        _experts_kernel,
        grid_spec=grid_spec,
        out_shape=jax.ShapeDtypeStruct((n_rows, D_MODEL), F32),
        compiler_params=pltpu.CompilerParams(
            dimension_semantics=("parallel", "arbitrary"), vmem_limit_bytes=VMEM_LIMIT),
        name="moe_experts",
    )(tile_e, tile_v, xs, wgu, wd)


def _combine_kernel(pos_ref, x_ref, sel_ref, *rest, tm, final):
    fg_ref = rest[0] if final else None
    ys_ref, o_ref, buf, sem = rest[-4:]

    def start(r, c):
        _row_copy(ys_ref, pos_ref[2 * r], buf.at[0], r, sem).start(priority=0)
        _row_copy(ys_ref, pos_ref[2 * r + 1], buf.at[1], r, sem).start(priority=1)
        return c

    def wait(r, c):
        _row_copy(ys_ref, 0, buf.at[0], 0, sem).wait()
        _row_copy(ys_ref, 0, buf.at[1], 0, sem).wait()
        return c

    for r in range(tm):
        start(r, 0)
    lax.fori_loop(0, tm, wait, 0, unroll=8)
    sel = sel_ref[...]
    lane = _iota(sel.shape, 1)
    g1 = jnp.sum(jnp.where(lane == 2, sel, 0.0), axis=1, keepdims=True)
    g2 = jnp.sum(jnp.where(lane == 3, sel, 0.0), axis=1, keepdims=True)
    y = x_ref[...] + g1 * buf[0] + g2 * buf[1]
    o_ref[...] = _rms_out(y, fg_ref) if final else y


def _combine(pos_flat, x2, sel, ys, final_g, tm):
    n = x2.shape[0]
    final = final_g is not None
    in_specs = [
        pl.BlockSpec((2 * tm,), lambda i: (i,), memory_space=pltpu.SMEM),
        pl.BlockSpec((tm, D_MODEL), lambda i: (i, 0)),
        pl.BlockSpec((tm, LANES), lambda i: (i, 0)),
    ]
    args = [pos_flat, x2, sel]
    if final:
        in_specs.append(pl.BlockSpec((1, D_MODEL), lambda i: (0, 0)))
        args.append(final_g)
    in_specs.append(pl.BlockSpec(memory_space=pl.ANY))
    args.append(ys)
    return pl.pallas_call(
        functools.partial(_combine_kernel, tm=tm, final=final),
        grid=(n // tm,),
        in_specs=in_specs,
        out_specs=pl.BlockSpec((tm, D_MODEL), lambda i: (i, 0)),
        out_shape=jax.ShapeDtypeStruct((n, D_MODEL), F32),
        scratch_shapes=[pltpu.VMEM((2, tm, D_MODEL), F32), pltpu.SemaphoreType.DMA(())],
        compiler_params=pltpu.CompilerParams(
            dimension_semantics=("arbitrary",), vmem_limit_bytes=VMEM_LIMIT),
        name="moe_combine",
    )(*args)


def _moe(h, x2, sel, wgu, wd, final_g, tm, tf):
    n = x2.shape[0]
    n_tiles = pl.cdiv(2 * n, MOE_ROWS) + N_EXPERTS
    pos, tiles = _plan(sel, tm, n_tiles)
    pos_flat = pos[:, 0:2].reshape(2 * n)
    xs = _dispatch(pos_flat, h, n_tiles * MOE_ROWS, tm)
    ys = _experts(tiles[:, 0], tiles[:, 1], xs, wgu, wd, tf)
    return _combine(pos_flat, x2, sel, ys, final_g, tm)


def _tile(n, pref):
    t = min(pref, n)
    while n % t:
        t //= 2
    return t


def _pad_lanes(a, width=LANES):
    return jnp.pad(a, ((0, 0), (0, width - a.shape[-1])))


def kernel(x, norm1_g, w_in, tshift_mu, w0, w_decay_up, a0, w_iclr_up, w_gate_up, k_k, k_a, r_k,
           lnx_g, lnx_b, w_pa, conv_w, a_log, dt_bias, gdn_norm_g, w_pb, w_out, norm2_g,
           ffn_w_gate, ffn_w_up, ffn_w_down, moe_router, moe_w_gate, moe_w_up, moe_w_down, final_g):
    nb, seq, d = x.shape
    n = nb * seq
    x2 = x.reshape(n, d)
    tm_in = _tile(n, 1024)
    tm = _tile(n, 512)
    rw = 1792
    gq = 3 * GDN_WIDTH
    for layer in range(DEPTH):
        wi = w_in[layer]
        ab = wi[:, rw + gq:rw + gq + 2 * GDN_HEADS]
        zc = wi[:, rw + gq + 2 * GDN_HEADS:rw + gq + 2 * GDN_HEADS + GDN_WIDTH]
        gate = wi[:, rw + gq + 2 * GDN_HEADS + GDN_WIDTH:]
        w_cat = jnp.concatenate(
            [wi[:, :rw], ab, jnp.zeros((d, COL_QKV - COL_AB - 2 * GDN_HEADS), F32),
             wi[:, rw:rw + gq], zc, gate],
            axis=1).astype(BF16)
        p = _inproj(x2, norm1_g[layer][None, :], w_cat, tm_in, 1536)
        one = lambda a: a[layer][None, :]
        ya = _rwkv(p, seq, one(tshift_mu), one(w0), one(a0), one(k_k), one(k_a), one(r_k),
                   one(lnx_g), one(lnx_b), w_decay_up[layer].astype(BF16),
                   w_iclr_up[layer].astype(BF16), w_gate_up[layer].astype(BF16))
        yb = _gdn(p, seq, conv_w[layer], _pad_lanes(one(a_log)), _pad_lanes(one(dt_bias)),
                  one(gdn_norm_g))
        j = layer // 2
        routed = layer % 2 == 1
        router = _pad_lanes(moe_router[j]) if routed else None
        res = _mixout(x2, ya, yb, p, w_pa[layer].astype(BF16), w_pb[layer].astype(BF16),
                      w_out[layer].astype(BF16), one(norm2_g), router, tm)
        fg = final_g[None, :] if layer == DEPTH - 1 else None
        if routed:
            x2, h, sel = res
            x2 = _moe(h, x2, sel, _gate_up(moe_w_gate[j], moe_w_up[j], FF_CHUNK),
                      moe_w_down[j].astype(BF16), fg, tm, FF_CHUNK)
        else:
            x2, h = res
            x2 = _ffn(h, x2, _gate_up(ffn_w_gate[j], ffn_w_up[j], FF_CHUNK),
                      ffn_w_down[j].astype(BF16), fg, tm, FF_CHUNK)
    return x2.reshape(nb, seq, d)
```

```python
import functools

import jax
import jax.numpy as jnp
from jax import lax
from jax.experimental import pallas as pl
from jax.experimental.pallas import tpu as pltpu

F32 = jnp.float32
BF16 = jnp.bfloat16
HI = lax.Precision.HIGHEST

D_MODEL = 1024
DEPTH = 2
RWKV_HEADS = 8
RWKV_HEAD_DIM = 64
RWKV_WIDTH = 512
DECAY_LORA = 64
ICLR_LORA = 64
GATE_LORA = 128
RWKV_GN_EPS = 64e-5
GDN_HEADS = 4
GDN_HEAD_DIM = 128
GDN_WIDTH = 512
CONV_WIDTH = 4
CHUNK = 64
D_FF = 2816
N_EXPERTS = 8
NORM_EPS = 1e-6
L2_EPS = 1e-6

LANES = 128
P_COLS = 6144
COL_AB = 1792
COL_QKV = 2048
COL_Z = 3584
COL_GATE = 4096
VMEM_LIMIT = 56 * 1024 * 1024
FF_CHUNK = D_FF // 2


def _dot(a, b):
    return jnp.dot(a.astype(BF16), b.astype(BF16), preferred_element_type=F32)


def _dot_nt(a, b):
    return lax.dot_general(a.astype(BF16), b.astype(BF16), (((1,), (1,)), ((), ())),
                           preferred_element_type=F32)


def _dot_tn(a, b):
    return lax.dot_general(a.astype(BF16), b.astype(BF16), (((0,), (0,)), ((), ())),
                           preferred_element_type=F32)


def _dot_hi(a, b):
    return jnp.dot(a, b, preferred_element_type=F32, precision=HI)


def _softplus(x):
    return jnp.maximum(x, 0.0) + jnp.log(1.0 + jnp.exp(-jnp.abs(x)))


def _sigmoid(x):
    return 1.0 / (1.0 + jnp.exp(-x))


def _silu(x):
    return x * _sigmoid(x)


def _iota(shape, dim):
    return lax.broadcasted_iota(jnp.int32, shape, dim)


def _inproj_kernel(x_ref, g_ref, w_ref, p_ref, h_scr):
    @pl.when(pl.program_id(1) == 0)
    def _():
        x = x_ref[...]
        h = x * lax.rsqrt(jnp.mean(x * x, axis=-1, keepdims=True) + NORM_EPS) * g_ref[...]
        h_scr[...] = h.astype(BF16)

    p_ref[...] = jnp.dot(h_scr[...], w_ref[...], preferred_element_type=F32).astype(p_ref.dtype)


def _inproj(x2, g, w_cat, tm, tn):
    n = x2.shape[0]
    return pl.pallas_call(
        _inproj_kernel,
        grid=(n // tm, P_COLS // tn),
        in_specs=[
            pl.BlockSpec((tm, D_MODEL), lambda i, j: (i, 0)),
            pl.BlockSpec((1, D_MODEL), lambda i, j: (0, 0)),
            pl.BlockSpec((D_MODEL, tn), lambda i, j: (0, j)),
        ],
        out_specs=pl.BlockSpec((tm, tn), lambda i, j: (i, j)),
        out_shape=jax.ShapeDtypeStruct((n, P_COLS), BF16),
        scratch_shapes=[pltpu.VMEM((tm, D_MODEL), BF16)],
        compiler_params=pltpu.CompilerParams(
            dimension_semantics=("parallel", "arbitrary"), vmem_limit_bytes=VMEM_LIMIT),
        name="inproj",
    )(x2, g, w_cat)


def _split_dot(x, m):
    hi = x.astype(BF16)
    lo = (x - hi.astype(F32)).astype(BF16)
    return (jnp.dot(hi, m, preferred_element_type=F32) + jnp.dot(lo, m, preferred_element_type=F32))


def _chunk_cumsum(x, tin):
    y = x
    sh = 1
    while sh < CHUNK:
        y = y + jnp.where(tin >= sh, pltpu.roll(y, sh, axis=0), 0.0)
        sh *= 2
    return y


def _level_masks(rm, lm):
    out = []
    for k in range(6):
        x = rm >> k
        y = lm >> k
        out.append((((x ^ y) + 2 * (1 - (x & 1))) == 1).astype(F32))
    return out


def _tri_inverse(mats, eye, masks):
    ts = [eye + a * masks[0] for a in mats]
    for k in range(1, 6):
        xs = [_dot(a * masks[k], t) for a, t in zip(mats, ts)]
        ts = [t + _dot(t, x) for t, x in zip(ts, xs)]
    return ts


def _rwkv_kernel(r_ref, k_ref, v_ref, lo_ref, mur_ref, muk_ref, muv_ref, mulo_ref,
                 w0_ref, a0_ref, kk_ref, ka_ref, rk_ref, lg_ref, lb_ref,
                 wd_ref, wa_ref, wg_ref, bd_ref, o_ref,
                 lhs_scr, u0_scr, qb_scr, ov_scr, c_scr, pre_scr, *, nb, ns):
    L = CHUNK
    W = LANES
    R = nb * L
    seq = r_ref.shape[0]
    nsb = seq // R

    row_r = _iota((R, W), 0)
    row_r2 = _iota((R, 2 * W), 0)
    row_c = _iota((L, W), 0)
    lane = _iota((L, W), 1)
    h1 = lane < 64
    rm = _iota((W, W), 0)
    lm = _iota((W, W), 1)
    tri_mask = (lm & 63) < (rm & 63) + (rm >> 6)
    blk_mask = (rm >> 6) == (lm >> 6)
    eye = (rm == lm).astype(F32)
    masks = _level_masks(rm, lm)

    mur, muk, muv, mulo = mur_ref[...], muk_ref[...], muv_ref[...], mulo_ref[...]
    w0, a0, k_k, k_a, r_k = w0_ref[...], a0_ref[...], kk_ref[...], ka_ref[...], rk_ref[...]
    ln_g, ln_b = lg_ref[...], lb_ref[...]
    wd, wa, wg = wd_ref[...], wa_ref[...], wg_ref[...]

    def shift(x, prev_row, mu, rowi):
        xp = pltpu.roll(x, 1, axis=0)
        xp = jnp.where(rowi == 0, prev_row, xp)
        return x + (xp - x) * mu

    def last_row_before(ref, s, cols):
        rows = ref[pl.ds(pl.multiple_of(jnp.maximum(s - 16, 0), 16), 16), cols].astype(F32)
        return rows[15:16, :] * (s > 0).astype(F32)

    keys = ("v", "k2", "bh", "cl", "g", "bonus", "rt", "at", "bt", "kt")

    class Stored:
        def __init__(self, par, j):
            self.par, self.j = par, j

        def __getitem__(self, key):
            par, j, q = self.par, self.j, keys.index(key)

            class Rows:
                def __getitem__(self, rows):
                    return pre_scr[par, j, q, rows]
            return Rows()

    def prologue_pieces(sb, par):
        sh = {}

        def shared():
            s = pl.multiple_of(sb * R, R)
            lo_raw = lo_ref[pl.ds(s, R), :].astype(F32)
            lo = shift(lo_raw, last_row_before(lo_ref, s, slice(None)), mulo, row_r2)
            wl = lo[:, 0:DECAY_LORA]
            al = lo[:, DECAY_LORA:DECAY_LORA + ICLR_LORA]
            gl = lo[:, DECAY_LORA + ICLR_LORA:]
            sh["z"] = w0 + _dot(jnp.tanh(wl), wd)
            sh["a"] = _sigmoid(a0 + _dot(al, wa))
            sh["g"] = _dot(_sigmoid(gl), wg)
            for j in range(ns):
                sj = slice(j * W, (j + 1) * W)
                for name, ref, mu in (("r", r_ref, mur), ("k", k_ref, muk), ("v", v_ref, muv)):
                    sh[name, j] = shift(ref[pl.ds(s, R), sj].astype(F32), last_row_before(ref, s, sj),
                                        mu[:, sj], row_r)

        def piece(j, c):
            sj = slice(j * W, (j + 1) * W)
            r, k, v = sh["r", j][c], sh["k", j][c], sh["v", j][c]
            w_log = -_softplus(-sh["z"][c, sj]) - 0.5
            lw = -jnp.exp(w_log)
            a = sh["a"][c, sj]
            kk = k * k_k[:, sj]
            kk = kk * lax.rsqrt(_split_dot(kk * kk, bd_ref[...]) + L2_EPS)
            k2 = k * (1.0 + (a - 1.0) * k_a[:, sj])
            bh = kk * a
            cl = _chunk_cumsum(lw, row_c)
            e_neg = jnp.exp(-cl)
            vals = dict(
                v=v, k2=k2, bh=bh, cl=cl, g=sh["g"][c, sj],
                bonus=_split_dot(r * k2 * r_k[:, sj], bd_ref[...]) * v,
                rt=r * jnp.exp(cl), at=-kk * jnp.exp(cl - lw), bt=bh * e_neg, kt=k2 * e_neg)
            for q, key in enumerate(keys):
                pre_scr[par, j, q, c] = vals[key]

        return [shared] + [functools.partial(piece, j, slice(i * L, (i + 1) * L))
                           for i in range(nb) for j in range(ns)]

    def run(par, Hs, sb, fillers):
        pre = [Stored(par, j) for j in range(ns)]

        def fill():
            if fillers:
                fillers.pop(0)()

        s = pl.multiple_of(sb * R, R)
        items = [(j, slice(i * L, (i + 1) * L)) for i in range(nb) for j in range(ns)]
        v_of = lambda it: pre[it[0]]["v"][it[1]]
        rts = [pre[j]["rt"][c] for j, c in items]
        ats = [pre[j]["at"][c] for j, c in items]
        zero = jnp.zeros((L, W), F32)
        m1s, m2s = [], []
        for (j, c), rt, at in zip(items, rts, ats):
            bt, kt = pre[j]["bt"][c], pre[j]["kt"][c]
            lhs1 = jnp.concatenate([jnp.where(h1, at, zero), jnp.where(h1, rt, zero)], axis=0)
            lhs2 = jnp.concatenate([jnp.where(h1, zero, at), jnp.where(h1, zero, rt)], axis=0)
            m1s.append(jnp.where(tri_mask, _dot_nt(lhs1, jnp.concatenate([bt, kt], axis=0)), 0.0))
            m2s.append(jnp.where(tri_mask, _dot_nt(lhs2, jnp.concatenate([kt, bt], axis=0)), 0.0))
        fill()
        a_bds = [jnp.concatenate([jnp.where(h1, m1[0:L], 0.0), jnp.where(h1, 0.0, m2[0:L])], axis=0)
                 for m1, m2 in zip(m1s, m2s)]
        tinvs = _tri_inverse(a_bds, eye, masks)
        avs, o_vs = [], []
        for it, m1, m2 in zip(items, m1s, m2s):
            vc = v_of(it)
            ak_m = jnp.concatenate(
                [jnp.where(h1, 0.0, m1[0:L]), jnp.where(h1, m2[0:L], 0.0)], axis=0)
            avs.append(_dot(ak_m, jnp.concatenate([vc, vc], axis=0)))
            qk = jnp.where(h1, m2[L:], m1[L:])
            vx = jnp.concatenate([jnp.where(h1, 0.0, vc), jnp.where(h1, vc, 0.0)], axis=0)
            o_vs.append(_dot(qk, vx))
        qbs = [jnp.where(h1, m1[L:], m2[L:]) for m1, m2 in zip(m1s, m2s)]
        u0s = [_dot(t, av) for t, av in zip(tinvs, avs)]
        tas = [jnp.where(blk_mask, _dot(t, jnp.concatenate([at, at], axis=0)), 0.0)
               for t, at in zip(tinvs, ats)]
        for i, ((j, c), ta, u0) in enumerate(zip(items, tas, u0s)):
            cl_i = pre[j]["cl"][c]
            cl_last = cl_i[L - 1:L, :]
            e_end = jnp.exp(cl_last - cl_i)
            b_end = pre[j]["bh"][c] * e_end
            b_st = jnp.concatenate([jnp.where(h1, b_end, 0.0), jnp.where(h1, 0.0, b_end)], axis=0)
            u0_bd = jnp.where(blk_mask, u0, 0.0)
            m_t = eye * jnp.exp(cl_last) + _dot_tn(b_st, ta)
            lhs_scr[i] = jnp.concatenate([m_t, ta, rts[i]], axis=0).astype(BF16)
            u0_scr[i] = u0_bd
            qb_scr[i] = qbs[i].astype(BF16)
            ov_scr[i] = o_vs[i]
            c_scr[i] = (_dot_tn(b_st, u0_bd)
                        + jnp.where(blk_mask, _dot_tn(pre[j]["k2"][c] * e_end, v_of((j, c))), 0.0))

        Hs = list(Hs)
        bigs = []
        for i, (j, c) in enumerate(items):
            big = jnp.dot(lhs_scr[i], Hs[j].astype(BF16), preferred_element_type=F32)
            Hs[j] = big[0:W] + c_scr[i]
            bigs.append(big)
            fill()
        outs = [[] for _ in range(ns)]
        for i, (j, c) in enumerate(items):
            u_bd = jnp.where(blk_mask, bigs[i][W:2 * W], 0.0) + u0_scr[i]
            outs[j].append(bigs[i][2 * W:] + _dot(qb_scr[i], u_bd) + ov_scr[i])

        for j in range(ns):
            sj = slice(j * W, (j + 1) * W)
            o = jnp.concatenate(outs[j], axis=0)
            mean = _split_dot(o, bd_ref[...]) * (1.0 / 64.0)
            oc = o - mean
            var = _split_dot(oc * oc, bd_ref[...]) * (1.0 / 64.0)
            y = (oc * lax.rsqrt(var + RWKV_GN_EPS) * ln_g[:, sj] + ln_b[:, sj]
                 + pre[j]["bonus"][slice(0, R)])
            o_ref[pl.ds(s, R), sj] = (y * pre[j]["g"][slice(0, R)]).astype(o_ref.dtype)
        while fillers:
            fill()
        return tuple(Hs)

    for thunk in prologue_pieces(0, 0):
        thunk()

    def body(t, Hs):
        Hs = run(0, Hs, 2 * t, prologue_pieces(2 * t + 1, 1))
        return run(1, Hs, 2 * t + 1, prologue_pieces(jnp.minimum(2 * t + 2, nsb - 1), 0))

    lax.fori_loop(0, nsb // 2, body, tuple(jnp.zeros((W, W), F32) for _ in range(ns)))


def _rwkv(p, seq, mu, w0, a0, k_k, k_a, r_k, ln_g, ln_b, wd, wa, wg):
    n = p.shape[0]
    nbatch = n // seq
    ns = 4
    wide = ns * LANES
    ngrp = RWKV_WIDTH // wide
    nb = 4 if seq % (8 * CHUNK) == 0 else 1

    def col(off):
        return pl.BlockSpec((seq, wide), lambda b, q, off=off: (b, off + q))

    def par(off=0):
        return pl.BlockSpec((1, wide), lambda b, q, off=off: (0, off + q))

    in_specs = [
        col(0), col(ngrp), col(2 * ngrp),
        pl.BlockSpec((seq, 2 * LANES), lambda b, q: (b, 3 * RWKV_WIDTH // (2 * LANES))),
        par(0), par(ngrp), par(2 * ngrp),
        pl.BlockSpec((1, 2 * LANES), lambda b, q: (0, 3 * RWKV_WIDTH // (2 * LANES))),
        par(), par(), par(), par(), par(), par(), par(),
        pl.BlockSpec((DECAY_LORA, wide), lambda b, q: (0, q)),
        pl.BlockSpec((ICLR_LORA, wide), lambda b, q: (0, q)),
        pl.BlockSpec((GATE_LORA, wide), lambda b, q: (0, q)),
        pl.BlockSpec((LANES, LANES), lambda b, q: (0, 0)),
    ]
    head_id = jnp.arange(LANES) // RWKV_HEAD_DIM
    bd_ones = (head_id[:, None] == head_id[None, :]).astype(BF16)
    return pl.pallas_call(
        functools.partial(_rwkv_kernel, nb=nb, ns=ns),
        grid=(nbatch, ngrp),
        in_specs=in_specs,
        out_specs=pl.BlockSpec((seq, wide), lambda b, q: (b, q)),
        out_shape=jax.ShapeDtypeStruct((n, RWKV_WIDTH), BF16),
        scratch_shapes=[
            pltpu.VMEM((nb * ns, 2 * LANES + CHUNK, LANES), BF16),
            pltpu.VMEM((nb * ns, LANES, LANES), F32),
            pltpu.VMEM((nb * ns, CHUNK, LANES), BF16),
            pltpu.VMEM((nb * ns, CHUNK, LANES), F32),
            pltpu.VMEM((nb * ns, LANES, LANES), F32),
            pltpu.VMEM((2, ns, 10, nb * CHUNK, LANES), F32),
        ],
        compiler_params=pltpu.CompilerParams(
            dimension_semantics=("parallel", "parallel"), vmem_limit_bytes=VMEM_LIMIT),
        name="rwkv",
    )(p, p, p, p, mu, mu, mu, mu, w0, a0, k_k, k_a, r_k, ln_g, ln_b, wd, wa, wg, bd_ones)


def _gdn_kernel(q_ref, k_ref, v_ref, z_ref, ab_ref, cwq_ref, cwk_ref, cwv_ref,
                alog_ref, dtb_ref, ng_ref, o_ref, lhs_scr, u_scr, qk_scr, c_scr, *, nb, ns):
    L = CHUNK
    W = LANES
    P = 2 * L
    R = nb * P
    seq = q_ref.shape[0]
    nsb = seq // R
    hd0 = pl.program_id(1) * ns

    row_r = _iota((R, W), 0)
    lane_r = _iota((R, W), 1)
    tin = row_r & (L - 1)
    row8 = _iota((8, W), 0)
    rm = _iota((W, W), 0)
    lm = _iota((W, W), 1)
    dif = rm - lm
    tin_m = rm & (L - 1)
    eye = (rm == lm).astype(F32)
    masks = _level_masks(rm, lm)

    lane1 = _iota((1, W), 1)
    pick = lambda ref, j: jnp.sum(jnp.where(lane1 == hd0 + j, ref[...], 0.0), axis=1, keepdims=True)
    neg_a = [-jnp.exp(pick(alog_ref, j)) for j in range(ns)]
    dt_b = [pick(dtb_ref, j) for j in range(ns)]
    cwq, cwk, cwv = cwq_ref[...], cwk_ref[...], cwv_ref[...]
    ng = ng_ref[...]
    scale = GDN_HEAD_DIM ** -0.5
    neg_inf = jnp.float32(-jnp.inf)

    def conv(x, prev8, cw):
        acc = x * cw[CONV_WIDTH - 1:CONV_WIDTH, :]
        for j in range(1, CONV_WIDTH):
            xs = pltpu.roll(x, j, axis=0)
            ps = pltpu.roll(prev8, j, axis=0)
            head = jnp.where(row8 < j, ps, xs[0:8])
            xs = jnp.concatenate([head, xs[8:]], axis=0)
            acc = acc + xs * cw[CONV_WIDTH - 1 - j:CONV_WIDTH - j, :]
        return _silu(acc)

    def l2n(x):
        return x * lax.rsqrt(jnp.sum(x * x, axis=-1, keepdims=True) + L2_EPS)

    def body(sb, carry):
        Ss, pqs, pks, pvs = carry
        s = pl.multiple_of(sb * R, R)
        ab = ab_ref[pl.ds(s, R), :].astype(F32)

        pre, raws = [], []
        for j in range(ns):
            sj = slice(j * W, (j + 1) * W)
            hd = hd0 + j
            q_raw = q_ref[pl.ds(s, R), sj].astype(F32)
            k_raw = k_ref[pl.ds(s, R), sj].astype(F32)
            v_raw = v_ref[pl.ds(s, R), sj].astype(F32)
            raws.append((q_raw[R - 8:, :], k_raw[R - 8:, :], v_raw[R - 8:, :]))
            q = l2n(conv(q_raw, pqs[j], cwq[:, sj])) * scale
            k = l2n(conv(k_raw, pks[j], cwk[:, sj]))
            v = conv(v_raw, pvs[j], cwv[:, sj])
            a_in = jnp.sum(jnp.where(lane_r == hd, ab, 0.0), axis=1, keepdims=True)
            b_in = jnp.sum(jnp.where(lane_r == hd + GDN_HEADS, ab, 0.0), axis=1, keepdims=True)
            beta = _sigmoid(b_in)
            g_log = neg_a[j] * _softplus(a_in + dt_b[j])
            gc = _chunk_cumsum(jnp.broadcast_to(g_log, (R, W)), tin)
            e_gc = jnp.exp(gc)
            kb = k * beta
            pre.append(dict(q=q, k=k, gc=gc, kb=kb, vb=v * beta, kbe=kb * e_gc, qe=q * e_gc))

        items = [(j, slice(i * P, (i + 1) * P)) for i in range(nb) for j in range(ns)]
        decs = []
        for j, sl in items:
            gc_p = pre[j]["gc"][sl]
            dlog = gc_p - jnp.transpose(gc_p)
            dlog = jnp.where(dif >= 0, dlog, neg_inf)
            decs.append(jnp.exp(jnp.where(dif <= tin_m, dlog, neg_inf)))
        negAs = [jnp.where(dif > 0, _dot_nt(pre[j]["kb"][sl], pre[j]["k"][sl]) * dec, 0.0) * -1.0
                 for (j, sl), dec in zip(items, decs)]
        qks = [_dot_nt(pre[j]["q"][sl], pre[j]["k"][sl]) * dec for (j, sl), dec in zip(items, decs)]
        tinvs = _tri_inverse(negAs, eye, masks)
        uws = [_dot(t, jnp.concatenate([pre[j]["vb"][sl], pre[j]["kbe"][sl]], axis=1))
               for t, (j, sl) in zip(tinvs, items)]
        steps = []
        for i in range(nb):
            for c in range(2):
                for j in range(ns):
                    it = i * ns + j
                    sl, uw, qk = items[it][1], uws[it], qks[it]
                    cs = slice(sl.start + c * L, sl.start + (c + 1) * L)
                    hs = slice(c * L, (c + 1) * L)
                    gc_c = pre[j]["gc"][cs]
                    gc_last = gc_c[L - 1:L, :]
                    k_dec = pre[j]["k"][cs] * jnp.exp(gc_last - gc_c)
                    u_c, w_c = uw[hs, 0:W], uw[hs, W:]
                    m_c = eye * jnp.exp(gc_last) - _dot_tn(k_dec, w_c)
                    n_st = len(steps)
                    lhs_scr[n_st] = jnp.concatenate(
                        [m_c, w_c, pre[j]["qe"][cs]], axis=0).astype(BF16)
                    u_scr[n_st] = u_c
                    qk_scr[n_st] = qk[hs].astype(BF16)
                    c_scr[n_st] = _dot_tn(k_dec, u_c)
                    steps.append(j)

        Ss = list(Ss)
        bigs = []
        for i, j in enumerate(steps):
            big = jnp.dot(lhs_scr[i], Ss[j].astype(BF16), preferred_element_type=F32)
            Ss[j] = big[0:W] + c_scr[i]
            bigs.append(big)
        outs = [[] for _ in range(ns)]
        for i, j in enumerate(steps):
            v_new = u_scr[i] - bigs[i][W:W + L]
            outs[j].append(bigs[i][W + L:]
                           + _dot(qk_scr[i], jnp.concatenate([v_new, v_new], axis=0)))

        for j in range(ns):
            sj = slice(j * W, (j + 1) * W)
            o = jnp.concatenate(outs[j], axis=0)
            o = o * lax.rsqrt(jnp.mean(o * o, axis=-1, keepdims=True) + NORM_EPS) * ng
            o_ref[pl.ds(s, R), sj] = (o * _silu(z_ref[pl.ds(s, R), sj].astype(F32))
                                      ).astype(o_ref.dtype)
        return (tuple(Ss), tuple(t[0] for t in raws), tuple(t[1] for t in raws),
                tuple(t[2] for t in raws))

    z8 = jnp.zeros((8, W), F32)
    init = (tuple(jnp.zeros((W, W), F32) for _ in range(ns)), (z8,) * ns, (z8,) * ns, (z8,) * ns)
    lax.fori_loop(0, nsb, body, init, unroll=2)


def _gdn(p, seq, conv_w, a_log, dt_bias, norm_g):
    n = p.shape[0]
    nbatch = n // seq
    ns = 4
    wide = ns * LANES
    ngrp = GDN_HEADS // ns
    assert COL_QKV % wide == 0 and COL_Z % wide == 0 and GDN_WIDTH % wide == 0
    qoff = COL_QKV // wide
    nb = 2 if seq % (4 * CHUNK) == 0 else 1

    def col(off):
        return pl.BlockSpec((seq, wide), lambda b, h, off=off: (b, off + h))

    def cw(off):
        return pl.BlockSpec((CONV_WIDTH, wide), lambda b, h, off=off: (0, off + h))

    one = pl.BlockSpec((1, LANES), lambda b, h: (0, 0))
    in_specs = [
        col(qoff), col(qoff + ngrp), col(qoff + 2 * ngrp), col(COL_Z // wide),
        pl.BlockSpec((seq, LANES), lambda b, h: (b, COL_AB // LANES)),
        cw(0), cw(ngrp), cw(2 * ngrp), one, one, one,
    ]
    n_steps = 2 * nb * ns
    return pl.pallas_call(
        functools.partial(_gdn_kernel, nb=nb, ns=ns),
        grid=(nbatch, ngrp),
        in_specs=in_specs,
        out_specs=pl.BlockSpec((seq, wide), lambda b, h: (b, h)),
        out_shape=jax.ShapeDtypeStruct((n, GDN_WIDTH), BF16),
        scratch_shapes=[
            pltpu.VMEM((n_steps, 2 * CHUNK + LANES, LANES), BF16),
            pltpu.VMEM((n_steps, CHUNK, LANES), F32),
            pltpu.VMEM((n_steps, CHUNK, LANES), BF16),
            pltpu.VMEM((n_steps, LANES, LANES), F32),
        ],
        compiler_params=pltpu.CompilerParams(
            dimension_semantics=("parallel", "parallel"), vmem_limit_bytes=VMEM_LIMIT),
        name="gdn",
    )(p, p, p, p, p, conv_w, conv_w, conv_w, a_log, dt_bias, norm_g)


def _mixout_kernel(x_ref, ya_ref, yb_ref, ga_ref, gb_ref, wpa_ref, wpb_ref, wout_ref, g2_ref,
                   *rest, with_router):
    if with_router:
        rt_ref, xo_ref, h_ref, comb_ref = rest
    else:
        xo_ref, h_ref = rest
    tm = x_ref.shape[0]
    ng = 2 if tm % 32 == 0 else 1
    grp = [slice(q * (tm // ng), (q + 1) * (tm // ng)) for q in range(ng)]
    yas = [_dot(ya_ref[r, :], wpa_ref[...]) for r in grp]
    ybs = [_dot(yb_ref[r, :], wpb_ref[...]) for r in grp]
    ys = [_sigmoid(ga_ref[r, :].astype(F32)) * ya + _sigmoid(gb_ref[r, :].astype(F32)) * yb
          for r, ya, yb in zip(grp, yas, ybs)]
    xns = [x_ref[r, :] + _dot(y, wout_ref[...]) for r, y in zip(grp, ys)]
    hs = []
    for r, xn in zip(grp, xns):
        xo_ref[r, :] = xn
        h = xn * lax.rsqrt(jnp.mean(xn * xn, axis=-1, keepdims=True) + NORM_EPS) * g2_ref[...]
        h_ref[r, :] = h.astype(h_ref.dtype)
        hs.append(h)
    if with_router:
        rt = rt_ref[...]
        r_hi = rt.astype(BF16)
        r_lo = (rt - r_hi.astype(F32)).astype(BF16)
        neg = jnp.float32(-jnp.inf)
        for r, h in zip(grp, hs):
            h_hi = h.astype(BF16)
            h_lo = (h - h_hi.astype(F32)).astype(BF16)
            logits = (jnp.dot(h_hi, r_hi, preferred_element_type=F32)
                      + jnp.dot(h_lo, r_hi, preferred_element_type=F32)
                      + jnp.dot(h_hi, r_lo, preferred_element_type=F32))
            lane = _iota(logits.shape, 1)
            logits = jnp.where(lane < N_EXPERTS, logits, neg)
            m1 = jnp.max(logits, axis=1, keepdims=True)
            i1 = jnp.min(jnp.where(logits == m1, lane, LANES), axis=1, keepdims=True)
            l2 = jnp.where(lane == i1, neg, logits)
            m2 = jnp.max(l2, axis=1, keepdims=True)
            i2 = jnp.min(jnp.where(l2 == m2, lane, LANES), axis=1, keepdims=True)
            e2 = jnp.exp(m2 - m1)
            g1 = 1.0 / (1.0 + e2)
            g2 = e2 / (1.0 + e2)
            comb_ref[r, :] = (jnp.where(lane == 0, i1.astype(F32), 0.0)
                              + jnp.where(lane == 1, i2.astype(F32), 0.0)
                              + jnp.where(lane == 2, g1, 0.0) + jnp.where(lane == 3, g2, 0.0))


def _mixout(x2, ya, yb, p, wpa, wpb, wout, g2, router, tm):
    n = x2.shape[0]
    with_router = router is not None
    row = lambda w: pl.BlockSpec((tm, w), lambda i: (i, 0))
    full = lambda a: pl.BlockSpec(a.shape, lambda i: (0, 0))
    in_specs = [
        row(D_MODEL), row(RWKV_WIDTH), row(GDN_WIDTH),
        pl.BlockSpec((tm, D_MODEL), lambda i: (i, COL_GATE // D_MODEL)),
        pl.BlockSpec((tm, D_MODEL), lambda i: (i, COL_GATE // D_MODEL + 1)),
        full(wpa), full(wpb), full(wout), full(g2),
    ]
    args = [x2, ya, yb, p, p, wpa, wpb, wout, g2]
    out_specs = [row(D_MODEL), row(D_MODEL)]
    out_shape = [jax.ShapeDtypeStruct((n, D_MODEL), F32),
                 jax.ShapeDtypeStruct((n, D_MODEL), F32 if with_router else BF16)]
    if with_router:
        in_specs.append(full(router))
        args.append(router)
        out_specs.append(row(LANES))
        out_shape.append(jax.ShapeDtypeStruct((n, LANES), F32))
    return pl.pallas_call(
        functools.partial(_mixout_kernel, with_router=with_router),
        grid=(n // tm,),
        in_specs=in_specs,
        out_specs=out_specs,
        out_shape=out_shape,
        compiler_params=pltpu.CompilerParams(
            dimension_semantics=("parallel",), vmem_limit_bytes=VMEM_LIMIT),
        name="mixout_router" if with_router else "mixout",
    )(*args)


def _rms_out(y, fg_ref):
    return y * lax.rsqrt(jnp.mean(y * y, axis=-1, keepdims=True) + NORM_EPS) * fg_ref[...]


def _gate_up(wg, wu, tf):
    parts = []
    for j in range(D_FF // tf):
        parts += [wg[..., j * tf:(j + 1) * tf], wu[..., j * tf:(j + 1) * tf]]
    return jnp.concatenate(parts, axis=-1).astype(BF16)


def _swiglu_chunk(h, wgu, wd):
    tf = wd.shape[0]
    gu = jnp.dot(h, wgu, preferred_element_type=F32)
    act = _silu(gu[:, 0:tf]) * gu[:, tf:]
    return jnp.dot(act.astype(BF16), wd, preferred_element_type=F32)


def _ffn_kernel(h_ref, x_ref, wgu_ref, wd_ref, *rest, final):
    fg_ref = rest[0] if final else None
    o_ref, acc_ref = rest[-2:]
    j = pl.program_id(1)

    @pl.when(j == 0)
    def _():
        acc_ref[...] = jnp.zeros_like(acc_ref)

    acc_ref[...] += _swiglu_chunk(h_ref[...], wgu_ref[...], wd_ref[...])

    @pl.when(j == pl.num_programs(1) - 1)
    def _():
        y = x_ref[...] + acc_ref[...]
        o_ref[...] = _rms_out(y, fg_ref) if final else y


def _ffn(h, x2, wgu, wd, final_g, tm, tf):
    n = x2.shape[0]
    final = final_g is not None
    row = lambda w: pl.BlockSpec((tm, w), lambda i, j: (i, 0))
    in_specs = [
        row(D_MODEL), row(D_MODEL),
        pl.BlockSpec((D_MODEL, 2 * tf), lambda i, j: (0, j)),
        pl.BlockSpec((tf, D_MODEL), lambda i, j: (j, 0)),
    ]
    args = [h, x2, wgu, wd]
    if final:
        in_specs.append(pl.BlockSpec((1, D_MODEL), lambda i, j: (0, 0)))
        args.append(final_g)
    return pl.pallas_call(
        functools.partial(_ffn_kernel, final=final),
        grid=(n // tm, D_FF // tf),
        in_specs=in_specs,
        out_specs=row(D_MODEL),
        out_shape=jax.ShapeDtypeStruct((n, D_MODEL), F32),
        scratch_shapes=[pltpu.VMEM((tm, D_MODEL), F32)],
        compiler_params=pltpu.CompilerParams(
            dimension_semantics=("parallel", "arbitrary"), vmem_limit_bytes=VMEM_LIMIT),
        name="ffn",
    )(*args)


MOE_ROWS = 512


def _plan_kernel(sel_ref, pos_ref, tile_ref, cnt_scr, *, tm, n_tiles):
    ph = pl.program_id(0)
    i = pl.program_id(1)
    sel = sel_ref[...]
    lane = _iota((tm, LANES), 1)
    e1 = jnp.sum(jnp.where(lane == 0, sel, 0.0), axis=1, keepdims=True).astype(jnp.int32)
    e2 = jnp.sum(jnp.where(lane == 1, sel, 0.0), axis=1, keepdims=True).astype(jnp.int32)
    oh1 = lane == e1
    oh2 = lane == e2
    both = jnp.where(oh1, 1.0, 0.0) + jnp.where(oh2, 1.0, 0.0)

    @pl.when((ph == 0) & (i == 0))
    def _():
        cnt_scr[...] = jnp.zeros_like(cnt_scr)

    @pl.when(ph == 0)
    def _():
        cnt_scr[0:1, :] += jnp.sum(both, axis=0, keepdims=True)

    @pl.when((ph == 1) & (i == 0))
    def _():
        cnt = cnt_scr[0:1, :]
        padded = jnp.floor((cnt + (MOE_ROWS - 1)) * (1.0 / MOE_ROWS)) * MOE_ROWS
        rl = _iota((LANES, LANES), 0)
        cl = _iota((LANES, LANES), 1)
        upper = jnp.where(rl < cl, 1.0, 0.0)
        starts = _dot_hi(jnp.broadcast_to(padded, (8, LANES)), upper)[0:1, :]
        ends = starts + padded
        cnt_scr[0:1, :] = starts
        tl = _iota((n_tiles, LANES), 1)
        t0 = (_iota((n_tiles, LANES), 0) * MOE_ROWS).astype(F32)
        lane_ok = tl < N_EXPERTS
        owner = jnp.sum(jnp.where(lane_ok, jnp.where(t0 >= ends, 1.0, 0.0), 0.0),
                        axis=1, keepdims=True)
        used = jnp.sum(jnp.where(lane_ok, padded, 0.0), axis=1, keepdims=True)
        valid = t0[:, 0:1] < used
        last_e = jnp.max(jnp.where(lane_ok, jnp.where(padded > 0.0, tl.astype(F32), 0.0), 0.0),
                         axis=1, keepdims=True)
        ex = jnp.where(valid, jnp.minimum(owner, N_EXPERTS - 1.0), last_e)
        tile_ref[...] = (jnp.where(tl == 0, ex, 0.0)
                         + jnp.where(tl == 1, jnp.where(valid, 1.0, 0.0), 0.0)).astype(jnp.int32)

    @pl.when(ph == 1)
    def _():
        rr = _iota((tm, tm), 0)
        cc = _iota((tm, tm), 1)
        before = jnp.where(cc < rr, 1.0, 0.0).astype(BF16)
        base = cnt_scr[0:1, :] + jnp.dot(before, both.astype(BF16), preferred_element_type=F32)
        p1 = jnp.sum(jnp.where(oh1, base, 0.0), axis=1, keepdims=True)
        p2 = jnp.sum(jnp.where(oh2, base, 0.0), axis=1, keepdims=True)
        pos_ref[...] = (jnp.where(lane == 0, p1, 0.0) + jnp.where(lane == 1, p2, 0.0)
                        ).astype(jnp.int32)
        cnt_scr[0:1, :] += jnp.sum(both, axis=0, keepdims=True)


def _plan(sel, tm, n_tiles):
    n = sel.shape[0]
    return pl.pallas_call(
        functools.partial(_plan_kernel, tm=tm, n_tiles=n_tiles),
        grid=(2, n // tm),
        in_specs=[pl.BlockSpec((tm, LANES), lambda ph, i: (i, 0))],
        out_specs=[pl.BlockSpec((tm, LANES), lambda ph, i: (i * ph, 0)),
                   pl.BlockSpec((n_tiles, LANES), lambda ph, i: (0, 0))],
        out_shape=[jax.ShapeDtypeStruct((n, LANES), jnp.int32),
                   jax.ShapeDtypeStruct((n_tiles, LANES), jnp.int32)],
        scratch_shapes=[pltpu.VMEM((8, LANES), F32)],
        compiler_params=pltpu.CompilerParams(
            dimension_semantics=("arbitrary", "arbitrary"), vmem_limit_bytes=VMEM_LIMIT),
        name="moe_plan",
    )(sel)


def _row_copy(src_ref, src_row, dst_ref, dst_row, sem):
    return pltpu.make_async_copy(src_ref.at[pl.ds(src_row, 1)], dst_ref.at[pl.ds(dst_row, 1)], sem)


def _dispatch_kernel(pos_ref, h_ref, xs_in_ref, xs_ref, sem, *, tm):
    del xs_in_ref

    def start(r, c):
        _row_copy(h_ref, r, xs_ref, pos_ref[2 * r], sem).start(priority=0)
        _row_copy(h_ref, r, xs_ref, pos_ref[2 * r + 1], sem).start(priority=1)
        return c

    def wait(r, c):
        _row_copy(h_ref, 0, xs_ref, 0, sem).wait()
        _row_copy(h_ref, 0, xs_ref, 0, sem).wait()
        return c

    for r in range(tm):
        start(r, 0)
    lax.fori_loop(0, tm, wait, 0, unroll=8)


def _dispatch(pos_flat, h, n_rows, tm):
    n = h.shape[0]
    xs0 = jnp.zeros((n_rows, D_MODEL), F32)
    return pl.pallas_call(
        functools.partial(_dispatch_kernel, tm=tm),
        grid=(n // tm,),
        in_specs=[
            pl.BlockSpec((2 * tm,), lambda i: (i,), memory_space=pltpu.SMEM),
            pl.BlockSpec((tm, D_MODEL), lambda i: (i, 0)),
            pl.BlockSpec(memory_space=pl.ANY),
        ],
        out_specs=pl.BlockSpec(memory_space=pl.ANY),
        out_shape=jax.ShapeDtypeStruct((n_rows, D_MODEL), F32),
        scratch_shapes=[pltpu.SemaphoreType.DMA(())],
        input_output_aliases={2: 0},
        compiler_params=pltpu.CompilerParams(
            dimension_semantics=("arbitrary",), vmem_limit_bytes=VMEM_LIMIT),
        name="moe_dispatch",
    )(pos_flat, h, xs0)


def _experts_kernel(te_ref, tv_ref, xs_ref, wgu_ref, wd_ref, ys_ref, xb_scr, acc_scr):
    t = pl.program_id(0)
    j = pl.program_id(1)
    del te_ref

    @pl.when(j == 0)
    def _():
        xb_scr[...] = xs_ref[...].astype(BF16)
        acc_scr[...] = jnp.zeros_like(acc_scr)

    @pl.when(tv_ref[t] == 1)
    def _():
        acc_scr[...] += _swiglu_chunk(xb_scr[...], wgu_ref[0], wd_ref[0])

    @pl.when(j == pl.num_programs(1) - 1)
    def _():
        ys_ref[...] = acc_scr[...]


def _experts(tile_e, tile_v, xs, wgu, wd, tf):
    n_rows = xs.shape[0]
    grid_spec = pltpu.PrefetchScalarGridSpec(
        num_scalar_prefetch=2,
        grid=(n_rows // MOE_ROWS, D_FF // tf),
        in_specs=[
            pl.BlockSpec((MOE_ROWS, D_MODEL), lambda t, j, te, tv: (t, 0)),
            pl.BlockSpec((1, D_MODEL, 2 * tf), lambda t, j, te, tv: (te[t], 0, j)),
            pl.BlockSpec((1, tf, D_MODEL), lambda t, j, te, tv: (te[t], j, 0)),
        ],
        out_specs=pl.BlockSpec((MOE_ROWS, D_MODEL), lambda t, j, te, tv: (t, 0)),
        scratch_shapes=[pltpu.VMEM((MOE_ROWS, D_MODEL), BF16), pltpu.VMEM((MOE_ROWS, D_MODEL), F32)],
    )
    return pl.pallas_call(
        _experts_kernel,
        grid_spec=grid_spec,
        out_shape=jax.ShapeDtypeStruct((n_rows, D_MODEL), F32),
        compiler_params=pltpu.CompilerParams(
            dimension_semantics=("parallel", "arbitrary"), vmem_limit_bytes=VMEM_LIMIT),
        name="moe_experts",
    )(tile_e, tile_v, xs, wgu, wd)


def _combine_kernel(pos_ref, x_ref, sel_ref, *rest, tm, final):
    fg_ref = rest[0] if final else None
    ys_ref, o_ref, buf, sem = rest[-4:]

    def start(r, c):
        _row_copy(ys_ref, pos_ref[2 * r], buf.at[0], r, sem).start(priority=0)
        _row_copy(ys_ref, pos_ref[2 * r + 1], buf.at[1], r, sem).start(priority=1)
        return c

    def wait(r, c):
        _row_copy(ys_ref, 0, buf.at[0], 0, sem).wait()
        _row_copy(ys_ref, 0, buf.at[1], 0, sem).wait()
        return c

    for r in range(tm):
        start(r, 0)
    lax.fori_loop(0, tm, wait, 0, unroll=8)
    sel = sel_ref[...]
    lane = _iota(sel.shape, 1)
    g1 = jnp.sum(jnp.where(lane == 2, sel, 0.0), axis=1, keepdims=True)
    g2 = jnp.sum(jnp.where(lane == 3, sel, 0.0), axis=1, keepdims=True)
    y = x_ref[...] + g1 * buf[0] + g2 * buf[1]
    o_ref[...] = _rms_out(y, fg_ref) if final else y


def _combine(pos_flat, x2, sel, ys, final_g, tm):
    n = x2.shape[0]
    final = final_g is not None
    in_specs = [
        pl.BlockSpec((2 * tm,), lambda i: (i,), memory_space=pltpu.SMEM),
        pl.BlockSpec((tm, D_MODEL), lambda i: (i, 0)),
        pl.BlockSpec((tm, LANES), lambda i: (i, 0)),
    ]
    args = [pos_flat, x2, sel]
    if final:
        in_specs.append(pl.BlockSpec((1, D_MODEL), lambda i: (0, 0)))
        args.append(final_g)
    in_specs.append(pl.BlockSpec(memory_space=pl.ANY))
    args.append(ys)
    return pl.pallas_call(
        functools.partial(_combine_kernel, tm=tm, final=final),
        grid=(n // tm,),
        in_specs=in_specs,
        out_specs=pl.BlockSpec((tm, D_MODEL), lambda i: (i, 0)),
        out_shape=jax.ShapeDtypeStruct((n, D_MODEL), F32),
        scratch_shapes=[pltpu.VMEM((2, tm, D_MODEL), F32), pltpu.SemaphoreType.DMA(())],
        compiler_params=pltpu.CompilerParams(
            dimension_semantics=("arbitrary",), vmem_limit_bytes=VMEM_LIMIT),
        name="moe_combine",
    )(*args)


def _moe(h, x2, sel, wgu, wd, final_g, tm, tf):
    n = x2.shape[0]
    n_tiles = pl.cdiv(2 * n, MOE_ROWS) + N_EXPERTS
    pos, tiles = _plan(sel, tm, n_tiles)
    pos_flat = pos[:, 0:2].reshape(2 * n)
    xs = _dispatch(pos_flat, h, n_tiles * MOE_ROWS, tm)
    ys = _experts(tiles[:, 0], tiles[:, 1], xs, wgu, wd, tf)
    return _combine(pos_flat, x2, sel, ys, final_g, tm)


def _tile(n, pref):
    t = min(pref, n)
    while n % t:
        t //= 2
    return t


def _pad_lanes(a, width=LANES):
    return jnp.pad(a, ((0, 0), (0, width - a.shape[-1])))


def kernel(x, norm1_g, w_in, tshift_mu, w0, w_decay_up, a0, w_iclr_up, w_gate_up, k_k, k_a, r_k,
           lnx_g, lnx_b, w_pa, conv_w, a_log, dt_bias, gdn_norm_g, w_pb, w_out, norm2_g,
           ffn_w_gate, ffn_w_up, ffn_w_down, moe_router, moe_w_gate, moe_w_up, moe_w_down, final_g):
    nb, seq, d = x.shape
    n = nb * seq
    x2 = x.reshape(n, d)
    tm_in = _tile(n, 1024)
    tm = _tile(n, 512)
    rw = 1792
    gq = 3 * GDN_WIDTH
    for layer in range(DEPTH):
        wi = w_in[layer]
        ab = wi[:, rw + gq:rw + gq + 2 * GDN_HEADS]
        zc = wi[:, rw + gq + 2 * GDN_HEADS:rw + gq + 2 * GDN_HEADS + GDN_WIDTH]
        gate = wi[:, rw + gq + 2 * GDN_HEADS + GDN_WIDTH:]
        w_cat = jnp.concatenate(
            [wi[:, :rw], ab, jnp.zeros((d, COL_QKV - COL_AB - 2 * GDN_HEADS), F32),
             wi[:, rw:rw + gq], zc, gate],
            axis=1).astype(BF16)
        p = _inproj(x2, norm1_g[layer][None, :], w_cat, tm_in, P_COLS // 2)
        one = lambda a: a[layer][None, :]
        ya = _rwkv(p, seq, one(tshift_mu), one(w0), one(a0), one(k_k), one(k_a), one(r_k),
                   one(lnx_g), one(lnx_b), w_decay_up[layer].astype(BF16),
                   w_iclr_up[layer].astype(BF16), w_gate_up[layer].astype(BF16))
        yb = _gdn(p, seq, conv_w[layer], _pad_lanes(one(a_log)), _pad_lanes(one(dt_bias)),
                  one(gdn_norm_g))
        j = layer // 2
        routed = layer % 2 == 1
        router = _pad_lanes(moe_router[j]) if routed else None
        res = _mixout(x2, ya, yb, p, w_pa[layer].astype(BF16), w_pb[layer].astype(BF16),
                      w_out[layer].astype(BF16), one(norm2_g), router, tm)
        fg = final_g[None, :] if layer == DEPTH - 1 else None
        if routed:
            x2, h, sel = res
            x2 = _moe(h, x2, sel, _gate_up(moe_w_gate[j], moe_w_up[j], FF_CHUNK),
                      moe_w_down[j].astype(BF16), fg, tm, FF_CHUNK)
        else:
            x2, h = res
            x2 = _ffn(h, x2, _gate_up(ffn_w_gate[j], ffn_w_up[j], FF_CHUNK),
                      ffn_w_down[j].astype(BF16), fg, tm, FF_CHUNK)
    return x2.reshape(nb, seq, d)
```

```python
import functools

import jax
import jax.numpy as jnp
from jax import lax
from jax.experimental import pallas as pl
from jax.experimental.pallas import tpu as pltpu

F32 = jnp.float32
BF16 = jnp.bfloat16
HI = lax.Precision.HIGHEST

D_MODEL = 1024
DEPTH = 2
RWKV_HEADS = 8
RWKV_HEAD_DIM = 64
RWKV_WIDTH = 512
DECAY_LORA = 64
ICLR_LORA = 64
GATE_LORA = 128
RWKV_GN_EPS = 64e-5
GDN_HEADS = 4
GDN_HEAD_DIM = 128
GDN_WIDTH = 512
CONV_WIDTH = 4
CHUNK = 64
D_FF = 2816
N_EXPERTS = 8
NORM_EPS = 1e-6
L2_EPS = 1e-6

LANES = 128
P_COLS = 6144
COL_AB = 1792
COL_QKV = 2048
COL_Z = 3584
COL_GATE = 4096
VMEM_LIMIT = 56 * 1024 * 1024
FF_CHUNK = D_FF // 2


def _dot(a, b):
    return jnp.dot(a.astype(BF16), b.astype(BF16), preferred_element_type=F32)


def _dot_nt(a, b):
    return lax.dot_general(a.astype(BF16), b.astype(BF16), (((1,), (1,)), ((), ())),
                           preferred_element_type=F32)


def _dot_tn(a, b):
    return lax.dot_general(a.astype(BF16), b.astype(BF16), (((0,), (0,)), ((), ())),
                           preferred_element_type=F32)


def _dot_hi(a, b):
    return jnp.dot(a, b, preferred_element_type=F32, precision=HI)


def _softplus(x):
    return jnp.maximum(x, 0.0) + jnp.log(1.0 + jnp.exp(-jnp.abs(x)))


def _sigmoid(x):
    return 1.0 / (1.0 + jnp.exp(-x))


def _silu(x):
    return x * _sigmoid(x)


def _iota(shape, dim):
    return lax.broadcasted_iota(jnp.int32, shape, dim)


def _inproj_kernel(x_ref, g_ref, w_ref, p_ref, h_scr):
    @pl.when(pl.program_id(1) == 0)
    def _():
        x = x_ref[...]
        h = x * lax.rsqrt(jnp.mean(x * x, axis=-1, keepdims=True) + NORM_EPS) * g_ref[...]
        h_scr[...] = h.astype(BF16)

    p_ref[...] = jnp.dot(h_scr[...], w_ref[...], preferred_element_type=F32).astype(p_ref.dtype)


def _inproj(x2, g, w_cat, tm, tn):
    n = x2.shape[0]
    return pl.pallas_call(
        _inproj_kernel,
        grid=(n // tm, P_COLS // tn),
        in_specs=[
            pl.BlockSpec((tm, D_MODEL), lambda i, j: (i, 0)),
            pl.BlockSpec((1, D_MODEL), lambda i, j: (0, 0)),
            pl.BlockSpec((D_MODEL, tn), lambda i, j: (0, j)),
        ],
        out_specs=pl.BlockSpec((tm, tn), lambda i, j: (i, j)),
        out_shape=jax.ShapeDtypeStruct((n, P_COLS), BF16),
        scratch_shapes=[pltpu.VMEM((tm, D_MODEL), BF16)],
        compiler_params=pltpu.CompilerParams(
            dimension_semantics=("parallel", "arbitrary"), vmem_limit_bytes=VMEM_LIMIT),
        name="inproj",
    )(x2, g, w_cat)


def _split_dot(x, m):
    hi = x.astype(BF16)
    lo = (x - hi.astype(F32)).astype(BF16)
    return (jnp.dot(hi, m, preferred_element_type=F32) + jnp.dot(lo, m, preferred_element_type=F32))


def _chunk_cumsum(x, tin):
    y = x
    sh = 1
    while sh < CHUNK:
        y = y + jnp.where(tin >= sh, pltpu.roll(y, sh, axis=0), 0.0)
        sh *= 2
    return y


def _level_masks(rm, lm):
    out = []
    for k in range(6):
        x = rm >> k
        y = lm >> k
        out.append((((x ^ y) + 2 * (1 - (x & 1))) == 1).astype(F32))
    return out


def _tri_inverse(mats, eye, masks):
    ts = [eye + a * masks[0] for a in mats]
    for k in range(1, 6):
        xs = [_dot(a * masks[k], t) for a, t in zip(mats, ts)]
        ts = [t + _dot(t, x) for t, x in zip(ts, xs)]
    return ts


def _rwkv_kernel(r_ref, k_ref, v_ref, lo_ref, mur_ref, muk_ref, muv_ref, mulo_ref,
                 w0_ref, a0_ref, kk_ref, ka_ref, rk_ref, lg_ref, lb_ref,
                 wd_ref, wa_ref, wg_ref, bd_ref, o_ref,
                 lhs_scr, u0_scr, qb_scr, ov_scr, c_scr, pre_scr, *, nb, ns):
    L = CHUNK
    W = LANES
    R = nb * L
    seq = r_ref.shape[0]
    nsb = seq // R

    row_r = _iota((R, W), 0)
    row_r2 = _iota((R, 2 * W), 0)
    row_c = _iota((L, W), 0)
    lane = _iota((L, W), 1)
    h1 = lane < 64
    rm = _iota((W, W), 0)
    lm = _iota((W, W), 1)
    tri_mask = (lm & 63) < (rm & 63) + (rm >> 6)
    blk_mask = (rm >> 6) == (lm >> 6)
    eye = (rm == lm).astype(F32)
    masks = _level_masks(rm, lm)

    mur, muk, muv, mulo = mur_ref[...], muk_ref[...], muv_ref[...], mulo_ref[...]
    w0, a0, k_k, k_a, r_k = w0_ref[...], a0_ref[...], kk_ref[...], ka_ref[...], rk_ref[...]
    ln_g, ln_b = lg_ref[...], lb_ref[...]
    wd, wa, wg = wd_ref[...], wa_ref[...], wg_ref[...]

    def shift(x, prev_row, mu, rowi):
        xp = pltpu.roll(x, 1, axis=0)
        xp = jnp.where(rowi == 0, prev_row, xp)
        return x + (xp - x) * mu

    def last_row_before(ref, s, cols):
        rows = ref[pl.ds(pl.multiple_of(jnp.maximum(s - 16, 0), 16), 16), cols].astype(F32)
        return rows[15:16, :] * (s > 0).astype(F32)

    keys = ("v", "k2", "bh", "cl", "g", "bonus", "rt", "at", "bt", "kt")

    class Stored:
        def __init__(self, par, j):
            self.par, self.j = par, j

        def __getitem__(self, key):
            par, j, q = self.par, self.j, keys.index(key)

            class Rows:
                def __getitem__(self, rows):
                    return pre_scr[par, j, q, rows]
            return Rows()

    def prologue_pieces(sb, par):
        sh = {}

        def shared():
            s = pl.multiple_of(sb * R, R)
            lo_raw = lo_ref[pl.ds(s, R), :].astype(F32)
            lo = shift(lo_raw, last_row_before(lo_ref, s, slice(None)), mulo, row_r2)
            wl = lo[:, 0:DECAY_LORA]
            al = lo[:, DECAY_LORA:DECAY_LORA + ICLR_LORA]
            gl = lo[:, DECAY_LORA + ICLR_LORA:]
            sh["z"] = w0 + _dot(jnp.tanh(wl), wd)
            sh["a"] = _sigmoid(a0 + _dot(al, wa))
            sh["g"] = _dot(_sigmoid(gl), wg)
            for j in range(ns):
                sj = slice(j * W, (j + 1) * W)
                for name, ref, mu in (("r", r_ref, mur), ("k", k_ref, muk), ("v", v_ref, muv)):
                    sh[name, j] = shift(ref[pl.ds(s, R), sj].astype(F32), last_row_before(ref, s, sj),
                                        mu[:, sj], row_r)

        def piece(j, c):
            sj = slice(j * W, (j + 1) * W)
            r, k, v = sh["r", j][c], sh["k", j][c], sh["v", j][c]
            w_log = -_softplus(-sh["z"][c, sj]) - 0.5
            lw = -jnp.exp(w_log)
            a = sh["a"][c, sj]
            kk = k * k_k[:, sj]
            kk = kk * lax.rsqrt(_split_dot(kk * kk, bd_ref[...]) + L2_EPS)
            k2 = k * (1.0 + (a - 1.0) * k_a[:, sj])
            bh = kk * a
            cl = _chunk_cumsum(lw, row_c)
            e_neg = jnp.exp(-cl)
            vals = dict(
                v=v, k2=k2, bh=bh, cl=cl, g=sh["g"][c, sj],
                bonus=_split_dot(r * k2 * r_k[:, sj], bd_ref[...]) * v,
                rt=r * jnp.exp(cl), at=-kk * jnp.exp(cl - lw), bt=bh * e_neg, kt=k2 * e_neg)
            for q, key in enumerate(keys):
                pre_scr[par, j, q, c] = vals[key]

        return [shared] + [functools.partial(piece, j, slice(i * L, (i + 1) * L))
                           for i in range(nb) for j in range(ns)]

    def run(par, Hs, sb, fillers):
        pre = [Stored(par, j) for j in range(ns)]

        def fill():
            if fillers:
                fillers.pop(0)()

        s = pl.multiple_of(sb * R, R)
        items = [(j, slice(i * L, (i + 1) * L)) for i in range(nb) for j in range(ns)]
        v_of = lambda it: pre[it[0]]["v"][it[1]]
        rts = [pre[j]["rt"][c] for j, c in items]
        ats = [pre[j]["at"][c] for j, c in items]
        zero = jnp.zeros((L, W), F32)
        m1s, m2s = [], []
        for (j, c), rt, at in zip(items, rts, ats):
            bt, kt = pre[j]["bt"][c], pre[j]["kt"][c]
            lhs1 = jnp.concatenate([jnp.where(h1, at, zero), jnp.where(h1, rt, zero)], axis=0)
            lhs2 = jnp.concatenate([jnp.where(h1, zero, at), jnp.where(h1, zero, rt)], axis=0)
            m1s.append(jnp.where(tri_mask, _dot_nt(lhs1, jnp.concatenate([bt, kt], axis=0)), 0.0))
            m2s.append(jnp.where(tri_mask, _dot_nt(lhs2, jnp.concatenate([kt, bt], axis=0)), 0.0))
        fill()
        a_bds = [jnp.concatenate([jnp.where(h1, m1[0:L], 0.0), jnp.where(h1, 0.0, m2[0:L])], axis=0)
                 for m1, m2 in zip(m1s, m2s)]
        tinvs = _tri_inverse(a_bds, eye, masks)
        avs, o_vs = [], []
        for it, m1, m2 in zip(items, m1s, m2s):
            vc = v_of(it)
            ak_m = jnp.concatenate(
                [jnp.where(h1, 0.0, m1[0:L]), jnp.where(h1, m2[0:L], 0.0)], axis=0)
            avs.append(_dot(ak_m, jnp.concatenate([vc, vc], axis=0)))
            qk = jnp.where(h1, m2[L:], m1[L:])
            vx = jnp.concatenate([jnp.where(h1, 0.0, vc), jnp.where(h1, vc, 0.0)], axis=0)
            o_vs.append(_dot(qk, vx))
        qbs = [jnp.where(h1, m1[L:], m2[L:]) for m1, m2 in zip(m1s, m2s)]
        u0s = [_dot(t, av) for t, av in zip(tinvs, avs)]
        tas = [jnp.where(blk_mask, _dot(t, jnp.concatenate([at, at], axis=0)), 0.0)
               for t, at in zip(tinvs, ats)]
        for i, ((j, c), ta, u0) in enumerate(zip(items, tas, u0s)):
            cl_i = pre[j]["cl"][c]
            cl_last = cl_i[L - 1:L, :]
            e_end = jnp.exp(cl_last - cl_i)
            b_end = pre[j]["bh"][c] * e_end
            b_st = jnp.concatenate([jnp.where(h1, b_end, 0.0), jnp.where(h1, 0.0, b_end)], axis=0)
            u0_bd = jnp.where(blk_mask, u0, 0.0)
            m_t = eye * jnp.exp(cl_last) + _dot_tn(b_st, ta)
            lhs_scr[i] = jnp.concatenate([m_t, ta, rts[i]], axis=0).astype(BF16)
            u0_scr[i] = u0_bd
            qb_scr[i] = qbs[i].astype(BF16)
            ov_scr[i] = o_vs[i]
            c_scr[i] = (_dot_tn(b_st, u0_bd)
                        + jnp.where(blk_mask, _dot_tn(pre[j]["k2"][c] * e_end, v_of((j, c))), 0.0))

        Hs = list(Hs)
        bigs = []
        for i, (j, c) in enumerate(items):
            big = jnp.dot(lhs_scr[i], Hs[j].astype(BF16), preferred_element_type=F32)
            Hs[j] = big[0:W] + c_scr[i]
            bigs.append(big)
            fill()
        outs = [[] for _ in range(ns)]
        for i, (j, c) in enumerate(items):
            u_bd = jnp.where(blk_mask, bigs[i][W:2 * W], 0.0) + u0_scr[i]
            outs[j].append(bigs[i][2 * W:] + _dot(qb_scr[i], u_bd) + ov_scr[i])

        for j in range(ns):
            sj = slice(j * W, (j + 1) * W)
            o = jnp.concatenate(outs[j], axis=0)
            mean = _split_dot(o, bd_ref[...]) * (1.0 / 64.0)
            oc = o - mean
            var = _split_dot(oc * oc, bd_ref[...]) * (1.0 / 64.0)
            y = (oc * lax.rsqrt(var + RWKV_GN_EPS) * ln_g[:, sj] + ln_b[:, sj]
                 + pre[j]["bonus"][slice(0, R)])
            o_ref[pl.ds(s, R), sj] = (y * pre[j]["g"][slice(0, R)]).astype(o_ref.dtype)
        while fillers:
            fill()
        return tuple(Hs)

    for thunk in prologue_pieces(0, 0):
        thunk()

    def body(t, Hs):
        Hs = run(0, Hs, 2 * t, prologue_pieces(2 * t + 1, 1))
        return run(1, Hs, 2 * t + 1, prologue_pieces(jnp.minimum(2 * t + 2, nsb - 1), 0))

    lax.fori_loop(0, nsb // 2, body, tuple(jnp.zeros((W, W), F32) for _ in range(ns)))


def _rwkv(p, seq, mu, w0, a0, k_k, k_a, r_k, ln_g, ln_b, wd, wa, wg):
    n = p.shape[0]
    nbatch = n // seq
    ns = 4
    wide = ns * LANES
    ngrp = RWKV_WIDTH // wide
    nb = 4 if seq % (8 * CHUNK) == 0 else 1

    def col(off):
        return pl.BlockSpec((seq, wide), lambda b, q, off=off: (b, off + q))

    def par(off=0):
        return pl.BlockSpec((1, wide), lambda b, q, off=off: (0, off + q))

    in_specs = [
        col(0), col(ngrp), col(2 * ngrp),
        pl.BlockSpec((seq, 2 * LANES), lambda b, q: (b, 3 * RWKV_WIDTH // (2 * LANES))),
        par(0), par(ngrp), par(2 * ngrp),
        pl.BlockSpec((1, 2 * LANES), lambda b, q: (0, 3 * RWKV_WIDTH // (2 * LANES))),
        par(), par(), par(), par(), par(), par(), par(),
        pl.BlockSpec((DECAY_LORA, wide), lambda b, q: (0, q)),
        pl.BlockSpec((ICLR_LORA, wide), lambda b, q: (0, q)),
        pl.BlockSpec((GATE_LORA, wide), lambda b, q: (0, q)),
        pl.BlockSpec((LANES, LANES), lambda b, q: (0, 0)),
    ]
    head_id = jnp.arange(LANES) // RWKV_HEAD_DIM
    bd_ones = (head_id[:, None] == head_id[None, :]).astype(BF16)
    return pl.pallas_call(
        functools.partial(_rwkv_kernel, nb=nb, ns=ns),
        grid=(nbatch, ngrp),
        in_specs=in_specs,
        out_specs=pl.BlockSpec((seq, wide), lambda b, q: (b, q)),
        out_shape=jax.ShapeDtypeStruct((n, RWKV_WIDTH), BF16),
        scratch_shapes=[
            pltpu.VMEM((nb * ns, 2 * LANES + CHUNK, LANES), BF16),
            pltpu.VMEM((nb * ns, LANES, LANES), F32),
            pltpu.VMEM((nb * ns, CHUNK, LANES), BF16),
            pltpu.VMEM((nb * ns, CHUNK, LANES), F32),
            pltpu.VMEM((nb * ns, LANES, LANES), F32),
            pltpu.VMEM((2, ns, 10, nb * CHUNK, LANES), F32),
        ],
        compiler_params=pltpu.CompilerParams(
            dimension_semantics=("parallel", "parallel"), vmem_limit_bytes=VMEM_LIMIT),
        name="rwkv",
    )(p, p, p, p, mu, mu, mu, mu, w0, a0, k_k, k_a, r_k, ln_g, ln_b, wd, wa, wg, bd_ones)


def _gdn_kernel(q_ref, k_ref, v_ref, z_ref, ab_ref, cwq_ref, cwk_ref, cwv_ref,
                alog_ref, dtb_ref, ng_ref, o_ref, lhs_scr, u_scr, qk_scr, c_scr, *, nb, ns):
    L = CHUNK
    W = LANES
    P = 2 * L
    R = nb * P
    seq = q_ref.shape[0]
    nsb = seq // R
    hd0 = pl.program_id(1) * ns

    row_r = _iota((R, W), 0)
    lane_r = _iota((R, W), 1)
    tin = row_r & (L - 1)
    row8 = _iota((8, W), 0)
    rm = _iota((W, W), 0)
    lm = _iota((W, W), 1)
    dif = rm - lm
    tin_m = rm & (L - 1)
    eye = (rm == lm).astype(F32)
    masks = _level_masks(rm, lm)

    lane1 = _iota((1, W), 1)
    pick = lambda ref, j: jnp.sum(jnp.where(lane1 == hd0 + j, ref[...], 0.0), axis=1, keepdims=True)
    neg_a = [-jnp.exp(pick(alog_ref, j)) for j in range(ns)]
    dt_b = [pick(dtb_ref, j) for j in range(ns)]
    cwq, cwk, cwv = cwq_ref[...], cwk_ref[...], cwv_ref[...]
    ng = ng_ref[...]
    scale = GDN_HEAD_DIM ** -0.5
    neg_inf = jnp.float32(-jnp.inf)

    def conv(x, prev8, cw):
        acc = x * cw[CONV_WIDTH - 1:CONV_WIDTH, :]
        for j in range(1, CONV_WIDTH):
            xs = pltpu.roll(x, j, axis=0)
            ps = pltpu.roll(prev8, j, axis=0)
            head = jnp.where(row8 < j, ps, xs[0:8])
            xs = jnp.concatenate([head, xs[8:]], axis=0)
            acc = acc + xs * cw[CONV_WIDTH - 1 - j:CONV_WIDTH - j, :]
        return _silu(acc)

    def l2n(x):
        return x * lax.rsqrt(jnp.sum(x * x, axis=-1, keepdims=True) + L2_EPS)

    def body(sb, carry):
        Ss, pqs, pks, pvs = carry
        s = pl.multiple_of(sb * R, R)
        ab = ab_ref[pl.ds(s, R), :].astype(F32)

        pre, raws = [], []
        for j in range(ns):
            sj = slice(j * W, (j + 1) * W)
            hd = hd0 + j
            q_raw = q_ref[pl.ds(s, R), sj].astype(F32)
            k_raw = k_ref[pl.ds(s, R), sj].astype(F32)
            v_raw = v_ref[pl.ds(s, R), sj].astype(F32)
            raws.append((q_raw[R - 8:, :], k_raw[R - 8:, :], v_raw[R - 8:, :]))
            q = l2n(conv(q_raw, pqs[j], cwq[:, sj])) * scale
            k = l2n(conv(k_raw, pks[j], cwk[:, sj]))
            v = conv(v_raw, pvs[j], cwv[:, sj])
            a_in = jnp.sum(jnp.where(lane_r == hd, ab, 0.0), axis=1, keepdims=True)
            b_in = jnp.sum(jnp.where(lane_r == hd + GDN_HEADS, ab, 0.0), axis=1, keepdims=True)
            beta = _sigmoid(b_in)
            g_log = neg_a[j] * _softplus(a_in + dt_b[j])
            gc = _chunk_cumsum(jnp.broadcast_to(g_log, (R, W)), tin)
            e_gc = jnp.exp(gc)
            kb = k * beta
            pre.append(dict(q=q, k=k, gc=gc, kb=kb, vb=v * beta, kbe=kb * e_gc, qe=q * e_gc))

        items = [(j, slice(i * P, (i + 1) * P)) for i in range(nb) for j in range(ns)]
        decs = []
        for j, sl in items:
            gc_p = pre[j]["gc"][sl]
            dlog = gc_p - jnp.transpose(gc_p)
            dlog = jnp.where(dif >= 0, dlog, neg_inf)
            decs.append(jnp.exp(jnp.where(dif <= tin_m, dlog, neg_inf)))
        negAs = [jnp.where(dif > 0, _dot_nt(pre[j]["kb"][sl], pre[j]["k"][sl]) * dec, 0.0) * -1.0
                 for (j, sl), dec in zip(items, decs)]
        qks = [_dot_nt(pre[j]["q"][sl], pre[j]["k"][sl]) * dec for (j, sl), dec in zip(items, decs)]
        tinvs = _tri_inverse(negAs, eye, masks)
        uws = [_dot(t, jnp.concatenate([pre[j]["vb"][sl], pre[j]["kbe"][sl]], axis=1))
               for t, (j, sl) in zip(tinvs, items)]
        steps = []
        for i in range(nb):
            for c in range(2):
                for j in range(ns):
                    it = i * ns + j
                    sl, uw, qk = items[it][1], uws[it], qks[it]
                    cs = slice(sl.start + c * L, sl.start + (c + 1) * L)
                    hs = slice(c * L, (c + 1) * L)
                    gc_c = pre[j]["gc"][cs]
                    gc_last = gc_c[L - 1:L, :]
                    k_dec = pre[j]["k"][cs] * jnp.exp(gc_last - gc_c)
                    u_c, w_c = uw[hs, 0:W], uw[hs, W:]
                    m_c = eye * jnp.exp(gc_last) - _dot_tn(k_dec, w_c)
                    n_st = len(steps)
                    lhs_scr[n_st] = jnp.concatenate(
                        [m_c, w_c, pre[j]["qe"][cs]], axis=0).astype(BF16)
                    u_scr[n_st] = u_c
                    qk_scr[n_st] = qk[hs].astype(BF16)
                    c_scr[n_st] = _dot_tn(k_dec, u_c)
                    steps.append(j)

        Ss = list(Ss)
        bigs = []
        for i, j in enumerate(steps):
            big = jnp.dot(lhs_scr[i], Ss[j].astype(BF16), preferred_element_type=F32)
            Ss[j] = big[0:W] + c_scr[i]
            bigs.append(big)
        outs = [[] for _ in range(ns)]
        for i, j in enumerate(steps):
            v_new = u_scr[i] - bigs[i][W:W + L]
            outs[j].append(bigs[i][W + L:]
                           + _dot(qk_scr[i], jnp.concatenate([v_new, v_new], axis=0)))

        for j in range(ns):
            sj = slice(j * W, (j + 1) * W)
            o = jnp.concatenate(outs[j], axis=0)
            o = o * lax.rsqrt(jnp.mean(o * o, axis=-1, keepdims=True) + NORM_EPS) * ng
            o_ref[pl.ds(s, R), sj] = (o * _silu(z_ref[pl.ds(s, R), sj].astype(F32))
                                      ).astype(o_ref.dtype)
        return (tuple(Ss), tuple(t[0] for t in raws), tuple(t[1] for t in raws),
                tuple(t[2] for t in raws))

    z8 = jnp.zeros((8, W), F32)
    init = (tuple(jnp.zeros((W, W), F32) for _ in range(ns)), (z8,) * ns, (z8,) * ns, (z8,) * ns)
    lax.fori_loop(0, nsb, body, init, unroll=4)


def _gdn(p, seq, conv_w, a_log, dt_bias, norm_g):
    n = p.shape[0]
    nbatch = n // seq
    ns = 4
    wide = ns * LANES
    ngrp = GDN_HEADS // ns
    assert COL_QKV % wide == 0 and COL_Z % wide == 0 and GDN_WIDTH % wide == 0
    qoff = COL_QKV // wide
    nb = 2 if seq % (4 * CHUNK) == 0 else 1

    def col(off):
        return pl.BlockSpec((seq, wide), lambda b, h, off=off: (b, off + h))

    def cw(off):
        return pl.BlockSpec((CONV_WIDTH, wide), lambda b, h, off=off: (0, off + h))

    one = pl.BlockSpec((1, LANES), lambda b, h: (0, 0))
    in_specs = [
        col(qoff), col(qoff + ngrp), col(qoff + 2 * ngrp), col(COL_Z // wide),
        pl.BlockSpec((seq, LANES), lambda b, h: (b, COL_AB // LANES)),
        cw(0), cw(ngrp), cw(2 * ngrp), one, one, one,
    ]
    n_steps = 2 * nb * ns
    return pl.pallas_call(
        functools.partial(_gdn_kernel, nb=nb, ns=ns),
        grid=(nbatch, ngrp),
        in_specs=in_specs,
        out_specs=pl.BlockSpec((seq, wide), lambda b, h: (b, h)),
        out_shape=jax.ShapeDtypeStruct((n, GDN_WIDTH), BF16),
        scratch_shapes=[
            pltpu.VMEM((n_steps, 2 * CHUNK + LANES, LANES), BF16),
            pltpu.VMEM((n_steps, CHUNK, LANES), F32),
            pltpu.VMEM((n_steps, CHUNK, LANES), BF16),
            pltpu.VMEM((n_steps, LANES, LANES), F32),
        ],
        compiler_params=pltpu.CompilerParams(
            dimension_semantics=("parallel", "parallel"), vmem_limit_bytes=VMEM_LIMIT),
        name="gdn",
    )(p, p, p, p, p, conv_w, conv_w, conv_w, a_log, dt_bias, norm_g)


def _mixout_kernel(x_ref, ya_ref, yb_ref, ga_ref, gb_ref, wpa_ref, wpb_ref, wout_ref, g2_ref,
                   *rest, with_router):
    if with_router:
        rt_ref, xo_ref, h_ref, comb_ref = rest
    else:
        xo_ref, h_ref = rest
    tm = x_ref.shape[0]
    ng = 2 if tm % 32 == 0 else 1
    grp = [slice(q * (tm // ng), (q + 1) * (tm // ng)) for q in range(ng)]
    yas = [_dot(ya_ref[r, :], wpa_ref[...]) for r in grp]
    ybs = [_dot(yb_ref[r, :], wpb_ref[...]) for r in grp]
    ys = [_sigmoid(ga_ref[r, :].astype(F32)) * ya + _sigmoid(gb_ref[r, :].astype(F32)) * yb
          for r, ya, yb in zip(grp, yas, ybs)]
    xns = [x_ref[r, :] + _dot(y, wout_ref[...]) for r, y in zip(grp, ys)]
    hs = []
    for r, xn in zip(grp, xns):
        xo_ref[r, :] = xn
        h = xn * lax.rsqrt(jnp.mean(xn * xn, axis=-1, keepdims=True) + NORM_EPS) * g2_ref[...]
        h_ref[r, :] = h.astype(h_ref.dtype)
        hs.append(h)
    if with_router:
        rt = rt_ref[...]
        r_hi = rt.astype(BF16)
        r_lo = (rt - r_hi.astype(F32)).astype(BF16)
        neg = jnp.float32(-jnp.inf)
        for r, h in zip(grp, hs):
            h_hi = h.astype(BF16)
            h_lo = (h - h_hi.astype(F32)).astype(BF16)
            logits = (jnp.dot(h_hi, r_hi, preferred_element_type=F32)
                      + jnp.dot(h_lo, r_hi, preferred_element_type=F32)
                      + jnp.dot(h_hi, r_lo, preferred_element_type=F32))
            lane = _iota(logits.shape, 1)
            logits = jnp.where(lane < N_EXPERTS, logits, neg)
            m1 = jnp.max(logits, axis=1, keepdims=True)
            i1 = jnp.min(jnp.where(logits == m1, lane, LANES), axis=1, keepdims=True)
            l2 = jnp.where(lane == i1, neg, logits)
            m2 = jnp.max(l2, axis=1, keepdims=True)
            i2 = jnp.min(jnp.where(l2 == m2, lane, LANES), axis=1, keepdims=True)
            e2 = jnp.exp(m2 - m1)
            g1 = 1.0 / (1.0 + e2)
            g2 = e2 / (1.0 + e2)
            comb_ref[r, :] = (jnp.where(lane == 0, i1.astype(F32), 0.0)
                              + jnp.where(lane == 1, i2.astype(F32), 0.0)
                              + jnp.where(lane == 2, g1, 0.0) + jnp.where(lane == 3, g2, 0.0))


def _mixout(x2, ya, yb, p, wpa, wpb, wout, g2, router, tm):
    n = x2.shape[0]
    with_router = router is not None
    row = lambda w: pl.BlockSpec((tm, w), lambda i: (i, 0))
    full = lambda a: pl.BlockSpec(a.shape, lambda i: (0, 0))
    in_specs = [
        row(D_MODEL), row(RWKV_WIDTH), row(GDN_WIDTH),
        pl.BlockSpec((tm, D_MODEL), lambda i: (i, COL_GATE // D_MODEL)),
        pl.BlockSpec((tm, D_MODEL), lambda i: (i, COL_GATE // D_MODEL + 1)),
        full(wpa), full(wpb), full(wout), full(g2),
    ]
    args = [x2, ya, yb, p, p, wpa, wpb, wout, g2]
    out_specs = [row(D_MODEL), row(D_MODEL)]
    out_shape = [jax.ShapeDtypeStruct((n, D_MODEL), F32),
                 jax.ShapeDtypeStruct((n, D_MODEL), F32 if with_router else BF16)]
    if with_router:
        in_specs.append(full(router))
        args.append(router)
        out_specs.append(row(LANES))
        out_shape.append(jax.ShapeDtypeStruct((n, LANES), F32))
    return pl.pallas_call(
        functools.partial(_mixout_kernel, with_router=with_router),
        grid=(n // tm,),
        in_specs=in_specs,
        out_specs=out_specs,
        out_shape=out_shape,
        compiler_params=pltpu.CompilerParams(
            dimension_semantics=("parallel",), vmem_limit_bytes=VMEM_LIMIT),
        name="mixout_router" if with_router else "mixout",
    )(*args)


def _rms_out(y, fg_ref):
    return y * lax.rsqrt(jnp.mean(y * y, axis=-1, keepdims=True) + NORM_EPS) * fg_ref[...]


def _gate_up(wg, wu, tf):
    parts = []
    for j in range(D_FF // tf):
        parts += [wg[..., j * tf:(j + 1) * tf], wu[..., j * tf:(j + 1) * tf]]
    return jnp.concatenate(parts, axis=-1).astype(BF16)


def _swiglu_chunk(h, wgu, wd):
    tf = wd.shape[0]
    gu = jnp.dot(h, wgu, preferred_element_type=F32)
    act = _silu(gu[:, 0:tf]) * gu[:, tf:]
    return jnp.dot(act.astype(BF16), wd, preferred_element_type=F32)


def _ffn_kernel(h_ref, x_ref, wgu_ref, wd_ref, *rest, final):
    fg_ref = rest[0] if final else None
    o_ref, acc_ref = rest[-2:]
    j = pl.program_id(1)

    @pl.when(j == 0)
    def _():
        acc_ref[...] = jnp.zeros_like(acc_ref)

    acc_ref[...] += _swiglu_chunk(h_ref[...], wgu_ref[...], wd_ref[...])

    @pl.when(j == pl.num_programs(1) - 1)
    def _():
        y = x_ref[...] + acc_ref[...]
        o_ref[...] = _rms_out(y, fg_ref) if final else y


def _ffn(h, x2, wgu, wd, final_g, tm, tf):
    n = x2.shape[0]
    final = final_g is not None
    row = lambda w: pl.BlockSpec((tm, w), lambda i, j: (i, 0))
    in_specs = [
        row(D_MODEL), row(D_MODEL),
        pl.BlockSpec((D_MODEL, 2 * tf), lambda i, j: (0, j)),
        pl.BlockSpec((tf, D_MODEL), lambda i, j: (j, 0)),
    ]
    args = [h, x2, wgu, wd]
    if final:
        in_specs.append(pl.BlockSpec((1, D_MODEL), lambda i, j: (0, 0)))
        args.append(final_g)
    return pl.pallas_call(
        functools.partial(_ffn_kernel, final=final),
        grid=(n // tm, D_FF // tf),
        in_specs=in_specs,
        out_specs=row(D_MODEL),
        out_shape=jax.ShapeDtypeStruct((n, D_MODEL), F32),
        scratch_shapes=[pltpu.VMEM((tm, D_MODEL), F32)],
        compiler_params=pltpu.CompilerParams(
            dimension_semantics=("parallel", "arbitrary"), vmem_limit_bytes=VMEM_LIMIT),
        name="ffn",
    )(*args)


MOE_ROWS = 512


def _plan_kernel(sel_ref, pos_ref, tile_ref, cnt_scr, *, tm, n_tiles):
    ph = pl.program_id(0)
    i = pl.program_id(1)
    sel = sel_ref[...]
    lane = _iota((tm, LANES), 1)
    e1 = jnp.sum(jnp.where(lane == 0, sel, 0.0), axis=1, keepdims=True).astype(jnp.int32)
    e2 = jnp.sum(jnp.where(lane == 1, sel, 0.0), axis=1, keepdims=True).astype(jnp.int32)
    oh1 = lane == e1
    oh2 = lane == e2
    both = jnp.where(oh1, 1.0, 0.0) + jnp.where(oh2, 1.0, 0.0)

    @pl.when((ph == 0) & (i == 0))
    def _():
        cnt_scr[...] = jnp.zeros_like(cnt_scr)

    @pl.when(ph == 0)
    def _():
        cnt_scr[0:1, :] += jnp.sum(both, axis=0, keepdims=True)

    @pl.when((ph == 1) & (i == 0))
    def _():
        cnt = cnt_scr[0:1, :]
        padded = jnp.floor((cnt + (MOE_ROWS - 1)) * (1.0 / MOE_ROWS)) * MOE_ROWS
        rl = _iota((LANES, LANES), 0)
        cl = _iota((LANES, LANES), 1)
        upper = jnp.where(rl < cl, 1.0, 0.0)
        starts = _dot_hi(jnp.broadcast_to(padded, (8, LANES)), upper)[0:1, :]
        ends = starts + padded
        cnt_scr[0:1, :] = starts
        tl = _iota((n_tiles, LANES), 1)
        t0 = (_iota((n_tiles, LANES), 0) * MOE_ROWS).astype(F32)
        lane_ok = tl < N_EXPERTS
        owner = jnp.sum(jnp.where(lane_ok, jnp.where(t0 >= ends, 1.0, 0.0), 0.0),
                        axis=1, keepdims=True)
        used = jnp.sum(jnp.where(lane_ok, padded, 0.0), axis=1, keepdims=True)
        valid = t0[:, 0:1] < used
        last_e = jnp.max(jnp.where(lane_ok, jnp.where(padded > 0.0, tl.astype(F32), 0.0), 0.0),
                         axis=1, keepdims=True)
        ex = jnp.where(valid, jnp.minimum(owner, N_EXPERTS - 1.0), last_e)
        tile_ref[...] = (jnp.where(tl == 0, ex, 0.0)
                         + jnp.where(tl == 1, jnp.where(valid, 1.0, 0.0), 0.0)).astype(jnp.int32)

    @pl.when(ph == 1)
    def _():
        rr = _iota((tm, tm), 0)
        cc = _iota((tm, tm), 1)
        before = jnp.where(cc < rr, 1.0, 0.0).astype(BF16)
        base = cnt_scr[0:1, :] + jnp.dot(before, both.astype(BF16), preferred_element_type=F32)
        p1 = jnp.sum(jnp.where(oh1, base, 0.0), axis=1, keepdims=True)
        p2 = jnp.sum(jnp.where(oh2, base, 0.0), axis=1, keepdims=True)
        pos_ref[...] = (jnp.where(lane == 0, p1, 0.0) + jnp.where(lane == 1, p2, 0.0)
                        ).astype(jnp.int32)
        cnt_scr[0:1, :] += jnp.sum(both, axis=0, keepdims=True)


def _plan(sel, tm, n_tiles):
    n = sel.shape[0]
    return pl.pallas_call(
        functools.partial(_plan_kernel, tm=tm, n_tiles=n_tiles),
        grid=(2, n // tm),
        in_specs=[pl.BlockSpec((tm, LANES), lambda ph, i: (i, 0))],
        out_specs=[pl.BlockSpec((tm, LANES), lambda ph, i: (i * ph, 0)),
                   pl.BlockSpec((n_tiles, LANES), lambda ph, i: (0, 0))],
        out_shape=[jax.ShapeDtypeStruct((n, LANES), jnp.int32),
                   jax.ShapeDtypeStruct((n_tiles, LANES), jnp.int32)],
        scratch_shapes=[pltpu.VMEM((8, LANES), F32)],
        compiler_params=pltpu.CompilerParams(
            dimension_semantics=("arbitrary", "arbitrary"), vmem_limit_bytes=VMEM_LIMIT),
        name="moe_plan",
    )(sel)


def _row_copy(src_ref, src_row, dst_ref, dst_row, sem):
    return pltpu.make_async_copy(src_ref.at[pl.ds(src_row, 1)], dst_ref.at[pl.ds(dst_row, 1)], sem)


def _dispatch_kernel(pos_ref, h_ref, xs_in_ref, xs_ref, sem, *, tm):
    del xs_in_ref

    def start(r, c):
        _row_copy(h_ref, r, xs_ref, pos_ref[2 * r], sem).start(priority=0)
        _row_copy(h_ref, r, xs_ref, pos_ref[2 * r + 1], sem).start(priority=1)
        return c

    def wait(r, c):
        _row_copy(h_ref, 0, xs_ref, 0, sem).wait()
        _row_copy(h_ref, 0, xs_ref, 0, sem).wait()
        return c

    for r in range(tm):
        start(r, 0)
    lax.fori_loop(0, tm, wait, 0, unroll=8)


def _dispatch(pos_flat, h, n_rows, tm):
    n = h.shape[0]
    xs0 = jnp.zeros((n_rows, D_MODEL), F32)
    return pl.pallas_call(
        functools.partial(_dispatch_kernel, tm=tm),
        grid=(n // tm,),
        in_specs=[
            pl.BlockSpec((2 * tm,), lambda i: (i,), memory_space=pltpu.SMEM),
            pl.BlockSpec((tm, D_MODEL), lambda i: (i, 0)),
            pl.BlockSpec(memory_space=pl.ANY),
        ],
        out_specs=pl.BlockSpec(memory_space=pl.ANY),
        out_shape=jax.ShapeDtypeStruct((n_rows, D_MODEL), F32),
        scratch_shapes=[pltpu.SemaphoreType.DMA(())],
        input_output_aliases={2: 0},
        compiler_params=pltpu.CompilerParams(
            dimension_semantics=("arbitrary",), vmem_limit_bytes=VMEM_LIMIT),
        name="moe_dispatch",
    )(pos_flat, h, xs0)


def _experts_kernel(te_ref, tv_ref, xs_ref, wgu_ref, wd_ref, ys_ref, xb_scr, acc_scr):
    t = pl.program_id(0)
    j = pl.program_id(1)
    del te_ref

    @pl.when(j == 0)
    def _():
        xb_scr[...] = xs_ref[...].astype(BF16)
        acc_scr[...] = jnp.zeros_like(acc_scr)

    @pl.when(tv_ref[t] == 1)
    def _():
        acc_scr[...] += _swiglu_chunk(xb_scr[...], wgu_ref[0], wd_ref[0])

    @pl.when(j == pl.num_programs(1) - 1)
    def _():
        ys_ref[...] = acc_scr[...]


def _experts(tile_e, tile_v, xs, wgu, wd, tf):
    n_rows = xs.shape[0]
    grid_spec = pltpu.PrefetchScalarGridSpec(
        num_scalar_prefetch=2,
        grid=(n_rows // MOE_ROWS, D_FF // tf),
        in_specs=[
            pl.BlockSpec((MOE_ROWS, D_MODEL), lambda t, j, te, tv: (t, 0)),
            pl.BlockSpec((1, D_MODEL, 2 * tf), lambda t, j, te, tv: (te[t], 0, j)),
            pl.BlockSpec((1, tf, D_MODEL), lambda t, j, te, tv: (te[t], j, 0)),
        ],
        out_specs=pl.BlockSpec((MOE_ROWS, D_MODEL), lambda t, j, te, tv: (t, 0)),
        scratch_shapes=[pltpu.VMEM((MOE_ROWS, D_MODEL), BF16), pltpu.VMEM((MOE_ROWS, D_MODEL), F32)],
    )
    return pl.pallas_call(
        _experts_kernel,
        grid_spec=grid_spec,
        out_shape=jax.ShapeDtypeStruct((n_rows, D_MODEL), F32),
        compiler_params=pltpu.CompilerParams(
            dimension_semantics=("parallel", "arbitrary"), vmem_limit_bytes=VMEM_LIMIT),
        name="moe_experts",
    )(tile_e, tile_v, xs, wgu, wd)


def _combine_kernel(pos_ref, x_ref, sel_ref, *rest, tm, final):
    fg_ref = rest[0] if final else None
    ys_ref, o_ref, buf, sem = rest[-4:]

    def start(r, c):
        _row_copy(ys_ref, pos_ref[2 * r], buf.at[0], r, sem).start(priority=0)
        _row_copy(ys_ref, pos_ref[2 * r + 1], buf.at[1], r, sem).start(priority=1)
        return c

    def wait(r, c):
        _row_copy(ys_ref, 0, buf.at[0], 0, sem).wait()
        _row_copy(ys_ref, 0, buf.at[1], 0, sem).wait()
        return c

    for r in range(tm):
        start(r, 0)
    lax.fori_loop(0, tm, wait, 0, unroll=8)
    sel = sel_ref[...]
    lane = _iota(sel.shape, 1)
    g1 = jnp.sum(jnp.where(lane == 2, sel, 0.0), axis=1, keepdims=True)
    g2 = jnp.sum(jnp.where(lane == 3, sel, 0.0), axis=1, keepdims=True)
    y = x_ref[...] + g1 * buf[0] + g2 * buf[1]
    o_ref[...] = _rms_out(y, fg_ref) if final else y


def _combine(pos_flat, x2, sel, ys, final_g, tm):
    n = x2.shape[0]
    final = final_g is not None
    in_specs = [
        pl.BlockSpec((2 * tm,), lambda i: (i,), memory_space=pltpu.SMEM),
        pl.BlockSpec((tm, D_MODEL), lambda i: (i, 0)),
        pl.BlockSpec((tm, LANES), lambda i: (i, 0)),
    ]
    args = [pos_flat, x2, sel]
    if final:
        in_specs.append(pl.BlockSpec((1, D_MODEL), lambda i: (0, 0)))
        args.append(final_g)
    in_specs.append(pl.BlockSpec(memory_space=pl.ANY))
    args.append(ys)
    return pl.pallas_call(
        functools.partial(_combine_kernel, tm=tm, final=final),
        grid=(n // tm,),
        in_specs=in_specs,
        out_specs=pl.BlockSpec((tm, D_MODEL), lambda i: (i, 0)),
        out_shape=jax.ShapeDtypeStruct((n, D_MODEL), F32),
        scratch_shapes=[pltpu.VMEM((2, tm, D_MODEL), F32), pltpu.SemaphoreType.DMA(())],
        compiler_params=pltpu.CompilerParams(
            dimension_semantics=("arbitrary",), vmem_limit_bytes=VMEM_LIMIT),
        name="moe_combine",
    )(*args)


def _moe(h, x2, sel, wgu, wd, final_g, tm, tf):
    n = x2.shape[0]
    n_tiles = pl.cdiv(2 * n, MOE_ROWS) + N_EXPERTS
    pos, tiles = _plan(sel, tm, n_tiles)
    pos_flat = pos[:, 0:2].reshape(2 * n)
    xs = _dispatch(pos_flat, h, n_tiles * MOE_ROWS, tm)
    ys = _experts(tiles[:, 0], tiles[:, 1], xs, wgu, wd, tf)
    return _combine(pos_flat, x2, sel, ys, final_g, tm)


def _tile(n, pref):
    t = min(pref, n)
    while n % t:
        t //= 2
    return t


def _pad_lanes(a, width=LANES):
    return jnp.pad(a, ((0, 0), (0, width - a.shape[-1])))


def kernel(x, norm1_g, w_in, tshift_mu, w0, w_decay_up, a0, w_iclr_up, w_gate_up, k_k, k_a, r_k,
           lnx_g, lnx_b, w_pa, conv_w, a_log, dt_bias, gdn_norm_g, w_pb, w_out, norm2_g,
           ffn_w_gate, ffn_w_up, ffn_w_down, moe_router, moe_w_gate, moe_w_up, moe_w_down, final_g):
    nb, seq, d = x.shape
    n = nb * seq
    x2 = x.reshape(n, d)
    tm_in = _tile(n, 1024)
    tm = _tile(n, 512)
    rw = 1792
    gq = 3 * GDN_WIDTH
    for layer in range(DEPTH):
        wi = w_in[layer]
        ab = wi[:, rw + gq:rw + gq + 2 * GDN_HEADS]
        zc = wi[:, rw + gq + 2 * GDN_HEADS:rw + gq + 2 * GDN_HEADS + GDN_WIDTH]
        gate = wi[:, rw + gq + 2 * GDN_HEADS + GDN_WIDTH:]
        w_cat = jnp.concatenate(
            [wi[:, :rw], ab, jnp.zeros((d, COL_QKV - COL_AB - 2 * GDN_HEADS), F32),
             wi[:, rw:rw + gq], zc, gate],
            axis=1).astype(BF16)
        p = _inproj(x2, norm1_g[layer][None, :], w_cat, tm_in, P_COLS // 2)
        one = lambda a: a[layer][None, :]
        ya = _rwkv(p, seq, one(tshift_mu), one(w0), one(a0), one(k_k), one(k_a), one(r_k),
                   one(lnx_g), one(lnx_b), w_decay_up[layer].astype(BF16),
                   w_iclr_up[layer].astype(BF16), w_gate_up[layer].astype(BF16))
        yb = _gdn(p, seq, conv_w[layer], _pad_lanes(one(a_log)), _pad_lanes(one(dt_bias)),
                  one(gdn_norm_g))
        j = layer // 2
        routed = layer % 2 == 1
        router = _pad_lanes(moe_router[j]) if routed else None
        res = _mixout(x2, ya, yb, p, w_pa[layer].astype(BF16), w_pb[layer].astype(BF16),
                      w_out[layer].astype(BF16), one(norm2_g), router, tm)
        fg = final_g[None, :] if layer == DEPTH - 1 else None
        if routed:
            x2, h, sel = res
            x2 = _moe(h, x2, sel, _gate_up(moe_w_gate[j], moe_w_up[j], FF_CHUNK),
                      moe_w_down[j].astype(BF16), fg, tm, FF_CHUNK)
        else:
            x2, h = res
            x2 = _ffn(h, x2, _gate_up(ffn_w_gate[j], ffn_w_up[j], FF_CHUNK),
                      ffn_w_down[j].astype(BF16), fg, tm, FF_CHUNK)
    return x2.reshape(nb, seq, d)
```
